```python
import math
import jax, jax.numpy as jnp
from jax import lax
import numpy as np

D_MODEL = 1024
BATCH = 32
SEQ = 2048
DEPTH = 1

N_HEADS = 8
HEAD_DIM = 64
N_KV_HEADS = 2
IDX_HEADS = 8
IDX_DIM = 64
TOPK_MAX = 256
Q_BLOCK = 64
N_BUCKETS = 32
MAX_DISTANCE = 128
D_RNN = D_MODEL
N_RNN_BLOCKS = 8
RNN_BLOCK = D_RNN // N_RNN_BLOCKS
RNN_CONV = 4
LRU_C = 8.0
D_FF = 2816
FFN_CONV = 3
EPS = 1e-6

ATTN_WIDTH = N_HEADS * HEAD_DIM
KV_WIDTH = N_KV_HEADS * HEAD_DIM
SPLITS = (ATTN_WIDTH, KV_WIDTH, KV_WIDTH, IDX_HEADS * IDX_DIM, IDX_DIM, IDX_HEADS, D_RNN, D_RNN, D_MODEL, D_MODEL)
N_IN = 5448

kernel_name = "hybrid_dsa_rglru_convffn_block"


def rms_norm(x, g):
    x32 = x.astype(jnp.float32)
    y = x32 * lax.rsqrt(jnp.mean(x32 * x32, axis=-1, keepdims=True) + EPS)
    return (y * g.astype(jnp.float32)).astype(x.dtype)


def causal_dwconv(x, w, b):
    width = w.shape[0]
    seq = x.shape[1]
    xp = jnp.pad(x, ((0, 0), (width - 1, 0), (0, 0)))
    y = xp[:, 0:seq] * w[0]
    for k in range(1, width):
        y = y + xp[:, k:k + seq] * w[k]
    return y + b


def t5_bucket(rel):
    max_exact = N_BUCKETS // 2
    n = jnp.maximum(rel, 0)
    nf = jnp.maximum(n, 1).astype(jnp.float32)
    large = max_exact + (jnp.log(nf / max_exact) / math.log(MAX_DISTANCE / max_exact)
                         * (N_BUCKETS - max_exact)).astype(jnp.int32)
    large = jnp.minimum(large, N_BUCKETS - 1)
    return jnp.where(n < max_exact, n, large)


def dsa_attention(q, k, v, q_idx, k_idx, w_idx, rel_bias):
    B, S = q.shape[0], q.shape[1]
    n_sel = min(TOPK_MAX, S // 4)
    nb = S // Q_BLOCK
    G = N_HEADS // N_KV_HEADS
    pos = jnp.arange(S, dtype=jnp.int32)
    gather = jax.vmap(lambda t_b, i_b: t_b[i_b])

    def to_blocks(t):
        return jnp.moveaxis(t.reshape((B, nb, Q_BLOCK) + t.shape[2:]), 1, 0)

    def block(args):
        qb, qib, wb, tb = args
        dots = jnp.einsum('bqhd,bsd->bqhs', qib, k_idx).astype(jnp.float32) * (IDX_DIM ** -0.5)
        score = jnp.einsum('bqh,bqhs->bqs', wb.astype(jnp.float32) * (IDX_HEADS ** -0.5), jax.nn.relu(dots))
        causal = pos[None, :] <= tb[:, None]
        score = jnp.where(causal[None], score, -jnp.inf)
        _, sel = lax.top_k(score, n_sel)
        valid = sel <= tb[None, :, None]
        ks = gather(k, sel)
        vs = gather(v, sel)
        qg = qb.reshape(B, Q_BLOCK, N_KV_HEADS, G, HEAD_DIM)
        logits = jnp.einsum('bqhgd,bqnhd->bqhgn', qg, ks).astype(jnp.float32) * (HEAD_DIM ** -0.5)
        bucket = t5_bucket(tb[None, :, None] - sel)
        bias = rel_bias[bucket].astype(jnp.float32)
        bias = bias.reshape(B, Q_BLOCK, n_sel, N_KV_HEADS, G).transpose(0, 1, 3, 4, 2)
        logits = jnp.where(valid[:, :, None, None, :], logits + bias, -jnp.inf)
        p = jax.nn.softmax(logits, axis=-1).astype(vs.dtype)
        o = jnp.einsum('bqhgn,bqnhd->bqhgd', p, vs)
        return o.reshape(B, Q_BLOCK, N_HEADS * HEAD_DIM)

    out = lax.map(block, (to_blocks(q), to_blocks(q_idx), to_blocks(w_idx), pos.reshape(nb, Q_BLOCK)))
    return jnp.moveaxis(out, 0, 1).reshape(B, S, N_HEADS * HEAD_DIM)


def rg_lru(x, w_rg, b_rg, w_ig, b_ig, lam):
    B, S = x.shape[0], x.shape[1]
    xb = x.reshape(B, S, N_RNN_BLOCKS, RNN_BLOCK)
    r = jax.nn.sigmoid(jnp.einsum('bsnc,ncd->bsnd', xb, w_rg).reshape(B, S, D_RNN) + b_rg)
    i = jax.nn.sigmoid(jnp.einsum('bsnc,ncd->bsnd', xb, w_ig).reshape(B, S, D_RNN) + b_ig)
    log_a = -LRU_C * r.astype(jnp.float32) * jax.nn.softplus(-lam.astype(jnp.float32))
    a = jnp.exp(log_a)
    mult = jnp.sqrt(-jnp.expm1(2.0 * log_a))
    u = x.astype(jnp.float32) * i.astype(jnp.float32) * mult

    def combine(left, right):
        a1, b1 = left
        a2, b2 = right
        return a1 * a2, a2 * b1 + b2

    _, h = lax.associative_scan(combine, (a, u), axis=1)
    return h.astype(x.dtype)


def setup_inputs(seed: int = 0) -> dict:
    key = jax.random.key(seed)
    ks = jax.random.split(key, 32)
    f32 = jnp.float32

    def nrm(k, shape, scale):
        return jax.random.normal(k, shape, f32) * scale

    L = DEPTH
    a0 = jax.random.uniform(ks[13], (L, D_RNN), f32, 0.9, 0.999)
    p = a0 ** (1.0 / LRU_C)
    lam = jnp.log(p) - jnp.log1p(-p)
    return {
        "x": nrm(ks[0], (BATCH, SEQ, D_MODEL), 1.0),
        "c": nrm(ks[1], (BATCH, D_MODEL), 1.0),
        "w_ada": nrm(ks[2], (L, D_MODEL, 6 * D_MODEL), D_MODEL ** -0.5),
        "b_ada": nrm(ks[3], (L, 6 * D_MODEL), 0.02),
        "g_mix": 1.0 + nrm(ks[4], (L, D_MODEL), 0.02),
        "w_in": nrm(ks[5], (L, D_MODEL, N_IN), D_MODEL ** -0.5),
        "b_in": nrm(ks[6], (L, N_IN), 0.01),
        "rel_bias": nrm(ks[7], (N_BUCKETS, N_HEADS), 0.5),
        "conv_rnn_w": nrm(ks[8], (L, RNN_CONV, D_RNN), RNN_CONV ** -0.5),
        "conv_rnn_b": nrm(ks[9], (L, D_RNN), 0.01),
        "w_rg": nrm(ks[10], (L, N_RNN_BLOCKS, RNN_BLOCK, RNN_BLOCK), RNN_BLOCK ** -0.5),
        "b_rg": nrm(ks[11], (L, D_RNN), 0.01),
        "w_ig": nrm(ks[12], (L, N_RNN_BLOCKS, RNN_BLOCK, RNN_BLOCK), RNN_BLOCK ** -0.5),
        "b_ig": nrm(ks[14], (L, D_RNN), 0.01),
        "lru_lambda": lam,
        "w_o_attn": nrm(ks[15], (L, ATTN_WIDTH, D_MODEL), ATTN_WIDTH ** -0.5),
        "w_o_rnn": nrm(ks[16], (L, D_RNN, D_MODEL), D_RNN ** -0.5),
        "w_out": nrm(ks[17], (L, D_MODEL, D_MODEL), D_MODEL ** -0.5),
        "g_ffn": 1.0 + nrm(ks[18], (L, D_MODEL), 0.02),
        "w_up": nrm(ks[19], (L, D_MODEL, 2 * D_FF), D_MODEL ** -0.5),
        "conv_ffn_w": nrm(ks[20], (L, FFN_CONV, 2 * D_FF), FFN_CONV ** -0.5),
        "conv_ffn_b": nrm(ks[21], (L, 2 * D_FF), 0.01),
        "w_down": nrm(ks[22], (L, D_FF, D_MODEL), D_FF ** -0.5),
        "g_final": 1.0 + nrm(ks[23], (D_MODEL,), 0.02),
    }


def reference(x, c, w_ada, b_ada, g_mix, w_in, b_in, rel_bias, conv_rnn_w, conv_rnn_b,
              w_rg, b_rg, w_ig, b_ig, lru_lambda, w_o_attn, w_o_rnn, w_out, g_ffn,
              w_up, conv_ffn_w, conv_ffn_b, w_down, g_final):
    B, S = x.shape[0], x.shape[1]
    cuts = []
    acc = 0
    for width in SPLITS[:-1]:
        acc += width
        cuts.append(acc)
    c_act = jax.nn.silu(c)
    h = x
    for l in range(DEPTH):
        mod = jnp.dot(c_act, w_ada[l]) + b_ada[l]
        sh1, sc1, ga1, sh2, sc2, ga2 = [m[:, None, :] for m in jnp.split(mod, 6, axis=-1)]

        xn = rms_norm(h, g_mix[l]) * (1.0 + sc1) + sh1
        proj = jnp.dot(xn, w_in[l]) + b_in[l]
        q, k, v, qi, ki, wi, xr, yr, gate_a, gate_b = jnp.split(proj, cuts, axis=-1)

        attn = dsa_attention(
            q.reshape(B, S, N_HEADS, HEAD_DIM),
            k.reshape(B, S, N_KV_HEADS, HEAD_DIM),
            v.reshape(B, S, N_KV_HEADS, HEAD_DIM),
            qi.reshape(B, S, IDX_HEADS, IDX_DIM), ki, wi, rel_bias)

        xr = causal_dwconv(xr, conv_rnn_w[l], conv_rnn_b[l])
        rnn = rg_lru(xr, w_rg[l], b_rg[l], w_ig[l], b_ig[l], lru_lambda[l]) * jax.nn.gelu(yr)

        merged = (jax.nn.sigmoid(gate_a) * jnp.dot(attn, w_o_attn[l])
                  + jax.nn.sigmoid(gate_b) * jnp.dot(rnn, w_o_rnn[l]))
        h = h + ga1 * jnp.dot(merged, w_out[l])

        xn = rms_norm(h, g_ffn[l]) * (1.0 + sc2) + sh2
        up = causal_dwconv(jnp.dot(xn, w_up[l]), conv_ffn_w[l], conv_ffn_b[l])
        val, gte = jnp.split(up, 2, axis=-1)
        h = h + ga2 * jnp.dot(jax.nn.silu(gte) * val, w_down[l])
    return rms_norm(h, g_final)
```

```python
import functools
import math

import numpy as np
import jax
import jax.numpy as jnp
from jax import lax
from jax.experimental import pallas as pl
from jax.experimental.pallas import tpu as pltpu

N_HEADS = 8
HEAD_DIM = 64
N_KV_HEADS = 2
IDX_HEADS = 8
IDX_DIM = 64
TOPK_MAX = 256
N_BUCKETS = 32
MAX_DISTANCE = 128
N_RNN_BLOCKS = 8
RNN_CONV = 4
LRU_C = 8.0
FFN_CONV = 3
EPS = 1e-6

LANES = 128
SUBLANES = 8
VMEM_LIMIT = 56 * 1024 * 1024

MASK_NEG = -1e30
KEY_NEG_INF = -2139095041

BF16 = jnp.bfloat16
F32 = jnp.float32


def _cparams(n_grid):
    return pltpu.CompilerParams(
        dimension_semantics=("arbitrary",) * n_grid, vmem_limit_bytes=VMEM_LIMIT)


def _resident(shape):
    nd = len(shape)
    return pl.BlockSpec(shape, lambda *_: (0,) * nd, pipeline_mode=pl.Buffered(1))


def _dot(a, b):
    return jnp.dot(a, b, preferred_element_type=F32)


def _dot_nt(a, b):
    return lax.dot_general(a, b, (((1,), (1,)), ((), ())), preferred_element_type=F32)


def _adaln_kernel(c_ref, w_ref, b_ref, o_ref):
    c = c_ref[...]
    c_act = (c * jax.nn.sigmoid(c)).astype(BF16)
    o_ref[...] = _dot(c_act, w_ref[...].astype(BF16)) + b_ref[...]


def _adaln(c, w, b):
    bsz, d = c.shape
    n = w.shape[1]
    return pl.pallas_call(
        _adaln_kernel,
        grid=(n // d,),
        in_specs=[pl.BlockSpec((bsz, d), lambda j: (0, 0)),
                  pl.BlockSpec((d, d), lambda j: (0, j)),
                  pl.BlockSpec((1, d), lambda j: (0, j))],
        out_specs=pl.BlockSpec((bsz, d), lambda j: (0, j)),
        out_shape=jax.ShapeDtypeStruct((bsz, n), F32),
        compiler_params=_cparams(1),
        name="adaln",
    )(c, w, b.reshape(1, n))


def _modulated_norm(x, g, sc, sh):
    ms = jnp.mean(x * x, axis=-1, keepdims=True)
    return (x * lax.rsqrt(ms + EPS) * g) * (1.0 + sc) + sh


def _in_proj_kernel(x_ref, sc_ref, sh_ref, g_ref, wa_ref, ba_ref, wb_ref, bb_ref,
                    wr_ref, br_ref, wy_ref, by_ref, wga_ref, bga_ref, wgb_ref, bgb_ref,
                    q_ref, k_ref, v_ref, qi_ref, ki_ref, wi_ref, xr_ref, gy_ref, sga_ref, sgb_ref):
    xn = _modulated_norm(x_ref[...], g_ref[...], sc_ref[...], sh_ref[...]).astype(BF16)

    res = _dot(xn, wa_ref[...]) + ba_ref[...]
    for h in range(N_HEADS):
        q_ref[h] = (res[:, h * HEAD_DIM:(h + 1) * HEAD_DIM] * (HEAD_DIM ** -0.5)).astype(BF16)
    k0 = N_HEADS * HEAD_DIM
    v0 = k0 + N_KV_HEADS * HEAD_DIM
    for j in range(N_KV_HEADS):
        k_ref[j] = res[:, k0 + j * HEAD_DIM:k0 + (j + 1) * HEAD_DIM].astype(BF16)
        v_ref[j] = res[:, v0 + j * HEAD_DIM:v0 + (j + 1) * HEAD_DIM].astype(BF16)

    res = _dot(xn, wb_ref[...]) + bb_ref[...]
    for h in range(IDX_HEADS):
        qi_ref[h] = (res[:, h * IDX_DIM:(h + 1) * IDX_DIM] * (IDX_DIM ** -0.5)).astype(BF16)
    ki0 = IDX_HEADS * IDX_DIM
    ki_ref[...] = res[:, ki0:ki0 + IDX_DIM].astype(BF16)
    wi_ref[...] = res[:, ki0 + IDX_DIM:ki0 + IDX_DIM + IDX_HEADS] * (IDX_HEADS ** -0.5)

    xr_ref[...] = (_dot(xn, wr_ref[...]) + br_ref[...]).astype(BF16)
    gy_ref[...] = jax.nn.gelu(_dot(xn, wy_ref[...]) + by_ref[...]).astype(BF16)
    sga_ref[...] = jax.nn.sigmoid(_dot(xn, wga_ref[...]) + bga_ref[...]).astype(BF16)
    sgb_ref[...] = jax.nn.sigmoid(_dot(xn, wgb_ref[...]) + bgb_ref[...]).astype(BF16)


def _in_proj(x, sc, sh, g, weights, tile):
    bsz, seq, d = x.shape
    row = lambda w: pl.BlockSpec((None, tile, w), lambda b, s: (b, s, 0))
    heads = lambda n, w: pl.BlockSpec((None, n, tile, w), lambda b, s: (b, 0, s, 0))
    per_batch = pl.BlockSpec((None, 1, d), lambda b, s: (b, 0, 0))
    in_specs = [row(d), per_batch, per_batch, _resident((1, d))]
    in_specs += [_resident(w.shape) for w in weights]
    bsd = lambda w, dt: jax.ShapeDtypeStruct((bsz, seq, w), dt)
    out_shape = [
        jax.ShapeDtypeStruct((bsz, N_HEADS, seq, HEAD_DIM), BF16),
        jax.ShapeDtypeStruct((bsz, N_KV_HEADS, seq, HEAD_DIM), BF16),
        jax.ShapeDtypeStruct((bsz, N_KV_HEADS, seq, HEAD_DIM), BF16),
        jax.ShapeDtypeStruct((bsz, IDX_HEADS, seq, IDX_DIM), BF16),
        bsd(IDX_DIM, BF16), bsd(IDX_HEADS, F32),
        bsd(d, BF16), bsd(d, BF16), bsd(d, BF16), bsd(d, BF16)]
    out_specs = [heads(N_HEADS, HEAD_DIM), heads(N_KV_HEADS, HEAD_DIM), heads(N_KV_HEADS, HEAD_DIM),
                 heads(IDX_HEADS, IDX_DIM), row(IDX_DIM), row(IDX_HEADS),
                 row(d), row(d), row(d), row(d)]
    return pl.pallas_call(
        _in_proj_kernel,
        grid=(bsz, seq // tile),
        in_specs=in_specs, out_specs=out_specs, out_shape=out_shape,
        compiler_params=_cparams(2),
        name="in_proj",
    )(x, sc, sh, g, *weights)


def _t5_bucket_np(rel):
    max_exact = N_BUCKETS // 2
    n = np.maximum(rel, 0)
    nf = np.maximum(n, 1).astype(np.float64)
    large = max_exact + (np.log(nf / max_exact) / math.log(MAX_DISTANCE / max_exact)
                         * (N_BUCKETS - max_exact)).astype(np.int32)
    large = np.minimum(large, N_BUCKETS - 1)
    return np.where(n < max_exact, n, large).astype(np.int32)


def _far_distance():
    d = np.arange(0, 4 * MAX_DISTANCE)
    b = _t5_bucket_np(d)
    assert b[-1] == N_BUCKETS - 1
    return int(np.max(np.nonzero(b != N_BUCKETS - 1)[0])) + 1


def _sort_key(s):
    bits = lax.bitcast_convert_type(s, jnp.int32)
    return bits ^ ((bits >> 31) & jnp.int32(0x7FFFFFFF))


def _attn_kernel(q_ref, qi_ref, wi_ref, k_ref, v_ref, ki_ref, bucket_ref, relb_ref, tri_ref,
                 o_ref, key_ref, madd_ref, bias_ref, *, tq, n_sel, n_near):
    b = pl.program_id(0)
    qt = pl.program_id(1)
    n_chunks = qt + 1
    tk = tq

    @pl.when((b == 0) & (qt == 0))
    def _():
        for dlt in range(n_near + 1):
            bucket = bucket_ref[dlt]
            for h in range(N_HEADS):
                tile = jnp.full((tq, tk), relb_ref[N_BUCKETS - 1, h], F32)
                for bk in range(N_BUCKETS - 1):
                    tile = jnp.where(bucket == bk, relb_ref[bk, h], tile)
                bias_ref[dlt * N_HEADS + h] = tile

    row = lax.broadcasted_iota(jnp.int32, (tq, tk), 0)
    col = lax.broadcasted_iota(jnp.int32, (tq, tk), 1)

    wi = wi_ref[...]

    def score_chunk(c, carry):
        off = pl.multiple_of(c * tk, tk)
        kic = ki_ref[pl.ds(off, tk), :]
        s = None
        for h in range(IDX_HEADS):
            d = _dot_nt(qi_ref[h], kic)
            term = wi[:, h:h + 1] * jnp.maximum(d, 0.0)
            s = term if s is None else s + term
        s = jnp.where((c < qt) | (col <= row), s, -jnp.inf)
        key_ref[:, pl.ds(off, tk)] = _sort_key(s)
        return carry

    lax.fori_loop(0, n_chunks, score_chunk, 0)

    t_row = qt * tq + lax.broadcasted_iota(jnp.int32, (tq, LANES), 0)
    need = jnp.minimum(t_row + 1, n_sel)
    lo0 = jnp.full((tq, LANES), KEY_NEG_INF + 1, jnp.int32)
    hi0 = jnp.full((tq, LANES), jnp.iinfo(jnp.int32).max, jnp.int32)
    cnt_lo0 = t_row + 1
    cnt_hi0 = jnp.zeros((tq, LANES), jnp.int32)

    def count_ge(mid):
        def body(c, acc):
            off = pl.multiple_of(c * tk, tk)
            kc = key_ref[:, pl.ds(off, tk)]
            for j in range(tk // LANES):
                acc = acc + jnp.where(kc[:, j * LANES:(j + 1) * LANES] >= mid, 1, 0)
            return acc
        acc = lax.fori_loop(0, n_chunks, body, jnp.zeros((tq, LANES), jnp.int32))
        return jnp.broadcast_to(jnp.sum(acc, axis=1, keepdims=True), (tq, LANES))

    def bisect(_, st):
        lo, hi, cnt_lo, cnt_hi = st
        mid = (lo >> 1) + (hi >> 1) + (lo & hi & 1)
        cnt = count_ge(mid)
        go_up = (cnt >= need) & (mid > lo)
        go_dn = cnt < need
        lo = jnp.where(go_up, mid, lo)
        cnt_lo = jnp.where(go_up, cnt, cnt_lo)
        hi = jnp.where(go_dn, mid, hi)
        cnt_hi = jnp.where(go_dn, cnt, cnt_hi)
        return lo, hi, cnt_lo, cnt_hi

    search = (qt + 1) * tq > n_sel
    n_steps = jnp.where(search, 32, 0)
    lo, hi, cnt_lo, cnt_hi = lax.fori_loop(0, n_steps, bisect, (lo0, hi0, cnt_lo0, cnt_hi0))
    tie_quota = (need - cnt_hi).astype(F32)
    has_tie = jnp.max(jnp.where(cnt_lo > need, 1, 0)) > 0

    @pl.when(jnp.logical_not(has_tie))
    def _():
        def body(c, carry):
            off = pl.multiple_of(c * tk, tk)
            kc = key_ref[:, pl.ds(off, tk)]
            for j in range(tk // LANES):
                sel = kc[:, j * LANES:(j + 1) * LANES] >= lo
                madd_ref[:, pl.ds(pl.multiple_of(off + j * LANES, LANES), LANES)] = (
                    jnp.where(sel, 0.0, MASK_NEG))
            return carry
        lax.fori_loop(0, n_chunks, body, 0)

    @pl.when(has_tie)
    def _():
        tri = tri_ref[...]

        def body(c, seen):
            off = pl.multiple_of(c * tk, tk)
            kc = key_ref[:, pl.ds(off, tk)]
            for j in range(tk // LANES):
                kj = kc[:, j * LANES:(j + 1) * LANES]
                eq = kj == lo
                eqf = jnp.where(eq, 1.0, 0.0)
                rank = _dot(eqf.astype(BF16), tri) + seen
                sel = (kj > lo) | (eq & (rank <= tie_quota))
                madd_ref[:, pl.ds(pl.multiple_of(off + j * LANES, LANES), LANES)] = (
                    jnp.where(sel, 0.0, MASK_NEG))
                seen = seen + jnp.sum(eqf, axis=1, keepdims=True)
            return seen
        lax.fori_loop(0, n_chunks, body, jnp.zeros((tq, LANES), F32))

    for h in range(N_HEADS):
        kvh = h // (N_HEADS // N_KV_HEADS)
        qh = q_ref[h]

        def chunk(c, st, h=h, kvh=kvh, qh=qh):
            m, l, acc = st
            off = pl.multiple_of(c * tk, tk)
            kc = k_ref[kvh, pl.ds(off, tk), :]
            vc = v_ref[kvh, pl.ds(off, tk), :]
            x = _dot_nt(qh, kc) + madd_ref[:, pl.ds(off, tk)]
            x = x + bias_ref[jnp.minimum(qt - c, n_near) * N_HEADS + h]
            m_new = jnp.maximum(m, jnp.max(x, axis=1, keepdims=True))
            alpha = jnp.exp(m - m_new)
            p = jnp.exp(x - m_new)
            l = alpha * l + jnp.sum(p, axis=1, keepdims=True)
            acc = alpha * acc + _dot(p.astype(BF16), vc)
            return m_new, l, acc

        m0 = jnp.full((tq, 1), MASK_NEG, F32)
        l0 = jnp.zeros((tq, 1), F32)
        a0 = jnp.zeros((tq, HEAD_DIM), F32)
        m, l, acc = lax.fori_loop(0, n_chunks, chunk, (m0, l0, a0))
        o_ref[:, h * HEAD_DIM:(h + 1) * HEAD_DIM] = (acc / l).astype(o_ref.dtype)


def _attention(q, qi, wi, k, v, ki, rel_bias, tq):
    bsz, _, seq, _ = q.shape
    n_sel = min(TOPK_MAX, seq // 4)
    far = _far_distance()
    n_near = min(seq // tq, (far - 1 + tq - 1) // tq + 1)
    i = np.arange(tq)[:, None]
    j = np.arange(tq)[None, :]
    bucket = np.stack([_t5_bucket_np(dlt * tq + i - j) for dlt in range(n_near)]
                      + [np.full((tq, tq), N_BUCKETS - 1, np.int32)])
    tri = (np.arange(LANES)[:, None] <= np.arange(LANES)[None, :]).astype(np.float32)

    kernel = functools.partial(_attn_kernel, tq=tq, n_sel=n_sel, n_near=n_near)
    qblk = lambda n, w: pl.BlockSpec((None, n, tq, w), lambda b, t: (b, 0, t, 0))
    full = lambda n, w: pl.BlockSpec((None, n, seq, w), lambda b, t: (b, 0, 0, 0))
    return pl.pallas_call(
        kernel,
        grid=(bsz, seq // tq),
        in_specs=[qblk(N_HEADS, HEAD_DIM), qblk(IDX_HEADS, IDX_DIM),
                  pl.BlockSpec((None, tq, IDX_HEADS), lambda b, t: (b, t, 0)),
                  full(N_KV_HEADS, HEAD_DIM), full(N_KV_HEADS, HEAD_DIM),
                  pl.BlockSpec((None, seq, IDX_DIM), lambda b, t: (b, 0, 0)),
                  _resident((n_near + 1, tq, tq)),
                  pl.BlockSpec(memory_space=pltpu.SMEM),
                  _resident((LANES, LANES))],
        out_specs=pl.BlockSpec((None, tq, N_HEADS * HEAD_DIM), lambda b, t: (b, t, 0)),
        out_shape=jax.ShapeDtypeStruct((bsz, seq, N_HEADS * HEAD_DIM), BF16),
        scratch_shapes=[pltpu.VMEM((tq, seq), jnp.int32),
                        pltpu.VMEM((tq, seq), F32),
                        pltpu.VMEM(((n_near + 1) * N_HEADS, tq, tq), F32)],
        compiler_params=_cparams(2),
        name="attention",
    )(q, qi, wi, k, v, ki, jnp.asarray(bucket), rel_bias, jnp.asarray(tri, BF16))


N_SEG = SUBLANES


def _seg_pitch(seg_len):
    p = seg_len // SUBLANES + 1
    if p % 2 == 0:
        p += 1
    return p * SUBLANES


def _rglru_kernel(xr_ref, gy_ref, cw_ref, cb_ref, wrg_ref, brg_ref, wig_ref, big_ref, lam_ref,
                  o_ref, xp_ref, a_ref, u_ref, *, seq, n_grp):
    seg = seq // N_SEG
    pitch = _seg_pitch(seg)
    pad = SUBLANES

    xp_ref[0:pad, :] = jnp.zeros((pad, LANES), F32)
    for g in range(n_grp):
        ls = slice(g * LANES, (g + 1) * LANES)
        xp_ref[pad:pad + seq, :] = xr_ref[:, ls].astype(F32)
        cw = cw_ref[:, ls]
        xc = cb_ref[:, ls] + cw[0:1] * xp_ref[pl.ds(pad - 3, seq), :]
        for kk in range(1, RNN_CONV):
            xc = xc + cw[kk:kk + 1] * xp_ref[pl.ds(pad - 3 + kk, seq), :]
        xb = xc.astype(BF16)
        r = jax.nn.sigmoid(_dot(xb, wrg_ref[g]) + brg_ref[:, ls])
        ig = jax.nn.sigmoid(_dot(xb, wig_ref[g]) + big_ref[:, ls])
        z = -lam_ref[:, ls]
        softplus = jnp.maximum(z, 0.0) + jnp.log1p(jnp.exp(-jnp.abs(z)))
        log_a = (-LRU_C * softplus) * r
        a = jnp.exp(log_a)
        u = xc * ig * jnp.sqrt(-jnp.tanh(log_a) * (1.0 + a * a))
        for j in range(N_SEG):
            a_ref[g, j * pitch:j * pitch + seg, :] = a[j * seg:(j + 1) * seg]
            u_ref[g, j * pitch:j * pitch + seg, :] = u[j * seg:(j + 1) * seg]

    def step(t, st):
        new = []
        for g in range(n_grp):
            h, p = st[g]
            a_t = a_ref[g, pl.ds(t, N_SEG, stride=pitch), :]
            u_t = u_ref[g, pl.ds(t, N_SEG, stride=pitch), :]
            h = a_t * h + u_t
            p = p * a_t
            u_ref[g, pl.ds(t, N_SEG, stride=pitch), :] = h
            a_ref[g, pl.ds(t, N_SEG, stride=pitch), :] = p
            new.append((h, p))
        return tuple(new)

    init = tuple((jnp.zeros((N_SEG, LANES), F32), jnp.ones((N_SEG, LANES), F32)) for _ in range(n_grp))
    final = lax.fori_loop(0, seg, step, init)

    for g in range(n_grp):
        ls = slice(g * LANES, (g + 1) * LANES)
        h_end, p_end = final[g]
        carry = jnp.zeros((1, LANES), F32)
        for j in range(N_SEG):
            rows = slice(j * pitch, j * pitch + seg)
            hj = u_ref[g, rows, :] + a_ref[g, rows, :] * carry
            o_ref[j * seg:(j + 1) * seg, ls] = (hj * gy_ref[j * seg:(j + 1) * seg, ls].astype(F32)).astype(o_ref.dtype)
            carry = h_end[j:j + 1] + p_end[j:j + 1] * carry


def _rglru(xr, gy, cw, cb, wrg, brg, wig, big, lam, cblk):
    bsz, seq, d = xr.shape
    n_grp = cblk // LANES
    pitch = _seg_pitch(seq // N_SEG)
    kernel = functools.partial(_rglru_kernel, seq=seq, n_grp=n_grp)
    act = pl.BlockSpec((None, seq, cblk), lambda b, c: (b, 0, c))
    vec = lambda r: pl.BlockSpec((r, cblk), lambda b, c: (0, c))
    gate_w = pl.BlockSpec((n_grp, LANES, LANES), lambda b, c: (c, 0, 0))
    return pl.pallas_call(
        kernel,
        grid=(bsz, d // cblk),
        in_specs=[act, act, vec(RNN_CONV), vec(1), gate_w, vec(1), gate_w, vec(1), vec(1)],
        out_specs=act,
        out_shape=jax.ShapeDtypeStruct((bsz, seq, d), BF16),
        scratch_shapes=[pltpu.VMEM((seq + SUBLANES, LANES), F32),
                        pltpu.VMEM((n_grp, N_SEG * pitch, LANES), F32),
                        pltpu.VMEM((n_grp, N_SEG * pitch, LANES), F32)],
        compiler_params=_cparams(2),
        name="rglru",
    )(xr, gy, cw, cb, wrg, brg, wig, big, lam)


def _merge_kernel(x_ref, attn_ref, rnn_ref, sga_ref, sgb_ref, ga_ref, woa_ref, wor_ref, wout_ref, o_ref):
    merged = (sga_ref[...].astype(F32) * _dot(attn_ref[...], woa_ref[...])
              + sgb_ref[...].astype(F32) * _dot(rnn_ref[...], wor_ref[...]))
    o_ref[...] = x_ref[...] + ga_ref[...] * _dot(merged.astype(BF16), wout_ref[...])


def _merge(x, attn, rnn, sga, sgb, ga, woa, wor, wout, tile):
    bsz, seq, d = x.shape
    row = lambda w: pl.BlockSpec((None, tile, w), lambda b, s: (b, s, 0))
    return pl.pallas_call(
        _merge_kernel,
        grid=(bsz, seq // tile),
        in_specs=[row(d), row(attn.shape[-1]), row(d), row(d), row(d),
                  pl.BlockSpec((None, 1, d), lambda b, s: (b, 0, 0)),
                  _resident(woa.shape), _resident(wor.shape), _resident(wout.shape)],
        out_specs=row(d),
        out_shape=jax.ShapeDtypeStruct((bsz, seq, d), F32),
        compiler_params=_cparams(2),
        name="merge",
    )(x, attn, rnn, sga, sgb, ga, woa, wor, wout)


def _ffn_kernel(h_ref, sc_ref, sh_ref, ga_ref, g_ref, gf_ref, wv_ref, wg_ref, cwv_ref, cwg_ref,
                cbv_ref, cbg_ref, wd_ref, o_ref, buf_ref, carry_ref, acc_ref, *, tile, fchunk, n_fchunks):
    s = pl.program_id(1)
    pad = SUBLANES

    @pl.when(s == 0)
    def _():
        carry_ref[...] = jnp.zeros(carry_ref.shape, F32)

    hres = h_ref[...]
    xn = _modulated_norm(hres, g_ref[...], sc_ref[...], sh_ref[...]).astype(BF16)

    def conv(up, idx, cw, cb):
        buf_ref[0:pad, :] = carry_ref[idx]
        buf_ref[pad:pad + tile, :] = up
        carry_ref[idx] = up[tile - pad:tile]
        y = cb + cw[FFN_CONV - 1:FFN_CONV] * up
        for kk in range(FFN_CONV - 1):
            y = y + cw[kk:kk + 1] * buf_ref[pl.ds(pad - (FFN_CONV - 1) + kk, tile), :]
        return y

    for c in range(n_fchunks):
        cs = slice(c * fchunk, (c + 1) * fchunk)
        val = conv(_dot(xn, wv_ref[:, cs]), 2 * c, cwv_ref[:, cs], cbv_ref[:, cs])
        gte = conv(_dot(xn, wg_ref[:, cs]), 2 * c + 1, cwg_ref[:, cs], cbg_ref[:, cs])
        act = ((gte * jax.nn.sigmoid(gte)) * val).astype(BF16)
        part = _dot(act, wd_ref[cs, :])
        if c == 0:
            acc_ref[...] = part
        else:
            acc_ref[...] += part

    h2 = hres + ga_ref[...] * acc_ref[...]
    ms = jnp.mean(h2 * h2, axis=-1, keepdims=True)
    o_ref[...] = h2 * lax.rsqrt(ms + EPS) * gf_ref[...]


def _ffn(h, sc, sh, ga, g, gf, wv, wg, cwv, cwg, cbv, cbg, wd, tile, fchunk):
    bsz, seq, d = h.shape
    dff = wv.shape[1]
    n_fchunks = dff // fchunk
    kernel = functools.partial(_ffn_kernel, tile=tile, fchunk=fchunk, n_fchunks=n_fchunks)
    row = pl.BlockSpec((None, tile, d), lambda b, s: (b, s, 0))
    per_batch = pl.BlockSpec((None, 1, d), lambda b, s: (b, 0, 0))
    return pl.pallas_call(
        kernel,
        grid=(bsz, seq // tile),
        in_specs=[row, per_batch, per_batch, per_batch, _resident((1, d)), _resident((1, d)),
                  _resident(wv.shape), _resident(wg.shape), _resident(cwv.shape), _resident(cwg.shape),
                  _resident(cbv.shape), _resident(cbg.shape), _resident(wd.shape)],
        out_specs=row,
        out_shape=jax.ShapeDtypeStruct((bsz, seq, d), F32),
        scratch_shapes=[pltpu.VMEM((tile + SUBLANES, fchunk), F32),
                        pltpu.VMEM((2 * n_fchunks, SUBLANES, fchunk), F32),
                        pltpu.VMEM((tile, d), F32)],
        compiler_params=_cparams(2),
        name="ffn",
    )(h, sc, sh, ga, g, gf, wv, wg, cwv, cwg, cbv, cbg, wd)


def _pick(seq, pref):
    t = min(seq, pref)
    assert seq % t == 0
    return t


def kernel(x, c, w_ada, b_ada, g_mix, w_in, b_in, rel_bias, conv_rnn_w, conv_rnn_b, w_rg, b_rg, w_ig, b_ig, lru_lambda, w_o_attn, w_o_rnn, w_out, g_ffn, w_up, conv_ffn_w, conv_ffn_b, w_down, g_final):
    bsz, seq, d = x.shape
    depth = w_ada.shape[0]
    assert depth == 1 and d == N_RNN_BLOCKS * LANES
    dff = w_down.shape[1]
    row_tile = _pick(seq, 512)
    tq = _pick(seq, 256)

    widths = (N_HEADS * HEAD_DIM, N_KV_HEADS * HEAD_DIM, N_KV_HEADS * HEAD_DIM, IDX_HEADS * IDX_DIM,
              IDX_DIM, IDX_HEADS, d, d, d, d)
    cuts = np.cumsum((0,) + widths)
    col = lambda a, i0, i1: a[..., cuts[i0]:cuts[i1]]

    h = x
    for l in range(depth):
        mod = _adaln(c, w_ada[l], b_ada[l])
        sh1, sc1, ga1, sh2, sc2, ga2 = [m[:, None, :] for m in jnp.split(mod, 6, axis=-1)]

        w, bias = w_in[l], b_in[l].reshape(1, -1)
        idx_pad = (-(cuts[6] - cuts[3])) % LANES
        weights = [
            col(w, 0, 3).astype(BF16), col(bias, 0, 3),
            jnp.pad(col(w, 3, 6), ((0, 0), (0, idx_pad))).astype(BF16), jnp.pad(col(bias, 3, 6), ((0, 0), (0, idx_pad))),
            col(w, 6, 7).astype(BF16), col(bias, 6, 7),
            col(w, 7, 8).astype(BF16), col(bias, 7, 8),
            col(w, 8, 9).astype(BF16), col(bias, 8, 9),
            col(w, 9, 10).astype(BF16), col(bias, 9, 10)]
        q, k, v, qi, ki, wi, xr, gy, sga, sgb = _in_proj(h, sc1, sh1, g_mix[l].reshape(1, d), weights, row_tile)

        attn = _attention(q, qi, wi, k, v, ki, rel_bias, tq)
        rnn = _rglru(xr, gy, conv_rnn_w[l], conv_rnn_b[l].reshape(1, d), w_rg[l].astype(BF16),
                     b_rg[l].reshape(1, d), w_ig[l].astype(BF16), b_ig[l].reshape(1, d),
                     lru_lambda[l].reshape(1, d), cblk=min(d, 512))
        h = _merge(h, attn, rnn, sga, sgb, ga1, w_o_attn[l].astype(BF16), w_o_rnn[l].astype(BF16),
                   w_out[l].astype(BF16), row_tile)

        wu, cw, cb = w_up[l], conv_ffn_w[l], conv_ffn_b[l].reshape(1, -1)
        h = _ffn(h, sc2, sh2, ga2, g_ffn[l].reshape(1, d), g_final.reshape(1, d),
                 wu[:, :dff].astype(BF16), wu[:, dff:].astype(BF16), cw[:, :dff], cw[:, dff:],
                 cb[:, :dff], cb[:, dff:], w_down[l].astype(BF16), row_tile, fchunk=256)
    return h
```

```python
import functools
import math

import numpy as np
import jax
import jax.numpy as jnp
from jax import lax
from jax.experimental import pallas as pl
from jax.experimental.pallas import tpu as pltpu

N_HEADS = 8
HEAD_DIM = 64
N_KV_HEADS = 2
IDX_HEADS = 8
IDX_DIM = 64
TOPK_MAX = 256
N_BUCKETS = 32
MAX_DISTANCE = 128
N_RNN_BLOCKS = 8
RNN_CONV = 4
LRU_C = 8.0
FFN_CONV = 3
EPS = 1e-6

LANES = 128
SUBLANES = 8
VMEM_LIMIT = 56 * 1024 * 1024

MASK_NEG = -1e30
KEY_NEG_INF = -2139095041

BF16 = jnp.bfloat16
F32 = jnp.float32


def _cparams(n_grid):
    return pltpu.CompilerParams(
        dimension_semantics=("arbitrary",) * n_grid, vmem_limit_bytes=VMEM_LIMIT)


def _resident(shape):
    nd = len(shape)
    return pl.BlockSpec(shape, lambda *_: (0,) * nd, pipeline_mode=pl.Buffered(1))


def _dot(a, b):
    return jnp.dot(a, b, preferred_element_type=F32)


def _dot_nt(a, b):
    return lax.dot_general(a, b, (((1,), (1,)), ((), ())), preferred_element_type=F32)


def _adaln_kernel(c_ref, w_ref, b_ref, o_ref):
    c = c_ref[...]
    c_act = (c * jax.nn.sigmoid(c)).astype(BF16)
    o_ref[...] = _dot(c_act, w_ref[...].astype(BF16)) + b_ref[...]


def _adaln(c, w, b):
    bsz, d = c.shape
    n = w.shape[1]
    return pl.pallas_call(
        _adaln_kernel,
        grid=(n // d,),
        in_specs=[pl.BlockSpec((bsz, d), lambda j: (0, 0)),
                  pl.BlockSpec((d, d), lambda j: (0, j)),
                  pl.BlockSpec((1, d), lambda j: (0, j))],
        out_specs=pl.BlockSpec((bsz, d), lambda j: (0, j)),
        out_shape=jax.ShapeDtypeStruct((bsz, n), F32),
        compiler_params=_cparams(1),
        name="adaln",
    )(c, w, b.reshape(1, n))


def _modulated_norm(x, g, sc, sh):
    ms = jnp.mean(x * x, axis=-1, keepdims=True)
    return (x * lax.rsqrt(ms + EPS) * g) * (1.0 + sc) + sh


def _in_proj_kernel(x_ref, sc_ref, sh_ref, g_ref, wt_ref, bt_ref, wk_ref, bk_ref,
                    wr_ref, br_ref, wy_ref, by_ref, wga_ref, bga_ref, wgb_ref, bgb_ref,
                    qt_ref, qit_ref, vt_ref, wit_ref, k_ref, ki_ref, xr_ref, gy_ref, sga_ref, sgb_ref):
    xn = _modulated_norm(x_ref[...], g_ref[...], sc_ref[...], sh_ref[...]).astype(BF16)

    res = _dot_nt(wt_ref[...], xn) + bt_ref[...]
    r0 = 0
    for h in range(N_HEADS):
        qt_ref[h] = (res[r0 + h * HEAD_DIM:r0 + (h + 1) * HEAD_DIM] * (HEAD_DIM ** -0.5)).astype(BF16)
    r0 += N_HEADS * HEAD_DIM
    for h in range(IDX_HEADS):
        qit_ref[h] = (res[r0 + h * IDX_DIM:r0 + (h + 1) * IDX_DIM] * (IDX_DIM ** -0.5)).astype(BF16)
    r0 += IDX_HEADS * IDX_DIM
    for j in range(N_KV_HEADS):
        vt_ref[j] = res[r0 + j * HEAD_DIM:r0 + (j + 1) * HEAD_DIM].astype(BF16)
    r0 += N_KV_HEADS * HEAD_DIM
    wit_ref[...] = res[r0:r0 + IDX_HEADS] * (IDX_HEADS ** -0.5)

    res = _dot(xn, wk_ref[...]) + bk_ref[...]
    for j in range(N_KV_HEADS):
        k_ref[j] = res[:, j * HEAD_DIM:(j + 1) * HEAD_DIM].astype(BF16)
    ki0 = N_KV_HEADS * HEAD_DIM
    ki_ref[...] = res[:, ki0:ki0 + IDX_DIM].astype(BF16)

    xr_ref[...] = (_dot(xn, wr_ref[...]) + br_ref[...]).astype(BF16)
    gy_ref[...] = jax.nn.gelu(_dot(xn, wy_ref[...]) + by_ref[...]).astype(BF16)
    sga_ref[...] = jax.nn.sigmoid(_dot(xn, wga_ref[...]) + bga_ref[...]).astype(BF16)
    sgb_ref[...] = jax.nn.sigmoid(_dot(xn, wgb_ref[...]) + bgb_ref[...]).astype(BF16)


def _in_proj(x, sc, sh, g, weights, tile):
    bsz, seq, d = x.shape
    row = lambda w: pl.BlockSpec((None, tile, w), lambda b, s: (b, s, 0))
    heads = lambda n, w: pl.BlockSpec((None, n, tile, w), lambda b, s: (b, 0, s, 0))
    heads_t = lambda n, w: pl.BlockSpec((None, n, w, tile), lambda b, s: (b, 0, 0, s))
    per_batch = pl.BlockSpec((None, 1, d), lambda b, s: (b, 0, 0))
    in_specs = [row(d), per_batch, per_batch, _resident((1, d))]
    in_specs += [_resident(w.shape) for w in weights]
    bsd = lambda w, dt: jax.ShapeDtypeStruct((bsz, seq, w), dt)
    out_shape = [
        jax.ShapeDtypeStruct((bsz, N_HEADS, HEAD_DIM, seq), BF16),
        jax.ShapeDtypeStruct((bsz, IDX_HEADS, IDX_DIM, seq), BF16),
        jax.ShapeDtypeStruct((bsz, N_KV_HEADS, HEAD_DIM, seq), BF16),
        jax.ShapeDtypeStruct((bsz, IDX_HEADS, seq), F32),
        jax.ShapeDtypeStruct((bsz, N_KV_HEADS, seq, HEAD_DIM), BF16),
        bsd(IDX_DIM, BF16),
        bsd(d, BF16), bsd(d, BF16), bsd(d, BF16), bsd(d, BF16)]
    out_specs = [heads_t(N_HEADS, HEAD_DIM), heads_t(IDX_HEADS, IDX_DIM), heads_t(N_KV_HEADS, HEAD_DIM),
                 pl.BlockSpec((None, IDX_HEADS, tile), lambda b, s: (b, 0, s)),
                 heads(N_KV_HEADS, HEAD_DIM), row(IDX_DIM),
                 row(d), row(d), row(d), row(d)]
    return pl.pallas_call(
        _in_proj_kernel,
        grid=(bsz, seq // tile),
        in_specs=in_specs, out_specs=out_specs, out_shape=out_shape,
        compiler_params=_cparams(2),
        name="in_proj",
    )(x, sc, sh, g, *weights)


def _t5_bucket_np(rel):
    max_exact = N_BUCKETS // 2
    n = np.maximum(rel, 0)
    nf = np.maximum(n, 1).astype(np.float64)
    large = max_exact + (np.log(nf / max_exact) / math.log(MAX_DISTANCE / max_exact)
                         * (N_BUCKETS - max_exact)).astype(np.int32)
    large = np.minimum(large, N_BUCKETS - 1)
    return np.where(n < max_exact, n, large).astype(np.int32)


def _far_distance():
    d = np.arange(0, 4 * MAX_DISTANCE)
    b = _t5_bucket_np(d)
    assert b[-1] == N_BUCKETS - 1
    return int(np.max(np.nonzero(b != N_BUCKETS - 1)[0])) + 1


def _sort_key(s):
    bits = lax.bitcast_convert_type(s, jnp.int32)
    return bits ^ ((bits >> 31) & jnp.int32(0x7FFFFFFF))


def _key_to_float(k):
    return lax.bitcast_convert_type(k ^ ((k >> 31) & jnp.int32(0x7FFFFFFF)), F32)


VALUE_STEPS = 12
SEARCH_UNROLL = 3


def _attn_kernel(qt_ref, qit_ref, wit_ref, k_ref, vt_ref, ki_ref, bucket_ref, relb_ref, tri_ref,
                 o_ref, key_ref, madd_ref, bias_ref, m_ref, l_ref, acc_ref, ot_ref, xa_ref, xb_ref, p_ref,
                 *, tq, n_sel, n_near):
    b = pl.program_id(0)
    qt = pl.program_id(1)
    n_chunks = qt + 1
    tk = tq
    int_max = jnp.iinfo(jnp.int32).max

    @pl.when((b == 0) & (qt == 0))
    def _():
        for dlt in range(n_near + 1):
            bucket = bucket_ref[dlt]
            for h in range(N_HEADS):
                tile = jnp.full((tk, tq), relb_ref[N_BUCKETS - 1, h], F32)
                for bk in range(N_BUCKETS - 1):
                    tile = jnp.where(bucket == bk, relb_ref[bk, h], tile)
                bias_ref[dlt * N_HEADS + h] = tile

    key_pos = lax.broadcasted_iota(jnp.int32, (tk, tq), 0)
    qry_pos = lax.broadcasted_iota(jnp.int32, (tk, tq), 1)
    t_row = qt * tq + lax.broadcasted_iota(jnp.int32, (1, tq), 1)

    def score_chunk(c, carry):
        kmin, kmax = carry
        off = pl.multiple_of(c * tk, tk)
        kic = ki_ref[pl.ds(off, tk), :]
        s = None
        for h in range(IDX_HEADS):
            d = _dot(kic, qit_ref[h])
            term = wit_ref[h:h + 1, :] * jnp.maximum(d, 0.0)
            s = term if s is None else s + term
        valid = (c < qt) | (key_pos <= qry_pos)
        key = _sort_key(s)
        key_ref[pl.ds(off, tk), :] = jnp.where(valid, key, KEY_NEG_INF)
        kmin = jnp.minimum(kmin, jnp.min(jnp.where(valid, key, int_max), axis=0, keepdims=True))
        kmax = jnp.maximum(kmax, jnp.max(jnp.where(valid, key, KEY_NEG_INF), axis=0, keepdims=True))
        return kmin, kmax

    kmin, kmax = lax.fori_loop(
        0, n_chunks, score_chunk,
        (jnp.full((1, tq), int_max, jnp.int32), jnp.full((1, tq), KEY_NEG_INF, jnp.int32)))

    need = jnp.minimum(t_row + 1, n_sel)

    def count_ge(mid):
        mid8 = jnp.broadcast_to(mid, (SUBLANES, tq))

        def body(c, acc):
            off = pl.multiple_of(c * tk, tk)
            kc = key_ref[pl.ds(off, tk), :].reshape(tk // SUBLANES, SUBLANES, tq)
            return acc + jnp.sum(jnp.where(kc >= mid8[None], 1, 0), axis=0)
        acc = lax.fori_loop(0, n_chunks, body, jnp.zeros((SUBLANES, tq), jnp.int32))
        return jnp.sum(acc, axis=0, keepdims=True)

    def search_cond(st):
        return jnp.min(st[-1]) == 0

    def search_steps(st):
        for _ in range(SEARCH_UNROLL):
            st = search_step(st)
        return st

    def search_step(st):
        it, lo, hi, cnt_lo, cnt_hi, done = st
        half = (lo >> 1) + (hi >> 1) + (lo & hi & 1)
        mid_f = 0.5 * _key_to_float(lo) + 0.5 * _key_to_float(hi)
        by_value = jnp.minimum(jnp.maximum(_sort_key(mid_f), lo + 1), hi - 1)
        mid = jnp.where(it < VALUE_STEPS, by_value, half)
        cnt = count_ge(mid)
        live = done == 0
        up = live & (cnt >= need)
        dn = live & (cnt < need)
        lo = jnp.where(up, mid, lo)
        cnt_lo = jnp.where(up, cnt, cnt_lo)
        hi = jnp.where(dn, mid, hi)
        cnt_hi = jnp.where(dn, cnt, cnt_hi)
        done = jnp.where((cnt_lo == need) | (hi == lo + 1), 1, done)
        return it + 1, lo, hi, cnt_lo, cnt_hi, done

    lo0 = kmin
    hi0 = jnp.minimum(kmax, int_max - 1) + 1
    cnt_lo0 = t_row + 1
    cnt_hi0 = jnp.zeros((1, tq), jnp.int32)
    done0 = jnp.where((cnt_lo0 == need) | (hi0 == lo0 + 1), 1, 0)
    _, lo, hi, cnt_lo, cnt_hi, _ = lax.while_loop(
        search_cond, search_steps, (jnp.int32(0), lo0, hi0, cnt_lo0, cnt_hi0, done0))

    has_tie = jnp.max(jnp.where(cnt_lo > need, 1, 0)) > 0

    @pl.when(jnp.logical_not(has_tie))
    def _():
        def body(c, carry):
            off = pl.multiple_of(c * tk, tk)
            madd_ref[pl.ds(off, tk), :] = jnp.where(key_ref[pl.ds(off, tk), :] >= lo, 0.0, MASK_NEG)
            return carry
        lax.fori_loop(0, n_chunks, body, 0)

    @pl.when(has_tie)
    def _():
        tri = tri_ref[...]
        quota = jnp.where(cnt_lo > need, need - cnt_hi, n_sel).astype(F32)

        def body(c, seen):
            off = pl.multiple_of(c * tk, tk)
            kc = key_ref[pl.ds(off, tk), :]
            eq = kc == lo
            eqf = jnp.where(eq, 1.0, 0.0)
            rank = _dot(tri, eqf.astype(BF16)) + seen
            sel = (kc > lo) | (eq & (rank <= quota))
            madd_ref[pl.ds(off, tk), :] = jnp.where(sel, 0.0, MASK_NEG)
            return seen + jnp.sum(eqf, axis=0, keepdims=True)
        lax.fori_loop(0, n_chunks, body, jnp.zeros((1, tq), F32))

    m_ref[...] = jnp.full(m_ref.shape, MASK_NEG, F32)
    l_ref[...] = jnp.zeros(l_ref.shape, F32)
    acc_ref[...] = jnp.zeros(acc_ref.shape, F32)
    group = N_HEADS // N_KV_HEADS

    def split(v):
        return v.reshape(v.shape[0] // SUBLANES, SUBLANES, v.shape[1])

    def logits(c, x_ref):
        off = pl.multiple_of(c * tk, tk)
        for kvh in range(N_KV_HEADS):
            kc = k_ref[kvh, pl.ds(off, tk), :]
            for h in range(kvh * group, (kvh + 1) * group):
                x_ref[h] = _dot(kc, qt_ref[h])

    def softmax_pv(c, x_ref):
        off = pl.multiple_of(c * tk, tk)
        madd = split(madd_ref[pl.ds(off, tk), :])
        near = jnp.minimum(qt - c, n_near) * N_HEADS
        alphas = []
        for h in range(N_HEADS):
            x = split(x_ref[h]) + madd + split(bias_ref[near + h])
            mx = jnp.max(x, axis=0)
            for shift in (4, 2, 1):
                mx = jnp.maximum(mx, pltpu.roll(mx, shift, 0))
            m_old = m_ref[h]
            m_new = jnp.maximum(m_old, mx)
            alpha = jnp.exp(m_old - m_new)
            p = jnp.exp(x - m_new[None])
            l_ref[h] = alpha * l_ref[h] + jnp.sum(p, axis=0)
            m_ref[h] = m_new
            p_ref[h] = p.reshape(tk, tq).astype(BF16)
            alphas.append(alpha)
        for h in range(N_HEADS):
            pv = _dot(vt_ref[h // group, :, pl.ds(off, tk)], p_ref[h])
            acc_ref[h] = (alphas[h][None] * split(acc_ref[h])).reshape(HEAD_DIM, tq) + pv

    logits(0, xa_ref)

    def attend(c, carry):
        nxt = jnp.minimum(c + 1, qt)

        @pl.when(c % 2 == 0)
        def _():
            logits(nxt, xb_ref)
            softmax_pv(c, xa_ref)

        @pl.when(c % 2 == 1)
        def _():
            logits(nxt, xa_ref)
            softmax_pv(c, xb_ref)
        return carry

    lax.fori_loop(0, n_chunks, attend, 0)

    for h in range(N_HEADS):
        ot_ref[h * HEAD_DIM:(h + 1) * HEAD_DIM, :] = acc_ref[h] / jnp.sum(l_ref[h], axis=0, keepdims=True)
    o_ref[...] = ot_ref[...].T.astype(o_ref.dtype)


def _attention(qt, qit, wit, k, vt, ki, rel_bias, tq):
    bsz, _, _, seq = qt.shape
    n_sel = min(TOPK_MAX, seq // 4)
    far = _far_distance()
    n_near = min(seq // tq, (far - 1 + tq - 1) // tq + 1)
    i = np.arange(tq)[:, None]
    j = np.arange(tq)[None, :]
    bucket = np.stack([_t5_bucket_np(dlt * tq + j - i) for dlt in range(n_near)]
                      + [np.full((tq, tq), N_BUCKETS - 1, np.int32)])
    tri = (j <= i).astype(np.float32)

    kernel = functools.partial(_attn_kernel, tq=tq, n_sel=n_sel, n_near=n_near)
    qblk = lambda n, w: pl.BlockSpec((None, n, w, tq), lambda b, t: (b, 0, 0, t))
    return pl.pallas_call(
        kernel,
        grid=(bsz, seq // tq),
        in_specs=[qblk(N_HEADS, HEAD_DIM), qblk(IDX_HEADS, IDX_DIM),
                  pl.BlockSpec((None, IDX_HEADS, tq), lambda b, t: (b, 0, t)),
                  pl.BlockSpec((None, N_KV_HEADS, seq, HEAD_DIM), lambda b, t: (b, 0, 0, 0)),
                  pl.BlockSpec((None, N_KV_HEADS, HEAD_DIM, seq), lambda b, t: (b, 0, 0, 0)),
                  pl.BlockSpec((None, seq, IDX_DIM), lambda b, t: (b, 0, 0)),
                  _resident((n_near + 1, tq, tq)),
                  pl.BlockSpec(memory_space=pltpu.SMEM),
                  _resident((tq, tq))],
        out_specs=pl.BlockSpec((None, tq, N_HEADS * HEAD_DIM), lambda b, t: (b, t, 0)),
        out_shape=jax.ShapeDtypeStruct((bsz, seq, N_HEADS * HEAD_DIM), BF16),
        scratch_shapes=[pltpu.VMEM((seq, tq), jnp.int32),
                        pltpu.VMEM((seq, tq), F32),
                        pltpu.VMEM(((n_near + 1) * N_HEADS, tq, tq), F32),
                        pltpu.VMEM((N_HEADS, SUBLANES, tq), F32),
                        pltpu.VMEM((N_HEADS, SUBLANES, tq), F32),
                        pltpu.VMEM((N_HEADS, HEAD_DIM, tq), F32),
                        pltpu.VMEM((N_HEADS * HEAD_DIM, tq), F32),
                        pltpu.VMEM((N_HEADS, tq, tq), F32),
                        pltpu.VMEM((N_HEADS, tq, tq), F32),
                        pltpu.VMEM((N_HEADS, tq, tq), BF16)],
        compiler_params=_cparams(2),
        name="attention",
    )(qt, qit, wit, k, vt, ki, jnp.asarray(bucket), rel_bias, jnp.asarray(tri, BF16))


N_SEG = SUBLANES


def _seg_pitch(seg_len):
    p = seg_len // SUBLANES + 1
    if p % 2 == 0:
        p += 1
    return p * SUBLANES


def _rglru_kernel(xr_ref, gy_ref, cw_ref, cb_ref, wrg_ref, brg_ref, wig_ref, big_ref, lam_ref,
                  o_ref, xp_ref, a_ref, u_ref, *, seq, n_grp):
    seg = seq // N_SEG
    pitch = _seg_pitch(seg)
    pad = SUBLANES

    xp_ref[0:pad, :] = jnp.zeros((pad, LANES), F32)
    for g in range(n_grp):
        ls = slice(g * LANES, (g + 1) * LANES)
        xp_ref[pad:pad + seq, :] = xr_ref[:, ls].astype(F32)
        cw = cw_ref[:, ls]
        xc = cb_ref[:, ls] + cw[0:1] * xp_ref[pl.ds(pad - 3, seq), :]
        for kk in range(1, RNN_CONV):
            xc = xc + cw[kk:kk + 1] * xp_ref[pl.ds(pad - 3 + kk, seq), :]
        xb = xc.astype(BF16)
        r = jax.nn.sigmoid(_dot(xb, wrg_ref[g]) + brg_ref[:, ls])
        ig = jax.nn.sigmoid(_dot(xb, wig_ref[g]) + big_ref[:, ls])
        z = -lam_ref[:, ls]
        softplus = jnp.maximum(z, 0.0) + jnp.log1p(jnp.exp(-jnp.abs(z)))
        log_a = (-LRU_C * softplus) * r
        a = jnp.exp(log_a)
        u = xc * ig * jnp.sqrt(-jnp.tanh(log_a) * (1.0 + a * a))
        for j in range(N_SEG):
            a_ref[g, j * pitch:j * pitch + seg, :] = a[j * seg:(j + 1) * seg]
            u_ref[g, j * pitch:j * pitch + seg, :] = u[j * seg:(j + 1) * seg]

    def step(t, st):
        new = []
        for g in range(n_grp):
            h, p = st[g]
            a_t = a_ref[g, pl.ds(t, N_SEG, stride=pitch), :]
            u_t = u_ref[g, pl.ds(t, N_SEG, stride=pitch), :]
            h = a_t * h + u_t
            p = p * a_t
            u_ref[g, pl.ds(t, N_SEG, stride=pitch), :] = h
            a_ref[g, pl.ds(t, N_SEG, stride=pitch), :] = p
            new.append((h, p))
        return tuple(new)

    init = tuple((jnp.zeros((N_SEG, LANES), F32), jnp.ones((N_SEG, LANES), F32)) for _ in range(n_grp))
    final = lax.fori_loop(0, seg, step, init)

    for g in range(n_grp):
        ls = slice(g * LANES, (g + 1) * LANES)
        h_end, p_end = final[g]
        carry = jnp.zeros((1, LANES), F32)
        for j in range(N_SEG):
            rows = slice(j * pitch, j * pitch + seg)
            hj = u_ref[g, rows, :] + a_ref[g, rows, :] * carry
            o_ref[j * seg:(j + 1) * seg, ls] = (hj * gy_ref[j * seg:(j + 1) * seg, ls].astype(F32)).astype(o_ref.dtype)
            carry = h_end[j:j + 1] + p_end[j:j + 1] * carry


def _rglru(xr, gy, cw, cb, wrg, brg, wig, big, lam, cblk):
    bsz, seq, d = xr.shape
    n_grp = cblk // LANES
    pitch = _seg_pitch(seq // N_SEG)
    kernel = functools.partial(_rglru_kernel, seq=seq, n_grp=n_grp)
    act = pl.BlockSpec((None, seq, cblk), lambda b, c: (b, 0, c))
    vec = lambda r: pl.BlockSpec((r, cblk), lambda b, c: (0, c))
    gate_w = pl.BlockSpec((n_grp, LANES, LANES), lambda b, c: (c, 0, 0))
    return pl.pallas_call(
        kernel,
        grid=(bsz, d // cblk),
        in_specs=[act, act, vec(RNN_CONV), vec(1), gate_w, vec(1), gate_w, vec(1), vec(1)],
        out_specs=act,
        out_shape=jax.ShapeDtypeStruct((bsz, seq, d), BF16),
        scratch_shapes=[pltpu.VMEM((seq + SUBLANES, LANES), F32),
                        pltpu.VMEM((n_grp, N_SEG * pitch, LANES), F32),
                        pltpu.VMEM((n_grp, N_SEG * pitch, LANES), F32)],
        compiler_params=_cparams(2),
        name="rglru",
    )(xr, gy, cw, cb, wrg, brg, wig, big, lam)


def _merge_kernel(x_ref, attn_ref, rnn_ref, sga_ref, sgb_ref, ga_ref, woa_ref, wor_ref, wout_ref, o_ref):
    merged = (sga_ref[...].astype(F32) * _dot(attn_ref[...], woa_ref[...])
              + sgb_ref[...].astype(F32) * _dot(rnn_ref[...], wor_ref[...]))
    o_ref[...] = x_ref[...] + ga_ref[...] * _dot(merged.astype(BF16), wout_ref[...])


def _merge(x, attn, rnn, sga, sgb, ga, woa, wor, wout, tile):
    bsz, seq, d = x.shape
    row = lambda w: pl.BlockSpec((None, tile, w), lambda b, s: (b, s, 0))
    return pl.pallas_call(
        _merge_kernel,
        grid=(bsz, seq // tile),
        in_specs=[row(d), row(attn.shape[-1]), row(d), row(d), row(d),
                  pl.BlockSpec((None, 1, d), lambda b, s: (b, 0, 0)),
                  _resident(woa.shape), _resident(wor.shape), _resident(wout.shape)],
        out_specs=row(d),
        out_shape=jax.ShapeDtypeStruct((bsz, seq, d), F32),
        compiler_params=_cparams(2),
        name="merge",
    )(x, attn, rnn, sga, sgb, ga, woa, wor, wout)


def _ffn_kernel(h_ref, sc_ref, sh_ref, ga_ref, g_ref, gf_ref, wv_ref, wg_ref, cwv_ref, cwg_ref,
                cbv_ref, cbg_ref, wd_ref, o_ref, buf_ref, carry_ref, acc_ref, *, tile, fchunk, n_fchunks):
    s = pl.program_id(1)
    pad = SUBLANES

    @pl.when(s == 0)
    def _():
        carry_ref[...] = jnp.zeros(carry_ref.shape, F32)

    hres = h_ref[...]
    xn = _modulated_norm(hres, g_ref[...], sc_ref[...], sh_ref[...]).astype(BF16)

    def conv(up, idx, cw, cb):
        buf_ref[0:pad, :] = carry_ref[idx]
        buf_ref[pad:pad + tile, :] = up
        carry_ref[idx] = up[tile - pad:tile]
        y = cb + cw[FFN_CONV - 1:FFN_CONV] * up
        for kk in range(FFN_CONV - 1):
            y = y + cw[kk:kk + 1] * buf_ref[pl.ds(pad - (FFN_CONV - 1) + kk, tile), :]
        return y

    for c in range(n_fchunks):
        cs = slice(c * fchunk, (c + 1) * fchunk)
        val = conv(_dot(xn, wv_ref[:, cs]), 2 * c, cwv_ref[:, cs], cbv_ref[:, cs])
        gte = conv(_dot(xn, wg_ref[:, cs]), 2 * c + 1, cwg_ref[:, cs], cbg_ref[:, cs])
        act = ((gte * jax.nn.sigmoid(gte)) * val).astype(BF16)
        part = _dot(act, wd_ref[cs, :])
        if c == 0:
            acc_ref[...] = part
        else:
            acc_ref[...] += part

    h2 = hres + ga_ref[...] * acc_ref[...]
    ms = jnp.mean(h2 * h2, axis=-1, keepdims=True)
    o_ref[...] = h2 * lax.rsqrt(ms + EPS) * gf_ref[...]


def _ffn(h, sc, sh, ga, g, gf, wv, wg, cwv, cwg, cbv, cbg, wd, tile, fchunk):
    bsz, seq, d = h.shape
    dff = wv.shape[1]
    n_fchunks = dff // fchunk
    kernel = functools.partial(_ffn_kernel, tile=tile, fchunk=fchunk, n_fchunks=n_fchunks)
    row = pl.BlockSpec((None, tile, d), lambda b, s: (b, s, 0))
    per_batch = pl.BlockSpec((None, 1, d), lambda b, s: (b, 0, 0))
    return pl.pallas_call(
        kernel,
        grid=(bsz, seq // tile),
        in_specs=[row, per_batch, per_batch, per_batch, _resident((1, d)), _resident((1, d)),
                  _resident(wv.shape), _resident(wg.shape), _resident(cwv.shape), _resident(cwg.shape),
                  _resident(cbv.shape), _resident(cbg.shape), _resident(wd.shape)],
        out_specs=row,
        out_shape=jax.ShapeDtypeStruct((bsz, seq, d), F32),
        scratch_shapes=[pltpu.VMEM((tile + SUBLANES, fchunk), F32),
                        pltpu.VMEM((2 * n_fchunks, SUBLANES, fchunk), F32),
                        pltpu.VMEM((tile, d), F32)],
        compiler_params=_cparams(2),
        name="ffn",
    )(h, sc, sh, ga, g, gf, wv, wg, cwv, cwg, cbv, cbg, wd)


def _pick(seq, pref):
    t = min(seq, pref)
    assert seq % t == 0
    return t


def kernel(x, c, w_ada, b_ada, g_mix, w_in, b_in, rel_bias, conv_rnn_w, conv_rnn_b, w_rg, b_rg, w_ig, b_ig, lru_lambda, w_o_attn, w_o_rnn, w_out, g_ffn, w_up, conv_ffn_w, conv_ffn_b, w_down, g_final):
    bsz, seq, d = x.shape
    depth = w_ada.shape[0]
    assert depth == 1 and d == N_RNN_BLOCKS * LANES
    dff = w_down.shape[1]
    row_tile = _pick(seq, 512)
    tq = _pick(seq, 256)

    widths = (N_HEADS * HEAD_DIM, N_KV_HEADS * HEAD_DIM, N_KV_HEADS * HEAD_DIM, IDX_HEADS * IDX_DIM,
              IDX_DIM, IDX_HEADS, d, d, d, d)
    cuts = np.cumsum((0,) + widths)
    col = lambda a, i0, i1: a[..., cuts[i0]:cuts[i1]]

    h = x
    for l in range(depth):
        mod = _adaln(c, w_ada[l], b_ada[l])
        sh1, sc1, ga1, sh2, sc2, ga2 = [m[:, None, :] for m in jnp.split(mod, 6, axis=-1)]

        w, bias = w_in[l], b_in[l]
        t_rows = [col(w, 0, 1), col(w, 3, 4), col(w, 2, 3), col(w, 5, 6)]
        t_bias = [col(bias, 0, 1), col(bias, 3, 4), col(bias, 2, 3), col(bias, 5, 6)]
        n_t = sum(r.shape[1] for r in t_rows)
        t_pad = (-n_t) % (2 * SUBLANES)
        wt = jnp.pad(jnp.concatenate(t_rows, axis=1).T, ((0, t_pad), (0, 0))).astype(BF16)
        bt = jnp.pad(jnp.concatenate(t_bias), (0, t_pad)).reshape(-1, 1)
        k_pad = (-(widths[1] + widths[4])) % LANES
        wk = jnp.pad(jnp.concatenate([col(w, 1, 2), col(w, 4, 5)], axis=1), ((0, 0), (0, k_pad))).astype(BF16)
        bk = jnp.pad(jnp.concatenate([col(bias, 1, 2), col(bias, 4, 5)]), (0, k_pad)).reshape(1, -1)
        weights = [wt, bt, wk, bk]
        for i in range(6, 10):
            weights += [col(w, i, i + 1).astype(BF16), col(bias, i, i + 1).reshape(1, -1)]
        qt, qit, vt, wit, k, ki, xr, gy, sga, sgb = _in_proj(h, sc1, sh1, g_mix[l].reshape(1, d), weights, row_tile)

        attn = _attention(qt, qit, wit, k, vt, ki, rel_bias, tq)
        rnn = _rglru(xr, gy, conv_rnn_w[l], conv_rnn_b[l].reshape(1, d), w_rg[l].astype(BF16),
                     b_rg[l].reshape(1, d), w_ig[l].astype(BF16), b_ig[l].reshape(1, d),
                     lru_lambda[l].reshape(1, d), cblk=min(d, 512))
        h = _merge(h, attn, rnn, sga, sgb, ga1, w_o_attn[l].astype(BF16), w_o_rnn[l].astype(BF16),
                   w_out[l].astype(BF16), row_tile)

        wu, cw, cb = w_up[l], conv_ffn_w[l], conv_ffn_b[l].reshape(1, -1)
        h = _ffn(h, sc2, sh2, ga2, g_ffn[l].reshape(1, d), g_final.reshape(1, d),
                 wu[:, :dff].astype(BF16), wu[:, dff:].astype(BF16), cw[:, :dff], cw[:, dff:],
                 cb[:, :dff], cb[:, dff:], w_down[l].astype(BF16), row_tile, fchunk=256)
    return h
```

```python
import functools
import math

import numpy as np
import jax
import jax.numpy as jnp
from jax import lax
from jax.experimental import pallas as pl
from jax.experimental.pallas import tpu as pltpu

N_HEADS = 8
HEAD_DIM = 64
N_KV_HEADS = 2
IDX_HEADS = 8
IDX_DIM = 64
TOPK_MAX = 256
N_BUCKETS = 32
MAX_DISTANCE = 128
N_RNN_BLOCKS = 8
RNN_CONV = 4
LRU_C = 8.0
FFN_CONV = 3
EPS = 1e-6

LANES = 128
SUBLANES = 8
VMEM_LIMIT = 56 * 1024 * 1024

MASK_NEG = -1e30
KEY_NEG_INF = -2139095041

BF16 = jnp.bfloat16
F32 = jnp.float32


def _cparams(n_grid):
    return pltpu.CompilerParams(
        dimension_semantics=("arbitrary",) * n_grid, vmem_limit_bytes=VMEM_LIMIT)


def _resident(shape):
    nd = len(shape)
    return pl.BlockSpec(shape, lambda *_: (0,) * nd, pipeline_mode=pl.Buffered(1))


def _dot(a, b):
    return jnp.dot(a, b, preferred_element_type=F32)


def _dot_nt(a, b):
    return lax.dot_general(a, b, (((1,), (1,)), ((), ())), preferred_element_type=F32)


def _adaln_kernel(c_ref, w_ref, b_ref, o_ref):
    c = c_ref[...]
    c_act = (c * jax.nn.sigmoid(c)).astype(BF16)
    o_ref[...] = _dot(c_act, w_ref[...].astype(BF16)) + b_ref[...]


def _adaln(c, w, b):
    bsz, d = c.shape
    n = w.shape[1]
    return pl.pallas_call(
        _adaln_kernel,
        grid=(n // d,),
        in_specs=[pl.BlockSpec((bsz, d), lambda j: (0, 0)),
                  pl.BlockSpec((d, d), lambda j: (0, j)),
                  pl.BlockSpec((1, d), lambda j: (0, j))],
        out_specs=pl.BlockSpec((bsz, d), lambda j: (0, j)),
        out_shape=jax.ShapeDtypeStruct((bsz, n), F32),
        compiler_params=_cparams(1),
        name="adaln",
    )(c, w, b.reshape(1, n))


def _modulated_norm(x, g, sc, sh):
    ms = jnp.mean(x * x, axis=-1, keepdims=True)
    return (x * lax.rsqrt(ms + EPS) * g) * (1.0 + sc) + sh


def _in_proj_kernel(x_ref, sc_ref, sh_ref, g_ref, wt_ref, bt_ref, wk_ref, bk_ref,
                    wr_ref, br_ref, wy_ref, by_ref, wga_ref, bga_ref, wgb_ref, bgb_ref,
                    qt_ref, qit_ref, vt_ref, wit_ref, k_ref, ki_ref, xr_ref, gy_ref, sga_ref, sgb_ref):
    xn = _modulated_norm(x_ref[...], g_ref[...], sc_ref[...], sh_ref[...]).astype(BF16)

    res = _dot_nt(wt_ref[...], xn) + bt_ref[...]
    r0 = 0
    for h in range(N_HEADS):
        qt_ref[h] = (res[r0 + h * HEAD_DIM:r0 + (h + 1) * HEAD_DIM] * (HEAD_DIM ** -0.5)).astype(BF16)
    r0 += N_HEADS * HEAD_DIM
    for h in range(IDX_HEADS):
        qit_ref[h] = (res[r0 + h * IDX_DIM:r0 + (h + 1) * IDX_DIM] * (IDX_DIM ** -0.5)).astype(BF16)
    r0 += IDX_HEADS * IDX_DIM
    for j in range(N_KV_HEADS):
        vt_ref[j] = res[r0 + j * HEAD_DIM:r0 + (j + 1) * HEAD_DIM].astype(BF16)
    r0 += N_KV_HEADS * HEAD_DIM
    wit_ref[...] = res[r0:r0 + IDX_HEADS] * (IDX_HEADS ** -0.5)

    res = _dot(xn, wk_ref[...]) + bk_ref[...]
    for j in range(N_KV_HEADS):
        k_ref[j] = res[:, j * HEAD_DIM:(j + 1) * HEAD_DIM].astype(BF16)
    ki0 = N_KV_HEADS * HEAD_DIM
    ki_ref[...] = res[:, ki0:ki0 + IDX_DIM].astype(BF16)

    xr_ref[...] = (_dot(xn, wr_ref[...]) + br_ref[...]).astype(BF16)
    gy_ref[...] = jax.nn.gelu(_dot(xn, wy_ref[...]) + by_ref[...]).astype(BF16)
    sga_ref[...] = jax.nn.sigmoid(_dot(xn, wga_ref[...]) + bga_ref[...]).astype(BF16)
    sgb_ref[...] = jax.nn.sigmoid(_dot(xn, wgb_ref[...]) + bgb_ref[...]).astype(BF16)


def _in_proj(x, sc, sh, g, weights, tile):
    bsz, seq, d = x.shape
    row = lambda w: pl.BlockSpec((None, tile, w), lambda b, s: (b, s, 0))
    heads = lambda n, w: pl.BlockSpec((None, n, tile, w), lambda b, s: (b, 0, s, 0))
    heads_t = lambda n, w: pl.BlockSpec((None, n, w, tile), lambda b, s: (b, 0, 0, s))
    per_batch = pl.BlockSpec((None, 1, d), lambda b, s: (b, 0, 0))
    in_specs = [row(d), per_batch, per_batch, _resident((1, d))]
    in_specs += [_resident(w.shape) for w in weights]
    bsd = lambda w, dt: jax.ShapeDtypeStruct((bsz, seq, w), dt)
    out_shape = [
        jax.ShapeDtypeStruct((bsz, N_HEADS, HEAD_DIM, seq), BF16),
        jax.ShapeDtypeStruct((bsz, IDX_HEADS, IDX_DIM, seq), BF16),
        jax.ShapeDtypeStruct((bsz, N_KV_HEADS, HEAD_DIM, seq), BF16),
        jax.ShapeDtypeStruct((bsz, IDX_HEADS, seq), F32),
        jax.ShapeDtypeStruct((bsz, N_KV_HEADS, seq, HEAD_DIM), BF16),
        bsd(IDX_DIM, BF16),
        bsd(d, BF16), bsd(d, BF16), bsd(d, BF16), bsd(d, BF16)]
    out_specs = [heads_t(N_HEADS, HEAD_DIM), heads_t(IDX_HEADS, IDX_DIM), heads_t(N_KV_HEADS, HEAD_DIM),
                 pl.BlockSpec((None, IDX_HEADS, tile), lambda b, s: (b, 0, s)),
                 heads(N_KV_HEADS, HEAD_DIM), row(IDX_DIM),
                 row(d), row(d), row(d), row(d)]
    return pl.pallas_call(
        _in_proj_kernel,
        grid=(bsz, seq // tile),
        in_specs=in_specs, out_specs=out_specs, out_shape=out_shape,
        compiler_params=_cparams(2),
        name="in_proj",
    )(x, sc, sh, g, *weights)


def _t5_bucket_np(rel):
    max_exact = N_BUCKETS // 2
    n = np.maximum(rel, 0)
    nf = np.maximum(n, 1).astype(np.float64)
    large = max_exact + (np.log(nf / max_exact) / math.log(MAX_DISTANCE / max_exact)
                         * (N_BUCKETS - max_exact)).astype(np.int32)
    large = np.minimum(large, N_BUCKETS - 1)
    return np.where(n < max_exact, n, large).astype(np.int32)


def _far_distance():
    d = np.arange(0, 4 * MAX_DISTANCE)
    b = _t5_bucket_np(d)
    assert b[-1] == N_BUCKETS - 1
    return int(np.max(np.nonzero(b != N_BUCKETS - 1)[0])) + 1


def _sort_key(s):
    bits = lax.bitcast_convert_type(s, jnp.int32)
    return bits ^ ((bits >> 31) & jnp.int32(0x7FFFFFFF))


def _key_to_float(k):
    return lax.bitcast_convert_type(k ^ ((k >> 31) & jnp.int32(0x7FFFFFFF)), F32)


VALUE_STEPS = 12
SEARCH_UNROLL = 3


def _attn_kernel(qt_ref, qit_ref, wit_ref, k_ref, vt_ref, ki_ref, bucket_ref, relb_ref, tri_ref,
                 o_ref, key_ref, madd_ref, bias_ref, m_ref, l_ref, acc_ref, ot_ref, xa_ref, xb_ref, p_ref,
                 *, tq, n_sel, n_near):
    b = pl.program_id(0)
    qt = pl.program_id(1)
    n_chunks = qt + 1
    tk = tq
    int_max = jnp.iinfo(jnp.int32).max

    @pl.when((b == 0) & (qt == 0))
    def _():
        for dlt in range(n_near + 1):
            bucket = bucket_ref[dlt]
            for h in range(N_HEADS):
                tile = jnp.full((tk, tq), relb_ref[N_BUCKETS - 1, h], F32)
                for bk in range(N_BUCKETS - 1):
                    tile = jnp.where(bucket == bk, relb_ref[bk, h], tile)
                bias_ref[dlt * N_HEADS + h] = tile

    key_pos = lax.broadcasted_iota(jnp.int32, (tk, tq), 0)
    qry_pos = lax.broadcasted_iota(jnp.int32, (tk, tq), 1)
    t_row = qt * tq + lax.broadcasted_iota(jnp.int32, (1, tq), 1)

    def split(v):
        return v.reshape(v.shape[0] // SUBLANES, SUBLANES, v.shape[1])

    def score_chunk(c, carry):
        kmin, kmax, npos, nzero = carry
        off = pl.multiple_of(c * tk, tk)
        kic = ki_ref[pl.ds(off, tk), :]
        s = None
        for h in range(IDX_HEADS):
            d = _dot(kic, qit_ref[h])
            term = wit_ref[h:h + 1, :] * jnp.maximum(d, 0.0)
            s = term if s is None else s + term
        valid = (c < qt) | (key_pos <= qry_pos)
        s = jnp.where(valid, s, -jnp.inf)
        is_zero = s == 0.0
        key = jnp.where(is_zero, 0, _sort_key(s))
        key_ref[pl.ds(off, tk), :] = key
        kmin = jnp.minimum(kmin, jnp.min(split(jnp.where(valid, key, int_max)), axis=0))
        kmax = jnp.maximum(kmax, jnp.max(split(key), axis=0))
        npos = npos + jnp.sum(split(jnp.where(s > 0.0, 1, 0)), axis=0)
        nzero = nzero + jnp.sum(split(jnp.where(is_zero, 1, 0)), axis=0)
        return kmin, kmax, npos, nzero

    zeros8 = jnp.zeros((SUBLANES, tq), jnp.int32)
    kmin, kmax, npos, nzero = lax.fori_loop(
        0, n_chunks, score_chunk,
        (jnp.full((SUBLANES, tq), int_max, jnp.int32), jnp.full((SUBLANES, tq), KEY_NEG_INF, jnp.int32),
         zeros8, zeros8))
    kmin = jnp.min(kmin, axis=0, keepdims=True)
    kmax = jnp.max(kmax, axis=0, keepdims=True)
    npos = jnp.sum(npos, axis=0, keepdims=True)
    nneg0 = npos + jnp.sum(nzero, axis=0, keepdims=True)

    need = jnp.minimum(t_row + 1, n_sel)

    def count_ge(mid):
        mid8 = jnp.broadcast_to(mid, (SUBLANES, tq))

        def body(c, acc):
            off = pl.multiple_of(c * tk, tk)
            kc = key_ref[pl.ds(off, tk), :].reshape(tk // SUBLANES, SUBLANES, tq)
            return acc + jnp.sum(jnp.where(kc >= mid8[None], 1, 0), axis=0)
        acc = lax.fori_loop(0, n_chunks, body, jnp.zeros((SUBLANES, tq), jnp.int32))
        return jnp.sum(acc, axis=0, keepdims=True)

    def search_cond(st):
        return jnp.min(st[-1]) == 0

    def search_steps(st):
        for _ in range(SEARCH_UNROLL):
            st = search_step(st)
        return st

    def search_step(st):
        it, lo, hi, cnt_lo, cnt_hi, done = st
        half = (lo >> 1) + (hi >> 1) + (lo & hi & 1)
        mid_f = 0.5 * _key_to_float(lo) + 0.5 * _key_to_float(hi)
        by_value = jnp.minimum(jnp.maximum(_sort_key(mid_f), lo + 1), hi - 1)
        mid = jnp.where(it < VALUE_STEPS, by_value, half)
        cnt = count_ge(mid)
        live = done == 0
        up = live & (cnt >= need)
        dn = live & (cnt < need)
        lo = jnp.where(up, mid, lo)
        cnt_lo = jnp.where(up, cnt, cnt_lo)
        hi = jnp.where(dn, mid, hi)
        cnt_hi = jnp.where(dn, cnt, cnt_hi)
        done = jnp.where((cnt_lo == need) | (hi == lo + 1), 1, done)
        return it + 1, lo, hi, cnt_lo, cnt_hi, done

    above = need <= npos
    below = need > nneg0
    lo0 = jnp.where(above, 1, jnp.where(below, kmin, 0))
    cnt_lo0 = jnp.where(above, npos, jnp.where(below, t_row + 1, nneg0))
    hi0 = jnp.where(above, jnp.minimum(kmax, int_max - 1) + 1, jnp.where(below, 0, 1))
    cnt_hi0 = jnp.where(above, 0, jnp.where(below, nneg0, npos))
    done0 = jnp.where((cnt_lo0 == need) | (hi0 == lo0 + 1), 1, 0)
    _, lo, hi, cnt_lo, cnt_hi, _ = lax.while_loop(
        search_cond, search_steps, (jnp.int32(0), lo0, hi0, cnt_lo0, cnt_hi0, done0))

    has_tie = jnp.max(jnp.where(cnt_lo > need, 1, 0)) > 0

    @pl.when(jnp.logical_not(has_tie))
    def _():
        def body(c, carry):
            off = pl.multiple_of(c * tk, tk)
            madd_ref[pl.ds(off, tk), :] = jnp.where(key_ref[pl.ds(off, tk), :] >= lo, 0.0, MASK_NEG)
            return carry
        lax.fori_loop(0, n_chunks, body, 0)

    @pl.when(has_tie)
    def _():
        tri = tri_ref[...]
        quota = jnp.where(cnt_lo > need, need - cnt_hi, n_sel).astype(F32)

        def body(c, seen):
            off = pl.multiple_of(c * tk, tk)
            kc = key_ref[pl.ds(off, tk), :]
            eq = kc == lo
            eqf = jnp.where(eq, 1.0, 0.0)
            rank = _dot(tri, eqf.astype(BF16)) + seen
            sel = (kc > lo) | (eq & (rank <= quota))
            madd_ref[pl.ds(off, tk), :] = jnp.where(sel, 0.0, MASK_NEG)
            return seen + jnp.sum(eqf, axis=0, keepdims=True)
        lax.fori_loop(0, n_chunks, body, jnp.zeros((1, tq), F32))

    m_ref[...] = jnp.full(m_ref.shape, MASK_NEG, F32)
    l_ref[...] = jnp.zeros(l_ref.shape, F32)
    acc_ref[...] = jnp.zeros(acc_ref.shape, F32)
    group = N_HEADS // N_KV_HEADS

    def logits(c, x_ref):
        off = pl.multiple_of(c * tk, tk)
        for kvh in range(N_KV_HEADS):
            kc = k_ref[kvh, pl.ds(off, tk), :]
            for h in range(kvh * group, (kvh + 1) * group):
                x_ref[h] = _dot(kc, qt_ref[h])

    def softmax_pv(c, x_ref, nxt, xn_ref):
        off = pl.multiple_of(c * tk, tk)
        off_n = pl.multiple_of(nxt * tk, tk)
        madd = split(madd_ref[pl.ds(off, tk), :])
        near = jnp.minimum(qt - c, n_near) * N_HEADS

        def pv(h, alpha):
            prod = _dot(vt_ref[h // group, :, pl.ds(off, tk)], p_ref[h])
            acc_ref[h] = (alpha[None] * split(acc_ref[h])).reshape(HEAD_DIM, tq) + prod

        prev_alpha = None
        for h in range(N_HEADS):
            xn_ref[h] = _dot(k_ref[h // group, pl.ds(off_n, tk), :], qt_ref[h])
            x = split(x_ref[h]) + madd + split(bias_ref[near + h])
            mx = jnp.max(x, axis=0)
            for shift in (4, 2, 1):
                mx = jnp.maximum(mx, pltpu.roll(mx, shift, 0))
            m_old = m_ref[h]
            m_new = jnp.maximum(m_old, mx)
            alpha = jnp.exp(m_old - m_new)
            p = jnp.exp(x - m_new[None])
            l_ref[h] = alpha * l_ref[h] + jnp.sum(p, axis=0)
            m_ref[h] = m_new
            p_ref[h] = p.reshape(tk, tq).astype(BF16)
            if h > 0:
                pv(h - 1, prev_alpha)
            prev_alpha = alpha
        pv(N_HEADS - 1, prev_alpha)

    logits(0, xa_ref)

    def attend(c, carry):
        nxt = jnp.minimum(c + 1, qt)

        @pl.when(c % 2 == 0)
        def _():
            softmax_pv(c, xa_ref, nxt, xb_ref)

        @pl.when(c % 2 == 1)
        def _():
            softmax_pv(c, xb_ref, nxt, xa_ref)
        return carry

    lax.fori_loop(0, n_chunks, attend, 0)

    for h in range(N_HEADS):
        ot_ref[h * HEAD_DIM:(h + 1) * HEAD_DIM, :] = acc_ref[h] / jnp.sum(l_ref[h], axis=0, keepdims=True)
    o_ref[...] = ot_ref[...].T.astype(o_ref.dtype)


def _attention(qt, qit, wit, k, vt, ki, rel_bias, tq):
    bsz, _, _, seq = qt.shape
    n_sel = min(TOPK_MAX, seq // 4)
    far = _far_distance()
    n_near = min(seq // tq, (far - 1 + tq - 1) // tq + 1)
    i = np.arange(tq)[:, None]
    j = np.arange(tq)[None, :]
    bucket = np.stack([_t5_bucket_np(dlt * tq + j - i) for dlt in range(n_near)]
                      + [np.full((tq, tq), N_BUCKETS - 1, np.int32)])
    tri = (j <= i).astype(np.float32)

    kernel = functools.partial(_attn_kernel, tq=tq, n_sel=n_sel, n_near=n_near)
    qblk = lambda n, w: pl.BlockSpec((None, n, w, tq), lambda b, t: (b, 0, 0, t))
    return pl.pallas_call(
        kernel,
        grid=(bsz, seq // tq),
        in_specs=[qblk(N_HEADS, HEAD_DIM), qblk(IDX_HEADS, IDX_DIM),
                  pl.BlockSpec((None, IDX_HEADS, tq), lambda b, t: (b, 0, t)),
                  pl.BlockSpec((None, N_KV_HEADS, seq, HEAD_DIM), lambda b, t: (b, 0, 0, 0)),
                  pl.BlockSpec((None, N_KV_HEADS, HEAD_DIM, seq), lambda b, t: (b, 0, 0, 0)),
                  pl.BlockSpec((None, seq, IDX_DIM), lambda b, t: (b, 0, 0)),
                  _resident((n_near + 1, tq, tq)),
                  pl.BlockSpec(memory_space=pltpu.SMEM),
                  _resident((tq, tq))],
        out_specs=pl.BlockSpec((None, tq, N_HEADS * HEAD_DIM), lambda b, t: (b, t, 0)),
        out_shape=jax.ShapeDtypeStruct((bsz, seq, N_HEADS * HEAD_DIM), BF16),
        scratch_shapes=[pltpu.VMEM((seq, tq), jnp.int32),
                        pltpu.VMEM((seq, tq), F32),
                        pltpu.VMEM(((n_near + 1) * N_HEADS, tq, tq), F32),
                        pltpu.VMEM((N_HEADS, SUBLANES, tq), F32),
                        pltpu.VMEM((N_HEADS, SUBLANES, tq), F32),
                        pltpu.VMEM((N_HEADS, HEAD_DIM, tq), F32),
                        pltpu.VMEM((N_HEADS * HEAD_DIM, tq), F32),
                        pltpu.VMEM((N_HEADS, tq, tq), F32),
                        pltpu.VMEM((N_HEADS, tq, tq), F32),
                        pltpu.VMEM((N_HEADS, tq, tq), BF16)],
        compiler_params=_cparams(2),
        name="attention",
    )(qt, qit, wit, k, vt, ki, jnp.asarray(bucket), rel_bias, jnp.asarray(tri, BF16))


N_SEG = SUBLANES


def _seg_pitch(seg_len):
    p = seg_len // SUBLANES + 1
    if p % 2 == 0:
        p += 1
    return p * SUBLANES


def _rglru_kernel(xr_ref, gy_ref, cw_ref, cb_ref, wrg_ref, brg_ref, wig_ref, big_ref, lam_ref,
                  o_ref, xp_ref, a_ref, u_ref, *, seq, n_grp):
    seg = seq // N_SEG
    pitch = _seg_pitch(seg)
    pad = SUBLANES

    xp_ref[0:pad, :] = jnp.zeros((pad, LANES), F32)
    for g in range(n_grp):
        ls = slice(g * LANES, (g + 1) * LANES)
        xp_ref[pad:pad + seq, :] = xr_ref[:, ls].astype(F32)
        cw = cw_ref[:, ls]
        xc = cb_ref[:, ls] + cw[0:1] * xp_ref[pl.ds(pad - 3, seq), :]
        for kk in range(1, RNN_CONV):
            xc = xc + cw[kk:kk + 1] * xp_ref[pl.ds(pad - 3 + kk, seq), :]
        xb = xc.astype(BF16)
        r = jax.nn.sigmoid(_dot(xb, wrg_ref[g]) + brg_ref[:, ls])
        ig = jax.nn.sigmoid(_dot(xb, wig_ref[g]) + big_ref[:, ls])
        z = -lam_ref[:, ls]
        softplus = jnp.maximum(z, 0.0) + jnp.log1p(jnp.exp(-jnp.abs(z)))
        log_a = (-LRU_C * softplus) * r
        a = jnp.exp(log_a)
        u = xc * ig * jnp.sqrt(-jnp.tanh(log_a) * (1.0 + a * a))
        for j in range(N_SEG):
            a_ref[g, j * pitch:j * pitch + seg, :] = a[j * seg:(j + 1) * seg]
            u_ref[g, j * pitch:j * pitch + seg, :] = u[j * seg:(j + 1) * seg]

    def step(t, st):
        new = []
        for g in range(n_grp):
            h, p = st[g]
            a_t = a_ref[g, pl.ds(t, N_SEG, stride=pitch), :]
            u_t = u_ref[g, pl.ds(t, N_SEG, stride=pitch), :]
            h = a_t * h + u_t
            p = p * a_t
            u_ref[g, pl.ds(t, N_SEG, stride=pitch), :] = h
            a_ref[g, pl.ds(t, N_SEG, stride=pitch), :] = p
            new.append((h, p))
        return tuple(new)

    init = tuple((jnp.zeros((N_SEG, LANES), F32), jnp.ones((N_SEG, LANES), F32)) for _ in range(n_grp))
    final = lax.fori_loop(0, seg, step, init)

    for g in range(n_grp):
        ls = slice(g * LANES, (g + 1) * LANES)
        h_end, p_end = final[g]
        carry = jnp.zeros((1, LANES), F32)
        for j in range(N_SEG):
            rows = slice(j * pitch, j * pitch + seg)
            hj = u_ref[g, rows, :] + a_ref[g, rows, :] * carry
            o_ref[j * seg:(j + 1) * seg, ls] = (hj * gy_ref[j * seg:(j + 1) * seg, ls].astype(F32)).astype(o_ref.dtype)
            carry = h_end[j:j + 1] + p_end[j:j + 1] * carry


def _rglru(xr, gy, cw, cb, wrg, brg, wig, big, lam, cblk):
    bsz, seq, d = xr.shape
    n_grp = cblk // LANES
    pitch = _seg_pitch(seq // N_SEG)
    kernel = functools.partial(_rglru_kernel, seq=seq, n_grp=n_grp)
    act = pl.BlockSpec((None, seq, cblk), lambda b, c: (b, 0, c))
    vec = lambda r: pl.BlockSpec((r, cblk), lambda b, c: (0, c))
    gate_w = pl.BlockSpec((n_grp, LANES, LANES), lambda b, c: (c, 0, 0))
    return pl.pallas_call(
        kernel,
        grid=(bsz, d // cblk),
        in_specs=[act, act, vec(RNN_CONV), vec(1), gate_w, vec(1), gate_w, vec(1), vec(1)],
        out_specs=act,
        out_shape=jax.ShapeDtypeStruct((bsz, seq, d), BF16),
        scratch_shapes=[pltpu.VMEM((seq + SUBLANES, LANES), F32),
                        pltpu.VMEM((n_grp, N_SEG * pitch, LANES), F32),
                        pltpu.VMEM((n_grp, N_SEG * pitch, LANES), F32)],
        compiler_params=_cparams(2),
        name="rglru",
    )(xr, gy, cw, cb, wrg, brg, wig, big, lam)


def _merge_ffn_kernel(x_ref, attn_ref, rnn_ref, sga_ref, sgb_ref, ga1_ref, woa_ref, wor_ref, wout_ref,
                      sc_ref, sh_ref, ga2_ref, g_ref, gf_ref, wv_ref, wg_ref, cwv_ref, cwg_ref,
                      cbv_ref, cbg_ref, wd_ref, o_ref, buf_ref, carry_ref, act_ref, *, tile, fchunk, n_fchunks):
    s = pl.program_id(1)
    pad = SUBLANES

    @pl.when(s == 0)
    def _():
        carry_ref[...] = jnp.zeros(carry_ref.shape, F32)

    merged = (sga_ref[...].astype(F32) * _dot(attn_ref[...], woa_ref[...])
              + sgb_ref[...].astype(F32) * _dot(rnn_ref[...], wor_ref[...]))
    hres = x_ref[...] + ga1_ref[...] * _dot(merged.astype(BF16), wout_ref[...])
    xn = _modulated_norm(hres, g_ref[...], sc_ref[...], sh_ref[...]).astype(BF16)

    def conv(up, idx, cw, cb):
        buf_ref[0:pad, :] = carry_ref[idx]
        buf_ref[pad:pad + tile, :] = up
        carry_ref[idx] = up[tile - pad:tile]
        y = cb + cw[FFN_CONV - 1:FFN_CONV] * up
        for kk in range(FFN_CONV - 1):
            y = y + cw[kk:kk + 1] * buf_ref[pl.ds(pad - (FFN_CONV - 1) + kk, tile), :]
        return y

    for c in range(n_fchunks):
        cs = slice(c * fchunk, (c + 1) * fchunk)
        val = conv(_dot(xn, wv_ref[:, cs]), 2 * c, cwv_ref[:, cs], cbv_ref[:, cs])
        gte = conv(_dot(xn, wg_ref[:, cs]), 2 * c + 1, cwg_ref[:, cs], cbg_ref[:, cs])
        act_ref[:, cs] = ((gte * jax.nn.sigmoid(gte)) * val).astype(BF16)

    h2 = hres + ga2_ref[...] * _dot(act_ref[...], wd_ref[...])
    ms = jnp.mean(h2 * h2, axis=-1, keepdims=True)
    o_ref[...] = h2 * lax.rsqrt(ms + EPS) * gf_ref[...]


def _merge_ffn(x, attn, rnn, sga, sgb, ga1, woa, wor, wout, sc, sh, ga2, g, gf,
               wv, wg, cwv, cwg, cbv, cbg, wd, tile, fchunk):
    bsz, seq, d = x.shape
    dff = wv.shape[1]
    n_fchunks = dff // fchunk
    kernel = functools.partial(_merge_ffn_kernel, tile=tile, fchunk=fchunk, n_fchunks=n_fchunks)
    row = lambda w: pl.BlockSpec((None, tile, w), lambda b, s: (b, s, 0))
    per_batch = pl.BlockSpec((None, 1, d), lambda b, s: (b, 0, 0))
    merge_w = [woa, wor, wout]
    ffn_w = [g, gf, wv, wg, cwv, cwg, cbv, cbg, wd]
    return pl.pallas_call(
        kernel,
        grid=(bsz, seq // tile),
        in_specs=([row(d), row(attn.shape[-1]), row(d), row(d), row(d), per_batch]
                  + [_resident(a.shape) for a in merge_w]
                  + [per_batch, per_batch, per_batch]
                  + [_resident(a.shape) for a in ffn_w]),
        out_specs=row(d),
        out_shape=jax.ShapeDtypeStruct((bsz, seq, d), F32),
        scratch_shapes=[pltpu.VMEM((tile + SUBLANES, fchunk), F32),
                        pltpu.VMEM((2 * n_fchunks, SUBLANES, fchunk), F32),
                        pltpu.VMEM((tile, dff), BF16)],
        compiler_params=_cparams(2),
        name="merge_ffn",
    )(x, attn, rnn, sga, sgb, ga1, *merge_w, sc, sh, ga2, *ffn_w)


def _pick(seq, pref):
    t = min(seq, pref)
    assert seq % t == 0
    return t


def kernel(x, c, w_ada, b_ada, g_mix, w_in, b_in, rel_bias, conv_rnn_w, conv_rnn_b, w_rg, b_rg, w_ig, b_ig, lru_lambda, w_o_attn, w_o_rnn, w_out, g_ffn, w_up, conv_ffn_w, conv_ffn_b, w_down, g_final):
    bsz, seq, d = x.shape
    depth = w_ada.shape[0]
    assert depth == 1 and d == N_RNN_BLOCKS * LANES
    dff = w_down.shape[1]
    row_tile = _pick(seq, 512)
    tq = _pick(seq, 256)

    widths = (N_HEADS * HEAD_DIM, N_KV_HEADS * HEAD_DIM, N_KV_HEADS * HEAD_DIM, IDX_HEADS * IDX_DIM,
              IDX_DIM, IDX_HEADS, d, d, d, d)
    cuts = np.cumsum((0,) + widths)
    col = lambda a, i0, i1: a[..., cuts[i0]:cuts[i1]]

    h = x
    for l in range(depth):
        mod = _adaln(c, w_ada[l], b_ada[l])
        sh1, sc1, ga1, sh2, sc2, ga2 = [m[:, None, :] for m in jnp.split(mod, 6, axis=-1)]

        w, bias = w_in[l], b_in[l]
        t_rows = [col(w, 0, 1), col(w, 3, 4), col(w, 2, 3), col(w, 5, 6)]
        t_bias = [col(bias, 0, 1), col(bias, 3, 4), col(bias, 2, 3), col(bias, 5, 6)]
        n_t = sum(r.shape[1] for r in t_rows)
        t_pad = (-n_t) % (2 * SUBLANES)
        wt = jnp.pad(jnp.concatenate(t_rows, axis=1).T, ((0, t_pad), (0, 0))).astype(BF16)
        bt = jnp.pad(jnp.concatenate(t_bias), (0, t_pad)).reshape(-1, 1)
        k_pad = (-(widths[1] + widths[4])) % LANES
        wk = jnp.pad(jnp.concatenate([col(w, 1, 2), col(w, 4, 5)], axis=1), ((0, 0), (0, k_pad))).astype(BF16)
        bk = jnp.pad(jnp.concatenate([col(bias, 1, 2), col(bias, 4, 5)]), (0, k_pad)).reshape(1, -1)
        weights = [wt, bt, wk, bk]
        for i in range(6, 10):
            weights += [col(w, i, i + 1).astype(BF16), col(bias, i, i + 1).reshape(1, -1)]
        qt, qit, vt, wit, k, ki, xr, gy, sga, sgb = _in_proj(h, sc1, sh1, g_mix[l].reshape(1, d), weights, row_tile)

        attn = _attention(qt, qit, wit, k, vt, ki, rel_bias, tq)
        rnn = _rglru(xr, gy, conv_rnn_w[l], conv_rnn_b[l].reshape(1, d), w_rg[l].astype(BF16),
                     b_rg[l].reshape(1, d), w_ig[l].astype(BF16), b_ig[l].reshape(1, d),
                     lru_lambda[l].reshape(1, d), cblk=min(d, 512))
        wu, cw, cb = w_up[l], conv_ffn_w[l], conv_ffn_b[l].reshape(1, -1)
        h = _merge_ffn(h, attn, rnn, sga, sgb, ga1, w_o_attn[l].astype(BF16), w_o_rnn[l].astype(BF16),
                       w_out[l].astype(BF16), sc2, sh2, ga2, g_ffn[l].reshape(1, d), g_final.reshape(1, d),
                       wu[:, :dff].astype(BF16), wu[:, dff:].astype(BF16), cw[:, :dff], cw[:, dff:],
                       cb[:, :dff], cb[:, dff:], w_down[l].astype(BF16), row_tile, fchunk=256)
    return h
```

```python
import functools
import math

import numpy as np
import jax
import jax.numpy as jnp
from jax import lax
from jax.experimental import pallas as pl
from jax.experimental.pallas import tpu as pltpu

N_HEADS = 8
HEAD_DIM = 64
N_KV_HEADS = 2
IDX_HEADS = 8
IDX_DIM = 64
TOPK_MAX = 256
N_BUCKETS = 32
MAX_DISTANCE = 128
N_RNN_BLOCKS = 8
RNN_CONV = 4
LRU_C = 8.0
FFN_CONV = 3
EPS = 1e-6

LANES = 128
SUBLANES = 8
VMEM_LIMIT = 56 * 1024 * 1024

MASK_NEG = -1e30
KEY_NEG_INF = -2139095041

BF16 = jnp.bfloat16
F32 = jnp.float32


def _cparams(n_grid):
    return pltpu.CompilerParams(
        dimension_semantics=("arbitrary",) * n_grid, vmem_limit_bytes=VMEM_LIMIT)


def _resident(shape):
    nd = len(shape)
    return pl.BlockSpec(shape, lambda *_: (0,) * nd, pipeline_mode=pl.Buffered(1))


def _dot(a, b):
    return jnp.dot(a, b, preferred_element_type=F32)


def _dot_nt(a, b):
    return lax.dot_general(a, b, (((1,), (1,)), ((), ())), preferred_element_type=F32)


def _adaln_kernel(c_ref, w_ref, b_ref, o_ref):
    c = c_ref[...]
    c_act = (c * jax.nn.sigmoid(c)).astype(BF16)
    o_ref[...] = _dot(c_act, w_ref[...].astype(BF16)) + b_ref[...]


def _adaln(c, w, b):
    bsz, d = c.shape
    n = w.shape[1]
    return pl.pallas_call(
        _adaln_kernel,
        grid=(n // d,),
        in_specs=[pl.BlockSpec((bsz, d), lambda j: (0, 0)),
                  pl.BlockSpec((d, d), lambda j: (0, j)),
                  pl.BlockSpec((1, d), lambda j: (0, j))],
        out_specs=pl.BlockSpec((bsz, d), lambda j: (0, j)),
        out_shape=jax.ShapeDtypeStruct((bsz, n), F32),
        compiler_params=_cparams(1),
        name="adaln",
    )(c, w, b.reshape(1, n))


def _modulated_norm(x, g, sc, sh):
    ms = jnp.mean(x * x, axis=-1, keepdims=True)
    return (x * lax.rsqrt(ms + EPS) * g) * (1.0 + sc) + sh


def _in_proj_kernel(x_ref, sc_ref, sh_ref, g_ref, wt_ref, bt_ref, wk_ref, bk_ref,
                    wr_ref, br_ref, wy_ref, by_ref, wga_ref, bga_ref, wgb_ref, bgb_ref,
                    qt_ref, qit_ref, vt_ref, wit_ref, k_ref, ki_ref, xr_ref, gy_ref, sga_ref, sgb_ref):
    xn = _modulated_norm(x_ref[...], g_ref[...], sc_ref[...], sh_ref[...]).astype(BF16)

    res = _dot_nt(wt_ref[...], xn) + bt_ref[...]
    r0 = 0
    for h in range(N_HEADS):
        qt_ref[h] = (res[r0 + h * HEAD_DIM:r0 + (h + 1) * HEAD_DIM] * (HEAD_DIM ** -0.5)).astype(BF16)
    r0 += N_HEADS * HEAD_DIM
    for h in range(IDX_HEADS):
        qit_ref[h] = (res[r0 + h * IDX_DIM:r0 + (h + 1) * IDX_DIM] * (IDX_DIM ** -0.5)).astype(BF16)
    r0 += IDX_HEADS * IDX_DIM
    for j in range(N_KV_HEADS):
        vt_ref[j] = res[r0 + j * HEAD_DIM:r0 + (j + 1) * HEAD_DIM].astype(BF16)
    r0 += N_KV_HEADS * HEAD_DIM
    wit_ref[...] = res[r0:r0 + IDX_HEADS] * (IDX_HEADS ** -0.5)

    res = _dot(xn, wk_ref[...]) + bk_ref[...]
    for j in range(N_KV_HEADS):
        k_ref[j] = res[:, j * HEAD_DIM:(j + 1) * HEAD_DIM].astype(BF16)
    ki0 = N_KV_HEADS * HEAD_DIM
    ki_ref[...] = res[:, ki0:ki0 + IDX_DIM].astype(BF16)

    xr_ref[...] = (_dot(xn, wr_ref[...]) + br_ref[...]).astype(BF16)
    gy_ref[...] = jax.nn.gelu(_dot(xn, wy_ref[...]) + by_ref[...]).astype(BF16)
    sga_ref[...] = jax.nn.sigmoid(_dot(xn, wga_ref[...]) + bga_ref[...]).astype(BF16)
    sgb_ref[...] = jax.nn.sigmoid(_dot(xn, wgb_ref[...]) + bgb_ref[...]).astype(BF16)


def _in_proj(x, sc, sh, g, weights, tile):
    bsz, seq, d = x.shape
    row = lambda w: pl.BlockSpec((None, tile, w), lambda b, s: (b, s, 0))
    heads = lambda n, w: pl.BlockSpec((None, n, tile, w), lambda b, s: (b, 0, s, 0))
    heads_t = lambda n, w: pl.BlockSpec((None, n, w, tile), lambda b, s: (b, 0, 0, s))
    per_batch = pl.BlockSpec((None, 1, d), lambda b, s: (b, 0, 0))
    in_specs = [row(d), per_batch, per_batch, _resident((1, d))]
    in_specs += [_resident(w.shape) for w in weights]
    bsd = lambda w, dt: jax.ShapeDtypeStruct((bsz, seq, w), dt)
    out_shape = [
        jax.ShapeDtypeStruct((bsz, N_HEADS, HEAD_DIM, seq), BF16),
        jax.ShapeDtypeStruct((bsz, IDX_HEADS, IDX_DIM, seq), BF16),
        jax.ShapeDtypeStruct((bsz, N_KV_HEADS, HEAD_DIM, seq), BF16),
        jax.ShapeDtypeStruct((bsz, IDX_HEADS, seq), F32),
        jax.ShapeDtypeStruct((bsz, N_KV_HEADS, seq, HEAD_DIM), BF16),
        bsd(IDX_DIM, BF16),
        bsd(d, BF16), bsd(d, BF16), bsd(d, BF16), bsd(d, BF16)]
    out_specs = [heads_t(N_HEADS, HEAD_DIM), heads_t(IDX_HEADS, IDX_DIM), heads_t(N_KV_HEADS, HEAD_DIM),
                 pl.BlockSpec((None, IDX_HEADS, tile), lambda b, s: (b, 0, s)),
                 heads(N_KV_HEADS, HEAD_DIM), row(IDX_DIM),
                 row(d), row(d), row(d), row(d)]
    return pl.pallas_call(
        _in_proj_kernel,
        grid=(bsz, seq // tile),
        in_specs=in_specs, out_specs=out_specs, out_shape=out_shape,
        compiler_params=_cparams(2),
        name="in_proj",
    )(x, sc, sh, g, *weights)


def _t5_bucket_np(rel):
    max_exact = N_BUCKETS // 2
    n = np.maximum(rel, 0)
    nf = np.maximum(n, 1).astype(np.float64)
    large = max_exact + (np.log(nf / max_exact) / math.log(MAX_DISTANCE / max_exact)
                         * (N_BUCKETS - max_exact)).astype(np.int32)
    large = np.minimum(large, N_BUCKETS - 1)
    return np.where(n < max_exact, n, large).astype(np.int32)


def _far_distance():
    d = np.arange(0, 4 * MAX_DISTANCE)
    b = _t5_bucket_np(d)
    assert b[-1] == N_BUCKETS - 1
    return int(np.max(np.nonzero(b != N_BUCKETS - 1)[0])) + 1


def _sort_key(s):
    bits = lax.bitcast_convert_type(s, jnp.int32)
    return bits ^ ((bits >> 31) & jnp.int32(0x7FFFFFFF))


def _key_to_float(k):
    return lax.bitcast_convert_type(k ^ ((k >> 31) & jnp.int32(0x7FFFFFFF)), F32)


VALUE_STEPS = 12
SEARCH_UNROLL = 4


def _attn_kernel(qt_ref, qit_ref, wit_ref, k_ref, vt_ref, ki_ref, bucket_ref, relb_ref, tri_ref,
                 o_ref, key_ref, madd_ref, bias_ref, m_ref, acc_ref, ot_ref, xa_ref, xb_ref, p_ref,
                 *, tq, n_sel, n_near):
    b = pl.program_id(0)
    qt = pl.program_id(1)
    n_chunks = qt + 1
    tk = tq
    int_max = jnp.iinfo(jnp.int32).max

    @pl.when((b == 0) & (qt == 0))
    def _():
        for dlt in range(n_near + 1):
            bucket = bucket_ref[dlt]
            for h in range(N_HEADS):
                tile = jnp.full((tk, tq), relb_ref[N_BUCKETS - 1, h], F32)
                for bk in range(N_BUCKETS - 1):
                    tile = jnp.where(bucket == bk, relb_ref[bk, h], tile)
                bias_ref[dlt * N_HEADS + h] = tile

    key_pos = lax.broadcasted_iota(jnp.int32, (tk, tq), 0)
    qry_pos = lax.broadcasted_iota(jnp.int32, (tk, tq), 1)
    t_row = qt * tq + lax.broadcasted_iota(jnp.int32, (1, tq), 1)

    def split(v):
        return v.reshape(v.shape[0] // SUBLANES, SUBLANES, v.shape[1])

    def score_chunk(c, carry):
        kmin, kmax, npos, nzero = carry
        off = pl.multiple_of(c * tk, tk)
        kic = ki_ref[pl.ds(off, tk), :]
        s = None
        for h in range(IDX_HEADS):
            d = _dot(kic, qit_ref[h])
            term = wit_ref[h:h + 1, :] * jnp.maximum(d, 0.0)
            s = term if s is None else s + term
        valid = (c < qt) | (key_pos <= qry_pos)
        s = jnp.where(valid, s, -jnp.inf)
        is_zero = s == 0.0
        key = jnp.where(is_zero, 0, _sort_key(s))
        key_ref[pl.ds(off, tk), :] = key
        kmin = jnp.minimum(kmin, jnp.min(split(jnp.where(valid, key, int_max)), axis=0))
        kmax = jnp.maximum(kmax, jnp.max(split(key), axis=0))
        npos = npos + jnp.sum(split(jnp.where(s > 0.0, 1, 0)), axis=0)
        nzero = nzero + jnp.sum(split(jnp.where(is_zero, 1, 0)), axis=0)
        return kmin, kmax, npos, nzero

    zeros8 = jnp.zeros((SUBLANES, tq), jnp.int32)
    kmin, kmax, npos, nzero = lax.fori_loop(
        0, n_chunks, score_chunk,
        (jnp.full((SUBLANES, tq), int_max, jnp.int32), jnp.full((SUBLANES, tq), KEY_NEG_INF, jnp.int32),
         zeros8, zeros8))
    kmin = jnp.min(kmin, axis=0, keepdims=True)
    kmax = jnp.max(kmax, axis=0, keepdims=True)
    npos = jnp.sum(npos, axis=0, keepdims=True)
    nneg0 = npos + jnp.sum(nzero, axis=0, keepdims=True)

    need = jnp.minimum(t_row + 1, n_sel)

    def count_ge(mid):
        mid8 = jnp.broadcast_to(mid, (SUBLANES, tq))

        def body(c, acc):
            off = pl.multiple_of(c * tk, tk)
            return acc + jnp.sum(jnp.where(split(key_ref[pl.ds(off, tk), :]) >= mid8[None], 1, 0), axis=0)
        acc = lax.fori_loop(0, n_chunks, body, jnp.zeros((SUBLANES, tq), jnp.int32))
        return jnp.sum(acc, axis=0, keepdims=True)

    def search_cond(st):
        return jnp.min(st[-1]) == 0

    def search_steps(st):
        for _ in range(SEARCH_UNROLL):
            st = search_step(st)
        return st

    def search_step(st):
        it, lo, hi, cnt_lo, cnt_hi, done = st
        half = (lo >> 1) + (hi >> 1) + (lo & hi & 1)
        mid_f = 0.5 * _key_to_float(lo) + 0.5 * _key_to_float(hi)
        by_value = jnp.minimum(jnp.maximum(_sort_key(mid_f), lo + 1), hi - 1)
        mid = jnp.where(it < VALUE_STEPS, by_value, half)
        cnt = count_ge(mid)
        live = done == 0
        up = live & (cnt >= need)
        dn = live & (cnt < need)
        lo = jnp.where(up, mid, lo)
        cnt_lo = jnp.where(up, cnt, cnt_lo)
        hi = jnp.where(dn, mid, hi)
        cnt_hi = jnp.where(dn, cnt, cnt_hi)
        done = jnp.where((cnt_lo == need) | (hi == lo + 1), 1, done)
        return it + 1, lo, hi, cnt_lo, cnt_hi, done

    above = need <= npos
    below = need > nneg0
    lo0 = jnp.where(above, 1, jnp.where(below, kmin, 0))
    cnt_lo0 = jnp.where(above, npos, jnp.where(below, t_row + 1, nneg0))
    hi0 = jnp.where(above, jnp.minimum(kmax, int_max - 1) + 1, jnp.where(below, 0, 1))
    cnt_hi0 = jnp.where(above, 0, jnp.where(below, nneg0, npos))
    done0 = jnp.where((cnt_lo0 == need) | (hi0 == lo0 + 1), 1, 0)
    _, lo, hi, cnt_lo, cnt_hi, _ = lax.while_loop(
        search_cond, search_steps, (jnp.int32(0), lo0, hi0, cnt_lo0, cnt_hi0, done0))

    has_tie = jnp.max(jnp.where(cnt_lo > need, 1, 0)) > 0

    @pl.when(jnp.logical_not(has_tie))
    def _():
        def body(c, carry):
            off = pl.multiple_of(c * tk, tk)
            madd_ref[pl.ds(off, tk), :] = jnp.where(key_ref[pl.ds(off, tk), :] >= lo, 0.0, MASK_NEG)
            return carry
        lax.fori_loop(0, n_chunks, body, 0)

    @pl.when(has_tie)
    def _():
        tri = tri_ref[...]
        quota = jnp.where(cnt_lo > need, need - cnt_hi, n_sel).astype(F32)

        def body(c, seen):
            off = pl.multiple_of(c * tk, tk)
            kc = key_ref[pl.ds(off, tk), :]
            eq = kc == lo
            eqf = jnp.where(eq, 1.0, 0.0)
            rank = _dot(tri, eqf.astype(BF16)) + seen
            sel = (kc > lo) | (eq & (rank <= quota))
            madd_ref[pl.ds(off, tk), :] = jnp.where(sel, 0.0, MASK_NEG)
            return seen + jnp.sum(eqf, axis=0, keepdims=True)
        lax.fori_loop(0, n_chunks, body, jnp.zeros((1, tq), F32))

    m_ref[...] = jnp.full(m_ref.shape, MASK_NEG, F32)
    acc_ref[...] = jnp.zeros(acc_ref.shape, F32)
    group = N_HEADS // N_KV_HEADS
    ones_rows = jnp.ones((2 * SUBLANES, tk), BF16)

    def masked_logits(c, h):
        off = pl.multiple_of(c * tk, tk)
        near = jnp.minimum(qt - c, n_near) * N_HEADS
        return (_dot(k_ref[h // group, pl.ds(off, tk), :], qt_ref[h])
                + madd_ref[pl.ds(off, tk), :] + bias_ref[near + h])

    def logits(c, x_ref):
        for h in range(N_HEADS):
            x_ref[h] = masked_logits(c, h)

    def softmax_pv(c, x_ref, nxt, xn_ref):
        off = pl.multiple_of(c * tk, tk)
        vt_ext = [jnp.concatenate([vt_ref[kvh, :, pl.ds(off, tk)], ones_rows], axis=0)
                  for kvh in range(N_KV_HEADS)]

        def pv(h, alpha):
            prod = _dot(vt_ext[h // group], p_ref[h])[:HEAD_DIM + SUBLANES]
            acc_ref[h] = (alpha[None] * split(acc_ref[h])).reshape(HEAD_DIM + SUBLANES, tq) + prod

        prev_alpha = None
        for h in range(N_HEADS):
            xn_ref[h] = masked_logits(nxt, h)
            mx = jnp.max(split(x_ref[h]), axis=0)
            for shift in (4, 2, 1):
                mx = jnp.maximum(mx, pltpu.roll(mx, shift, 0))
            m_old = m_ref[h]
            m_new = jnp.maximum(m_old, mx)
            alpha = jnp.exp(m_old - m_new)
            m_ref[h] = m_new
            p = jnp.exp(split(x_ref[h]) - m_new[None])
            p_ref[h] = p.reshape(tk, tq).astype(BF16)
            if h > 0:
                pv(h - 1, prev_alpha)
            prev_alpha = alpha
        pv(N_HEADS - 1, prev_alpha)

    logits(0, xa_ref)

    def attend(c, carry):
        nxt = jnp.minimum(c + 1, qt)

        @pl.when(c % 2 == 0)
        def _():
            softmax_pv(c, xa_ref, nxt, xb_ref)

        @pl.when(c % 2 == 1)
        def _():
            softmax_pv(c, xb_ref, nxt, xa_ref)
        return carry

    lax.fori_loop(0, n_chunks, attend, 0)

    for h in range(N_HEADS):
        ot_ref[h * HEAD_DIM:(h + 1) * HEAD_DIM, :] = (
            acc_ref[h, 0:HEAD_DIM, :] / acc_ref[h, HEAD_DIM:HEAD_DIM + 1, :])
    o_ref[...] = ot_ref[...].T.astype(o_ref.dtype)


def _attention(qt, qit, wit, k, vt, ki, rel_bias, tq):
    bsz, _, _, seq = qt.shape
    n_sel = min(TOPK_MAX, seq // 4)
    far = _far_distance()
    n_near = min(seq // tq, (far - 1 + tq - 1) // tq + 1)
    i = np.arange(tq)[:, None]
    j = np.arange(tq)[None, :]
    bucket = np.stack([_t5_bucket_np(dlt * tq + j - i) for dlt in range(n_near)]
                      + [np.full((tq, tq), N_BUCKETS - 1, np.int32)])
    tri = (j <= i).astype(np.float32)

    kernel = functools.partial(_attn_kernel, tq=tq, n_sel=n_sel, n_near=n_near)
    qblk = lambda n, w: pl.BlockSpec((None, n, w, tq), lambda b, t: (b, 0, 0, t))
    return pl.pallas_call(
        kernel,
        grid=(bsz, seq // tq),
        in_specs=[qblk(N_HEADS, HEAD_DIM), qblk(IDX_HEADS, IDX_DIM),
                  pl.BlockSpec((None, IDX_HEADS, tq), lambda b, t: (b, 0, t)),
                  pl.BlockSpec((None, N_KV_HEADS, seq, HEAD_DIM), lambda b, t: (b, 0, 0, 0)),
                  pl.BlockSpec((None, N_KV_HEADS, HEAD_DIM, seq), lambda b, t: (b, 0, 0, 0)),
                  pl.BlockSpec((None, seq, IDX_DIM), lambda b, t: (b, 0, 0)),
                  _resident((n_near + 1, tq, tq)),
                  pl.BlockSpec(memory_space=pltpu.SMEM),
                  _resident((tq, tq))],
        out_specs=pl.BlockSpec((None, tq, N_HEADS * HEAD_DIM), lambda b, t: (b, t, 0)),
        out_shape=jax.ShapeDtypeStruct((bsz, seq, N_HEADS * HEAD_DIM), BF16),
        scratch_shapes=[pltpu.VMEM((seq, tq), jnp.int32),
                        pltpu.VMEM((seq, tq), F32),
                        pltpu.VMEM(((n_near + 1) * N_HEADS, tq, tq), F32),
                        pltpu.VMEM((N_HEADS, SUBLANES, tq), F32),
                        pltpu.VMEM((N_HEADS, HEAD_DIM + SUBLANES, tq), F32),
                        pltpu.VMEM((N_HEADS * HEAD_DIM, tq), F32),
                        pltpu.VMEM((N_HEADS, tq, tq), F32),
                        pltpu.VMEM((N_HEADS, tq, tq), F32),
                        pltpu.VMEM((N_HEADS, tq, tq), BF16)],
        compiler_params=_cparams(2),
        name="attention",
    )(qt, qit, wit, k, vt, ki, jnp.asarray(bucket), rel_bias, jnp.asarray(tri, BF16))


N_SEG = SUBLANES


def _seg_pitch(seg_len):
    p = seg_len // SUBLANES + 1
    if p % 2 == 0:
        p += 1
    return p * SUBLANES


def _rglru_kernel(xr_ref, gy_ref, cw_ref, cb_ref, wrg_ref, brg_ref, wig_ref, big_ref, lam_ref,
                  o_ref, xp_ref, a_ref, u_ref, *, seq, n_grp):
    seg = seq // N_SEG
    pitch = _seg_pitch(seg)
    pad = SUBLANES

    xp_ref[0:pad, :] = jnp.zeros((pad, LANES), F32)
    for g in range(n_grp):
        ls = slice(g * LANES, (g + 1) * LANES)
        xp_ref[pad:pad + seq, :] = xr_ref[:, ls].astype(F32)
        cw = cw_ref[:, ls]
        xc = cb_ref[:, ls] + cw[0:1] * xp_ref[pl.ds(pad - 3, seq), :]
        for kk in range(1, RNN_CONV):
            xc = xc + cw[kk:kk + 1] * xp_ref[pl.ds(pad - 3 + kk, seq), :]
        xb = xc.astype(BF16)
        r = jax.nn.sigmoid(_dot(xb, wrg_ref[g]) + brg_ref[:, ls])
        ig = jax.nn.sigmoid(_dot(xb, wig_ref[g]) + big_ref[:, ls])
        z = -lam_ref[:, ls]
        softplus = jnp.maximum(z, 0.0) + jnp.log1p(jnp.exp(-jnp.abs(z)))
        log_a = (-LRU_C * softplus) * r
        a = jnp.exp(log_a)
        u = xc * ig * jnp.sqrt(-jnp.tanh(log_a) * (1.0 + a * a))
        for j in range(N_SEG):
            a_ref[g, j * pitch:j * pitch + seg, :] = a[j * seg:(j + 1) * seg]
            u_ref[g, j * pitch:j * pitch + seg, :] = u[j * seg:(j + 1) * seg]

    def step(t, st):
        new = []
        for g in range(n_grp):
            h, p = st[g]
            a_t = a_ref[g, pl.ds(t, N_SEG, stride=pitch), :]
            u_t = u_ref[g, pl.ds(t, N_SEG, stride=pitch), :]
            h = a_t * h + u_t
            p = p * a_t
            u_ref[g, pl.ds(t, N_SEG, stride=pitch), :] = h
            a_ref[g, pl.ds(t, N_SEG, stride=pitch), :] = p
            new.append((h, p))
        return tuple(new)

    init = tuple((jnp.zeros((N_SEG, LANES), F32), jnp.ones((N_SEG, LANES), F32)) for _ in range(n_grp))
    final = lax.fori_loop(0, seg, step, init)

    for g in range(n_grp):
        ls = slice(g * LANES, (g + 1) * LANES)
        h_end, p_end = final[g]
        carry = jnp.zeros((1, LANES), F32)
        for j in range(N_SEG):
            rows = slice(j * pitch, j * pitch + seg)
            hj = u_ref[g, rows, :] + a_ref[g, rows, :] * carry
            o_ref[j * seg:(j + 1) * seg, ls] = (hj * gy_ref[j * seg:(j + 1) * seg, ls].astype(F32)).astype(o_ref.dtype)
            carry = h_end[j:j + 1] + p_end[j:j + 1] * carry


def _rglru(xr, gy, cw, cb, wrg, brg, wig, big, lam, cblk):
    bsz, seq, d = xr.shape
    n_grp = cblk // LANES
    pitch = _seg_pitch(seq // N_SEG)
    kernel = functools.partial(_rglru_kernel, seq=seq, n_grp=n_grp)
    act = pl.BlockSpec((None, seq, cblk), lambda b, c: (b, 0, c))
    vec = lambda r: pl.BlockSpec((r, cblk), lambda b, c: (0, c))
    gate_w = pl.BlockSpec((n_grp, LANES, LANES), lambda b, c: (c, 0, 0))
    return pl.pallas_call(
        kernel,
        grid=(bsz, d // cblk),
        in_specs=[act, act, vec(RNN_CONV), vec(1), gate_w, vec(1), gate_w, vec(1), vec(1)],
        out_specs=act,
        out_shape=jax.ShapeDtypeStruct((bsz, seq, d), BF16),
        scratch_shapes=[pltpu.VMEM((seq + SUBLANES, LANES), F32),
                        pltpu.VMEM((n_grp, N_SEG * pitch, LANES), F32),
                        pltpu.VMEM((n_grp, N_SEG * pitch, LANES), F32)],
        compiler_params=_cparams(2),
        name="rglru",
    )(xr, gy, cw, cb, wrg, brg, wig, big, lam)


def _merge_ffn_kernel(x_ref, attn_ref, rnn_ref, sga_ref, sgb_ref, ga1_ref, woa_ref, wor_ref, wout_ref,
                      sc_ref, sh_ref, ga2_ref, g_ref, gf_ref, wv_ref, wg_ref, cwv_ref, cwg_ref,
                      cbv_ref, cbg_ref, wd_ref, o_ref, buf_ref, carry_ref, act_ref, *, tile, fchunk, n_fchunks):
    s = pl.program_id(1)
    pad = SUBLANES

    @pl.when(s == 0)
    def _():
        carry_ref[...] = jnp.zeros(carry_ref.shape, F32)

    merged = (sga_ref[...].astype(F32) * _dot(attn_ref[...], woa_ref[...])
              + sgb_ref[...].astype(F32) * _dot(rnn_ref[...], wor_ref[...]))
    hres = x_ref[...] + ga1_ref[...] * _dot(merged.astype(BF16), wout_ref[...])
    xn = _modulated_norm(hres, g_ref[...], sc_ref[...], sh_ref[...]).astype(BF16)

    def conv(up, idx, cw, cb):
        buf_ref[0:pad, :] = carry_ref[idx]
        buf_ref[pad:pad + tile, :] = up
        carry_ref[idx] = up[tile - pad:tile]
        y = cb + cw[FFN_CONV - 1:FFN_CONV] * up
        for kk in range(FFN_CONV - 1):
            y = y + cw[kk:kk + 1] * buf_ref[pl.ds(pad - (FFN_CONV - 1) + kk, tile), :]
        return y

    for c in range(n_fchunks):
        cs = slice(c * fchunk, (c + 1) * fchunk)
        val = conv(_dot(xn, wv_ref[:, cs]), 2 * c, cwv_ref[:, cs], cbv_ref[:, cs])
        gte = conv(_dot(xn, wg_ref[:, cs]), 2 * c + 1, cwg_ref[:, cs], cbg_ref[:, cs])
        act_ref[:, cs] = ((gte * jax.nn.sigmoid(gte)) * val).astype(BF16)

    h2 = hres + ga2_ref[...] * _dot(act_ref[...], wd_ref[...])
    ms = jnp.mean(h2 * h2, axis=-1, keepdims=True)
    o_ref[...] = h2 * lax.rsqrt(ms + EPS) * gf_ref[...]


def _merge_ffn(x, attn, rnn, sga, sgb, ga1, woa, wor, wout, sc, sh, ga2, g, gf,
               wv, wg, cwv, cwg, cbv, cbg, wd, tile, fchunk):
    bsz, seq, d = x.shape
    dff = wv.shape[1]
    n_fchunks = dff // fchunk
    kernel = functools.partial(_merge_ffn_kernel, tile=tile, fchunk=fchunk, n_fchunks=n_fchunks)
    row = lambda w: pl.BlockSpec((None, tile, w), lambda b, s: (b, s, 0))
    per_batch = pl.BlockSpec((None, 1, d), lambda b, s: (b, 0, 0))
    merge_w = [woa, wor, wout]
    ffn_w = [g, gf, wv, wg, cwv, cwg, cbv, cbg, wd]
    return pl.pallas_call(
        kernel,
        grid=(bsz, seq // tile),
        in_specs=([row(d), row(attn.shape[-1]), row(d), row(d), row(d), per_batch]
                  + [_resident(a.shape) for a in merge_w]
                  + [per_batch, per_batch, per_batch]
                  + [_resident(a.shape) for a in ffn_w]),
        out_specs=row(d),
        out_shape=jax.ShapeDtypeStruct((bsz, seq, d), F32),
        scratch_shapes=[pltpu.VMEM((tile + SUBLANES, fchunk), F32),
                        pltpu.VMEM((2 * n_fchunks, SUBLANES, fchunk), F32),
                        pltpu.VMEM((tile, dff), BF16)],
        compiler_params=_cparams(2),
        name="merge_ffn",
    )(x, attn, rnn, sga, sgb, ga1, *merge_w, sc, sh, ga2, *ffn_w)


def _pick(seq, pref):
    t = min(seq, pref)
    assert seq % t == 0
    return t


def kernel(x, c, w_ada, b_ada, g_mix, w_in, b_in, rel_bias, conv_rnn_w, conv_rnn_b, w_rg, b_rg, w_ig, b_ig, lru_lambda, w_o_attn, w_o_rnn, w_out, g_ffn, w_up, conv_ffn_w, conv_ffn_b, w_down, g_final):
    bsz, seq, d = x.shape
    depth = w_ada.shape[0]
    assert depth == 1 and d == N_RNN_BLOCKS * LANES
    dff = w_down.shape[1]
    row_tile = _pick(seq, 512)
    tq = _pick(seq, 256)

    widths = (N_HEADS * HEAD_DIM, N_KV_HEADS * HEAD_DIM, N_KV_HEADS * HEAD_DIM, IDX_HEADS * IDX_DIM,
              IDX_DIM, IDX_HEADS, d, d, d, d)
    cuts = np.cumsum((0,) + widths)
    col = lambda a, i0, i1: a[..., cuts[i0]:cuts[i1]]

    h = x
    for l in range(depth):
        mod = _adaln(c, w_ada[l], b_ada[l])
        sh1, sc1, ga1, sh2, sc2, ga2 = [m[:, None, :] for m in jnp.split(mod, 6, axis=-1)]

        w, bias = w_in[l], b_in[l]
        t_rows = [col(w, 0, 1), col(w, 3, 4), col(w, 2, 3), col(w, 5, 6)]
        t_bias = [col(bias, 0, 1), col(bias, 3, 4), col(bias, 2, 3), col(bias, 5, 6)]
        n_t = sum(r.shape[1] for r in t_rows)
        t_pad = (-n_t) % (2 * SUBLANES)
        wt = jnp.pad(jnp.concatenate(t_rows, axis=1).T, ((0, t_pad), (0, 0))).astype(BF16)
        bt = jnp.pad(jnp.concatenate(t_bias), (0, t_pad)).reshape(-1, 1)
        k_pad = (-(widths[1] + widths[4])) % LANES
        wk = jnp.pad(jnp.concatenate([col(w, 1, 2), col(w, 4, 5)], axis=1), ((0, 0), (0, k_pad))).astype(BF16)
        bk = jnp.pad(jnp.concatenate([col(bias, 1, 2), col(bias, 4, 5)]), (0, k_pad)).reshape(1, -1)
        weights = [wt, bt, wk, bk]
        for i in range(6, 10):
            weights += [col(w, i, i + 1).astype(BF16), col(bias, i, i + 1).reshape(1, -1)]
        qt, qit, vt, wit, k, ki, xr, gy, sga, sgb = _in_proj(h, sc1, sh1, g_mix[l].reshape(1, d), weights, row_tile)

        attn = _attention(qt, qit, wit, k, vt, ki, rel_bias, tq)
        rnn = _rglru(xr, gy, conv_rnn_w[l], conv_rnn_b[l].reshape(1, d), w_rg[l].astype(BF16),
                     b_rg[l].reshape(1, d), w_ig[l].astype(BF16), b_ig[l].reshape(1, d),
                     lru_lambda[l].reshape(1, d), cblk=min(d, 512))
        wu, cw, cb = w_up[l], conv_ffn_w[l], conv_ffn_b[l].reshape(1, -1)
        h = _merge_ffn(h, attn, rnn, sga, sgb, ga1, w_o_attn[l].astype(BF16), w_o_rnn[l].astype(BF16),
                       w_out[l].astype(BF16), sc2, sh2, ga2, g_ffn[l].reshape(1, d), g_final.reshape(1, d),
                       wu[:, :dff].astype(BF16), wu[:, dff:].astype(BF16), cw[:, :dff], cw[:, dff:],
                       cb[:, :dff], cb[:, dff:], w_down[l].astype(BF16), row_tile, fchunk=256)
    return h
```

```python
import functools
import math

import numpy as np
import jax
import jax.numpy as jnp
from jax import lax
from jax.experimental import pallas as pl
from jax.experimental.pallas import tpu as pltpu

N_HEADS = 8
HEAD_DIM = 64
N_KV_HEADS = 2
IDX_HEADS = 8
IDX_DIM = 64
TOPK_MAX = 256
N_BUCKETS = 32
MAX_DISTANCE = 128
N_RNN_BLOCKS = 8
RNN_CONV = 4
LRU_C = 8.0
FFN_CONV = 3
EPS = 1e-6

LANES = 128
SUBLANES = 8
VMEM_LIMIT = 56 * 1024 * 1024

MASK_NEG = -1e30
KEY_NEG_INF = -2139095041

BF16 = jnp.bfloat16
F32 = jnp.float32


def _cparams(n_grid):
    return pltpu.CompilerParams(
        dimension_semantics=("arbitrary",) * n_grid, vmem_limit_bytes=VMEM_LIMIT)


def _resident(shape):
    nd = len(shape)
    return pl.BlockSpec(shape, lambda *_: (0,) * nd, pipeline_mode=pl.Buffered(1))


def _dot(a, b):
    return jnp.dot(a, b, preferred_element_type=F32)


def _dot_nt(a, b):
    return lax.dot_general(a, b, (((1,), (1,)), ((), ())), preferred_element_type=F32)


def _adaln_kernel(c_ref, w_ref, b_ref, o_ref):
    c = c_ref[...]
    c_act = (c * jax.nn.sigmoid(c)).astype(BF16)
    o_ref[...] = _dot(c_act, w_ref[...].astype(BF16)) + b_ref[...]


def _adaln(c, w, b):
    bsz, d = c.shape
    n = w.shape[1]
    return pl.pallas_call(
        _adaln_kernel,
        grid=(n // d,),
        in_specs=[pl.BlockSpec((bsz, d), lambda j: (0, 0)),
                  pl.BlockSpec((d, d), lambda j: (0, j)),
                  pl.BlockSpec((1, d), lambda j: (0, j))],
        out_specs=pl.BlockSpec((bsz, d), lambda j: (0, j)),
        out_shape=jax.ShapeDtypeStruct((bsz, n), F32),
        compiler_params=_cparams(1),
        name="adaln",
    )(c, w, b.reshape(1, n))


def _modulated_norm(x, g, sc, sh):
    ms = jnp.mean(x * x, axis=-1, keepdims=True)
    return (x * lax.rsqrt(ms + EPS) * g) * (1.0 + sc) + sh


def _in_proj_kernel(x_ref, sc_ref, sh_ref, g_ref, wt_ref, bt_ref, wk_ref, bk_ref,
                    wr_ref, br_ref, wy_ref, by_ref, wga_ref, bga_ref, wgb_ref, bgb_ref,
                    qt_ref, qit_ref, vt_ref, wit_ref, k_ref, ki_ref, xr_ref, gy_ref, sga_ref, sgb_ref):
    xn = _modulated_norm(x_ref[...], g_ref[...], sc_ref[...], sh_ref[...]).astype(BF16)

    res = _dot_nt(wt_ref[...], xn) + bt_ref[...]
    r0 = 0
    for h in range(N_HEADS):
        qt_ref[h] = (res[r0 + h * HEAD_DIM:r0 + (h + 1) * HEAD_DIM] * (HEAD_DIM ** -0.5)).astype(BF16)
    r0 += N_HEADS * HEAD_DIM
    for h in range(IDX_HEADS):
        qit_ref[h] = (res[r0 + h * IDX_DIM:r0 + (h + 1) * IDX_DIM] * (IDX_DIM ** -0.5)).astype(BF16)
    r0 += IDX_HEADS * IDX_DIM
    for j in range(N_KV_HEADS):
        vt_ref[j] = res[r0 + j * HEAD_DIM:r0 + (j + 1) * HEAD_DIM].astype(BF16)
    r0 += N_KV_HEADS * HEAD_DIM
    wit_ref[...] = res[r0:r0 + IDX_HEADS] * (IDX_HEADS ** -0.5)

    res = _dot(xn, wk_ref[...]) + bk_ref[...]
    for j in range(N_KV_HEADS):
        k_ref[j] = res[:, j * HEAD_DIM:(j + 1) * HEAD_DIM].astype(BF16)
    ki0 = N_KV_HEADS * HEAD_DIM
    ki_ref[...] = res[:, ki0:ki0 + IDX_DIM].astype(BF16)

    xr_ref[...] = (_dot(xn, wr_ref[...]) + br_ref[...]).astype(BF16)
    gy_ref[...] = jax.nn.gelu(_dot(xn, wy_ref[...]) + by_ref[...]).astype(BF16)
    sga_ref[...] = jax.nn.sigmoid(_dot(xn, wga_ref[...]) + bga_ref[...]).astype(BF16)
    sgb_ref[...] = jax.nn.sigmoid(_dot(xn, wgb_ref[...]) + bgb_ref[...]).astype(BF16)


def _in_proj(x, sc, sh, g, weights, tile):
    bsz, seq, d = x.shape
    row = lambda w: pl.BlockSpec((None, tile, w), lambda b, s: (b, s, 0))
    heads = lambda n, w: pl.BlockSpec((None, n, tile, w), lambda b, s: (b, 0, s, 0))
    heads_t = lambda n, w: pl.BlockSpec((None, n, w, tile), lambda b, s: (b, 0, 0, s))
    per_batch = pl.BlockSpec((None, 1, d), lambda b, s: (b, 0, 0))
    in_specs = [row(d), per_batch, per_batch, _resident((1, d))]
    in_specs += [_resident(w.shape) for w in weights]
    bsd = lambda w, dt: jax.ShapeDtypeStruct((bsz, seq, w), dt)
    out_shape = [
        jax.ShapeDtypeStruct((bsz, N_HEADS, HEAD_DIM, seq), BF16),
        jax.ShapeDtypeStruct((bsz, IDX_HEADS, IDX_DIM, seq), BF16),
        jax.ShapeDtypeStruct((bsz, N_KV_HEADS, HEAD_DIM, seq), BF16),
        jax.ShapeDtypeStruct((bsz, IDX_HEADS, seq), F32),
        jax.ShapeDtypeStruct((bsz, N_KV_HEADS, seq, HEAD_DIM), BF16),
        bsd(IDX_DIM, BF16),
        bsd(d, BF16), bsd(d, BF16), bsd(d, BF16), bsd(d, BF16)]
    out_specs = [heads_t(N_HEADS, HEAD_DIM), heads_t(IDX_HEADS, IDX_DIM), heads_t(N_KV_HEADS, HEAD_DIM),
                 pl.BlockSpec((None, IDX_HEADS, tile), lambda b, s: (b, 0, s)),
                 heads(N_KV_HEADS, HEAD_DIM), row(IDX_DIM),
                 row(d), row(d), row(d), row(d)]
    return pl.pallas_call(
        _in_proj_kernel,
        grid=(bsz, seq // tile),
        in_specs=in_specs, out_specs=out_specs, out_shape=out_shape,
        compiler_params=_cparams(2),
        name="in_proj",
    )(x, sc, sh, g, *weights)


def _t5_bucket_np(rel):
    max_exact = N_BUCKETS // 2
    n = np.maximum(rel, 0)
    nf = np.maximum(n, 1).astype(np.float64)
    large = max_exact + (np.log(nf / max_exact) / math.log(MAX_DISTANCE / max_exact)
                         * (N_BUCKETS - max_exact)).astype(np.int32)
    large = np.minimum(large, N_BUCKETS - 1)
    return np.where(n < max_exact, n, large).astype(np.int32)


def _far_distance():
    d = np.arange(0, 4 * MAX_DISTANCE)
    b = _t5_bucket_np(d)
    assert b[-1] == N_BUCKETS - 1
    return int(np.max(np.nonzero(b != N_BUCKETS - 1)[0])) + 1


def _sort_key(s):
    bits = lax.bitcast_convert_type(s, jnp.int32)
    return bits ^ ((bits >> 31) & jnp.int32(0x7FFFFFFF))


def _key_to_float(k):
    return lax.bitcast_convert_type(k ^ ((k >> 31) & jnp.int32(0x7FFFFFFF)), F32)


VALUE_STEPS = 12
SEARCH_UNROLL = 4
SCORE_ROWS = 64


def _attn_kernel(qt_ref, qit_ref, wit_ref, k_ref, vt_ref, ki_ref, bucket_ref, relb_ref, tri_ref,
                 o_ref, key_ref, madd_ref, bias_ref, m_ref, acc_ref, ot_ref, xa_ref, xb_ref, p_ref, stat_ref,
                 *, tq, n_sel, n_near):
    b = pl.program_id(0)
    qt = pl.program_id(1)
    n_chunks = qt + 1
    tk = tq
    int_max = jnp.iinfo(jnp.int32).max

    @pl.when((b == 0) & (qt == 0))
    def _():
        for dlt in range(n_near + 1):
            bucket = bucket_ref[dlt]
            for h in range(N_HEADS):
                tile = jnp.full((tk, tq), relb_ref[N_BUCKETS - 1, h], F32)
                for bk in range(N_BUCKETS - 1):
                    tile = jnp.where(bucket == bk, relb_ref[bk, h], tile)
                bias_ref[dlt * N_HEADS + h] = tile

    t_row = qt * tq + lax.broadcasted_iota(jnp.int32, (1, tq), 1)

    def split(v):
        return v.reshape(v.shape[0] // SUBLANES, SUBLANES, v.shape[1])

    n_rb = tk // SCORE_ROWS
    heads_per_rb = IDX_HEADS // n_rb
    assert n_rb * SCORE_ROWS == tk and heads_per_rb * n_rb == IDX_HEADS
    key_pos = lax.broadcasted_iota(jnp.int32, (SCORE_ROWS, tq), 0)
    qry_pos = lax.broadcasted_iota(jnp.int32, (SCORE_ROWS, tq), 1)

    def idx_dots(c, d_ref, heads):
        kic = ki_ref[pl.ds(pl.multiple_of(c * tk, tk), tk), :]
        for h in heads:
            d_ref[h] = _dot(kic, qit_ref[h])

    def score_chunk(c, d_ref, nxt, dn_ref):
        off = pl.multiple_of(c * tk, tk)
        kmin, kmax, npos, nzero = stat_ref[0], stat_ref[1], stat_ref[2], stat_ref[3]
        for rb in range(n_rb):
            idx_dots(nxt, dn_ref, range(rb * heads_per_rb, (rb + 1) * heads_per_rb))
            rows = slice(rb * SCORE_ROWS, (rb + 1) * SCORE_ROWS)
            s = None
            for h in range(IDX_HEADS):
                term = wit_ref[h:h + 1, :] * jnp.maximum(d_ref[h, rows, :], 0.0)
                s = term if s is None else s + term
            valid = (c < qt) | (key_pos + rb * SCORE_ROWS <= qry_pos)
            s = jnp.where(valid, s, -jnp.inf)
            is_zero = s == 0.0
            key = jnp.where(is_zero, 0, _sort_key(s))
            key_ref[pl.ds(pl.multiple_of(off + rb * SCORE_ROWS, SCORE_ROWS), SCORE_ROWS), :] = key
            kmin = jnp.minimum(kmin, jnp.min(split(jnp.where(valid, key, int_max)), axis=0))
            kmax = jnp.maximum(kmax, jnp.max(split(key), axis=0))
            npos = npos + jnp.sum(split(jnp.where(s > 0.0, 1, 0)), axis=0)
            nzero = nzero + jnp.sum(split(jnp.where(is_zero, 1, 0)), axis=0)
        stat_ref[0], stat_ref[1], stat_ref[2], stat_ref[3] = kmin, kmax, npos, nzero

    stat_ref[0] = jnp.full((SUBLANES, tq), int_max, jnp.int32)
    stat_ref[1] = jnp.full((SUBLANES, tq), KEY_NEG_INF, jnp.int32)
    stat_ref[2] = jnp.zeros((SUBLANES, tq), jnp.int32)
    stat_ref[3] = jnp.zeros((SUBLANES, tq), jnp.int32)
    idx_dots(0, xa_ref, range(IDX_HEADS))

    def score_step(c, carry):
        nxt = jnp.minimum(c + 1, qt)

        @pl.when(c % 2 == 0)
        def _():
            score_chunk(c, xa_ref, nxt, xb_ref)

        @pl.when(c % 2 == 1)
        def _():
            score_chunk(c, xb_ref, nxt, xa_ref)
        return carry

    lax.fori_loop(0, n_chunks, score_step, 0)
    kmin, kmax, npos, nzero = stat_ref[0], stat_ref[1], stat_ref[2], stat_ref[3]
    kmin = jnp.min(kmin, axis=0, keepdims=True)
    kmax = jnp.max(kmax, axis=0, keepdims=True)
    npos = jnp.sum(npos, axis=0, keepdims=True)
    nneg0 = npos + jnp.sum(nzero, axis=0, keepdims=True)

    need = jnp.minimum(t_row + 1, n_sel)

    def count_ge(mid):
        mid8 = jnp.broadcast_to(mid, (SUBLANES, tq))

        def body(c, acc):
            off = pl.multiple_of(c * tk, tk)
            return acc + jnp.sum(jnp.where(split(key_ref[pl.ds(off, tk), :]) >= mid8[None], 1, 0), axis=0)
        acc = lax.fori_loop(0, n_chunks, body, jnp.zeros((SUBLANES, tq), jnp.int32))
        return jnp.sum(acc, axis=0, keepdims=True)

    def search_cond(st):
        return jnp.min(st[-1]) == 0

    def search_steps(st):
        for _ in range(SEARCH_UNROLL):
            st = search_step(st)
        return st

    def search_step(st):
        it, lo, hi, cnt_lo, cnt_hi, done = st
        half = (lo >> 1) + (hi >> 1) + (lo & hi & 1)
        mid_f = 0.5 * _key_to_float(lo) + 0.5 * _key_to_float(hi)
        by_value = jnp.minimum(jnp.maximum(_sort_key(mid_f), lo + 1), hi - 1)
        mid = jnp.where(it < VALUE_STEPS, by_value, half)
        cnt = count_ge(mid)
        live = done == 0
        up = live & (cnt >= need)
        dn = live & (cnt < need)
        lo = jnp.where(up, mid, lo)
        cnt_lo = jnp.where(up, cnt, cnt_lo)
        hi = jnp.where(dn, mid, hi)
        cnt_hi = jnp.where(dn, cnt, cnt_hi)
        done = jnp.where((cnt_lo == need) | (hi == lo + 1), 1, done)
        return it + 1, lo, hi, cnt_lo, cnt_hi, done

    above = need <= npos
    below = need > nneg0
    lo0 = jnp.where(above, 1, jnp.where(below, kmin, 0))
    cnt_lo0 = jnp.where(above, npos, jnp.where(below, t_row + 1, nneg0))
    hi0 = jnp.where(above, jnp.minimum(kmax, int_max - 1) + 1, jnp.where(below, 0, 1))
    cnt_hi0 = jnp.where(above, 0, jnp.where(below, nneg0, npos))
    done0 = jnp.where((cnt_lo0 == need) | (hi0 == lo0 + 1), 1, 0)
    _, lo, hi, cnt_lo, cnt_hi, _ = lax.while_loop(
        search_cond, search_steps, (jnp.int32(0), lo0, hi0, cnt_lo0, cnt_hi0, done0))

    has_tie = jnp.max(jnp.where(cnt_lo > need, 1, 0)) > 0

    @pl.when(jnp.logical_not(has_tie))
    def _():
        def body(c, carry):
            off = pl.multiple_of(c * tk, tk)
            madd_ref[pl.ds(off, tk), :] = jnp.where(key_ref[pl.ds(off, tk), :] >= lo, 0.0, MASK_NEG)
            return carry
        lax.fori_loop(0, n_chunks, body, 0)

    @pl.when(has_tie)
    def _():
        tri = tri_ref[...]
        quota = jnp.where(cnt_lo > need, need - cnt_hi, n_sel).astype(F32)

        def body(c, seen):
            off = pl.multiple_of(c * tk, tk)
            kc = key_ref[pl.ds(off, tk), :]
            eq = kc == lo
            eqf = jnp.where(eq, 1.0, 0.0)
            rank = _dot(tri, eqf.astype(BF16)) + seen
            sel = (kc > lo) | (eq & (rank <= quota))
            madd_ref[pl.ds(off, tk), :] = jnp.where(sel, 0.0, MASK_NEG)
            return seen + jnp.sum(eqf, axis=0, keepdims=True)
        lax.fori_loop(0, n_chunks, body, jnp.zeros((1, tq), F32))

    m_ref[...] = jnp.full(m_ref.shape, MASK_NEG, F32)
    acc_ref[...] = jnp.zeros(acc_ref.shape, F32)
    group = N_HEADS // N_KV_HEADS
    ones_rows = jnp.ones((2 * SUBLANES, tk), BF16)

    def masked_logits(c, h):
        off = pl.multiple_of(c * tk, tk)
        near = jnp.minimum(qt - c, n_near) * N_HEADS
        return (_dot(k_ref[h // group, pl.ds(off, tk), :], qt_ref[h])
                + madd_ref[pl.ds(off, tk), :] + bias_ref[near + h])

    def logits(c, x_ref):
        for h in range(N_HEADS):
            x_ref[h] = masked_logits(c, h)

    def softmax_pv(c, x_ref, nxt, xn_ref):
        off = pl.multiple_of(c * tk, tk)
        vt_ext = [jnp.concatenate([vt_ref[kvh, :, pl.ds(off, tk)], ones_rows], axis=0)
                  for kvh in range(N_KV_HEADS)]

        def pv(h, alpha):
            prod = _dot(vt_ext[h // group], p_ref[h])[:HEAD_DIM + SUBLANES]
            acc_ref[h] = (alpha[None] * split(acc_ref[h])).reshape(HEAD_DIM + SUBLANES, tq) + prod

        prev_alpha = None
        for h in range(N_HEADS):
            xn_ref[h] = masked_logits(nxt, h)
            mx = jnp.max(split(x_ref[h]), axis=0)
            for shift in (4, 2, 1):
                mx = jnp.maximum(mx, pltpu.roll(mx, shift, 0))
            m_old = m_ref[h]
            m_new = jnp.maximum(m_old, mx)
            alpha = jnp.exp(m_old - m_new)
            m_ref[h] = m_new
            p = jnp.exp(split(x_ref[h]) - m_new[None])
            p_ref[h] = p.reshape(tk, tq).astype(BF16)
            if h > 0:
                pv(h - 1, prev_alpha)
            prev_alpha = alpha
        pv(N_HEADS - 1, prev_alpha)

    logits(0, xa_ref)

    def attend(c, carry):
        nxt = jnp.minimum(c + 1, qt)

        @pl.when(c % 2 == 0)
        def _():
            softmax_pv(c, xa_ref, nxt, xb_ref)

        @pl.when(c % 2 == 1)
        def _():
            softmax_pv(c, xb_ref, nxt, xa_ref)
        return carry

    lax.fori_loop(0, n_chunks, attend, 0)

    for h in range(N_HEADS):
        ot_ref[h * HEAD_DIM:(h + 1) * HEAD_DIM, :] = (
            acc_ref[h, 0:HEAD_DIM, :] / acc_ref[h, HEAD_DIM:HEAD_DIM + 1, :])
    o_ref[...] = ot_ref[...].T.astype(o_ref.dtype)


def _attention(qt, qit, wit, k, vt, ki, rel_bias, tq):
    bsz, _, _, seq = qt.shape
    n_sel = min(TOPK_MAX, seq // 4)
    far = _far_distance()
    n_near = min(seq // tq, (far - 1 + tq - 1) // tq + 1)
    i = np.arange(tq)[:, None]
    j = np.arange(tq)[None, :]
    bucket = np.stack([_t5_bucket_np(dlt * tq + j - i) for dlt in range(n_near)]
                      + [np.full((tq, tq), N_BUCKETS - 1, np.int32)])
    tri = (j <= i).astype(np.float32)

    kernel = functools.partial(_attn_kernel, tq=tq, n_sel=n_sel, n_near=n_near)
    qblk = lambda n, w: pl.BlockSpec((None, n, w, tq), lambda b, t: (b, 0, 0, t))
    return pl.pallas_call(
        kernel,
        grid=(bsz, seq // tq),
        in_specs=[qblk(N_HEADS, HEAD_DIM), qblk(IDX_HEADS, IDX_DIM),
                  pl.BlockSpec((None, IDX_HEADS, tq), lambda b, t: (b, 0, t)),
                  pl.BlockSpec((None, N_KV_HEADS, seq, HEAD_DIM), lambda b, t: (b, 0, 0, 0)),
                  pl.BlockSpec((None, N_KV_HEADS, HEAD_DIM, seq), lambda b, t: (b, 0, 0, 0)),
                  pl.BlockSpec((None, seq, IDX_DIM), lambda b, t: (b, 0, 0)),
                  _resident((n_near + 1, tq, tq)),
                  pl.BlockSpec(memory_space=pltpu.SMEM),
                  _resident((tq, tq))],
        out_specs=pl.BlockSpec((None, tq, N_HEADS * HEAD_DIM), lambda b, t: (b, t, 0)),
        out_shape=jax.ShapeDtypeStruct((bsz, seq, N_HEADS * HEAD_DIM), BF16),
        scratch_shapes=[pltpu.VMEM((seq, tq), jnp.int32),
                        pltpu.VMEM((seq, tq), F32),
                        pltpu.VMEM(((n_near + 1) * N_HEADS, tq, tq), F32),
                        pltpu.VMEM((N_HEADS, SUBLANES, tq), F32),
                        pltpu.VMEM((N_HEADS, HEAD_DIM + SUBLANES, tq), F32),
                        pltpu.VMEM((N_HEADS * HEAD_DIM, tq), F32),
                        pltpu.VMEM((N_HEADS, tq, tq), F32),
                        pltpu.VMEM((N_HEADS, tq, tq), F32),
                        pltpu.VMEM((N_HEADS, tq, tq), BF16),
                        pltpu.VMEM((4, SUBLANES, tq), jnp.int32)],
        compiler_params=_cparams(2),
        name="attention",
    )(qt, qit, wit, k, vt, ki, jnp.asarray(bucket), rel_bias, jnp.asarray(tri, BF16))


N_SEG = SUBLANES


def _seg_pitch(seg_len):
    p = seg_len // SUBLANES + 1
    if p % 2 == 0:
        p += 1
    return p * SUBLANES


def _rglru_kernel(xr_ref, gy_ref, cw_ref, cb_ref, wrg_ref, brg_ref, wig_ref, big_ref, lam_ref,
                  o_ref, xp_ref, a_ref, u_ref, *, seq, n_grp):
    seg = seq // N_SEG
    pitch = _seg_pitch(seg)
    pad = SUBLANES

    xp_ref[0:pad, :] = jnp.zeros((pad, LANES), F32)
    for g in range(n_grp):
        ls = slice(g * LANES, (g + 1) * LANES)
        xp_ref[pad:pad + seq, :] = xr_ref[:, ls].astype(F32)
        cw = cw_ref[:, ls]
        xc = cb_ref[:, ls] + cw[0:1] * xp_ref[pl.ds(pad - 3, seq), :]
        for kk in range(1, RNN_CONV):
            xc = xc + cw[kk:kk + 1] * xp_ref[pl.ds(pad - 3 + kk, seq), :]
        xb = xc.astype(BF16)
        r = jax.nn.sigmoid(_dot(xb, wrg_ref[g]) + brg_ref[:, ls])
        ig = jax.nn.sigmoid(_dot(xb, wig_ref[g]) + big_ref[:, ls])
        z = -lam_ref[:, ls]
        softplus = jnp.maximum(z, 0.0) + jnp.log1p(jnp.exp(-jnp.abs(z)))
        log_a = (-LRU_C * softplus) * r
        a = jnp.exp(log_a)
        gain2 = -jnp.tanh(log_a) * (1.0 + a * a)
        u = xc * ig * jnp.where(gain2 > 0.0, gain2 * lax.rsqrt(gain2), 0.0)
        for j in range(N_SEG):
            a_ref[g, j * pitch:j * pitch + seg, :] = a[j * seg:(j + 1) * seg]
            u_ref[g, j * pitch:j * pitch + seg, :] = u[j * seg:(j + 1) * seg]

    def step(t, st):
        new = []
        for g in range(n_grp):
            h, p = st[g]
            a_t = a_ref[g, pl.ds(t, N_SEG, stride=pitch), :]
            u_t = u_ref[g, pl.ds(t, N_SEG, stride=pitch), :]
            h = a_t * h + u_t
            p = p * a_t
            u_ref[g, pl.ds(t, N_SEG, stride=pitch), :] = h
            a_ref[g, pl.ds(t, N_SEG, stride=pitch), :] = p
            new.append((h, p))
        return tuple(new)

    init = tuple((jnp.zeros((N_SEG, LANES), F32), jnp.ones((N_SEG, LANES), F32)) for _ in range(n_grp))
    final = lax.fori_loop(0, seg, step, init)

    for g in range(n_grp):
        ls = slice(g * LANES, (g + 1) * LANES)
        h_end, p_end = final[g]
        carry = jnp.zeros((1, LANES), F32)
        for j in range(N_SEG):
            rows = slice(j * pitch, j * pitch + seg)
            hj = u_ref[g, rows, :] + a_ref[g, rows, :] * carry
            o_ref[j * seg:(j + 1) * seg, ls] = (hj * gy_ref[j * seg:(j + 1) * seg, ls].astype(F32)).astype(o_ref.dtype)
            carry = h_end[j:j + 1] + p_end[j:j + 1] * carry


def _rglru(xr, gy, cw, cb, wrg, brg, wig, big, lam, cblk):
    bsz, seq, d = xr.shape
    n_grp = cblk // LANES
    pitch = _seg_pitch(seq // N_SEG)
    kernel = functools.partial(_rglru_kernel, seq=seq, n_grp=n_grp)
    act = pl.BlockSpec((None, seq, cblk), lambda b, c: (b, 0, c))
    vec = lambda r: pl.BlockSpec((r, cblk), lambda b, c: (0, c))
    gate_w = pl.BlockSpec((n_grp, LANES, LANES), lambda b, c: (c, 0, 0))
    return pl.pallas_call(
        kernel,
        grid=(bsz, d // cblk),
        in_specs=[act, act, vec(RNN_CONV), vec(1), gate_w, vec(1), gate_w, vec(1), vec(1)],
        out_specs=act,
        out_shape=jax.ShapeDtypeStruct((bsz, seq, d), BF16),
        scratch_shapes=[pltpu.VMEM((seq + SUBLANES, LANES), F32),
                        pltpu.VMEM((n_grp, N_SEG * pitch, LANES), F32),
                        pltpu.VMEM((n_grp, N_SEG * pitch, LANES), F32)],
        compiler_params=_cparams(2),
        name="rglru",
    )(xr, gy, cw, cb, wrg, brg, wig, big, lam)


def _merge_ffn_kernel(x_ref, attn_ref, rnn_ref, sga_ref, sgb_ref, ga1_ref, woa_ref, wor_ref, wout_ref,
                      sc_ref, sh_ref, ga2_ref, g_ref, gf_ref, wv_ref, wg_ref, cwv_ref, cwg_ref,
                      cbv_ref, cbg_ref, wd_ref, o_ref, buf_ref, carry_ref, act_ref, *, tile, fchunk, n_fchunks):
    s = pl.program_id(1)
    pad = SUBLANES

    @pl.when(s == 0)
    def _():
        carry_ref[...] = jnp.zeros(carry_ref.shape, F32)

    merged = (sga_ref[...].astype(F32) * _dot(attn_ref[...], woa_ref[...])
              + sgb_ref[...].astype(F32) * _dot(rnn_ref[...], wor_ref[...]))
    hres = x_ref[...] + ga1_ref[...] * _dot(merged.astype(BF16), wout_ref[...])
    xn = _modulated_norm(hres, g_ref[...], sc_ref[...], sh_ref[...]).astype(BF16)

    def conv(up, idx, cw, cb):
        buf_ref[0:pad, :] = carry_ref[idx]
        buf_ref[pad:pad + tile, :] = up
        carry_ref[idx] = up[tile - pad:tile]
        y = cb + cw[FFN_CONV - 1:FFN_CONV] * up
        for kk in range(FFN_CONV - 1):
            y = y + cw[kk:kk + 1] * buf_ref[pl.ds(pad - (FFN_CONV - 1) + kk, tile), :]
        return y

    for c in range(n_fchunks):
        cs = slice(c * fchunk, (c + 1) * fchunk)
        val = conv(_dot(xn, wv_ref[:, cs]), 2 * c, cwv_ref[:, cs], cbv_ref[:, cs])
        gte = conv(_dot(xn, wg_ref[:, cs]), 2 * c + 1, cwg_ref[:, cs], cbg_ref[:, cs])
        act_ref[:, cs] = ((gte * jax.nn.sigmoid(gte)) * val).astype(BF16)

    h2 = hres + ga2_ref[...] * _dot(act_ref[...], wd_ref[...])
    ms = jnp.mean(h2 * h2, axis=-1, keepdims=True)
    o_ref[...] = h2 * lax.rsqrt(ms + EPS) * gf_ref[...]


def _merge_ffn(x, attn, rnn, sga, sgb, ga1, woa, wor, wout, sc, sh, ga2, g, gf,
               wv, wg, cwv, cwg, cbv, cbg, wd, tile, fchunk):
    bsz, seq, d = x.shape
    dff = wv.shape[1]
    n_fchunks = dff // fchunk
    kernel = functools.partial(_merge_ffn_kernel, tile=tile, fchunk=fchunk, n_fchunks=n_fchunks)
    row = lambda w: pl.BlockSpec((None, tile, w), lambda b, s: (b, s, 0))
    per_batch = pl.BlockSpec((None, 1, d), lambda b, s: (b, 0, 0))
    merge_w = [woa, wor, wout]
    ffn_w = [g, gf, wv, wg, cwv, cwg, cbv, cbg, wd]
    return pl.pallas_call(
        kernel,
        grid=(bsz, seq // tile),
        in_specs=([row(d), row(attn.shape[-1]), row(d), row(d), row(d), per_batch]
                  + [_resident(a.shape) for a in merge_w]
                  + [per_batch, per_batch, per_batch]
                  + [_resident(a.shape) for a in ffn_w]),
        out_specs=row(d),
        out_shape=jax.ShapeDtypeStruct((bsz, seq, d), F32),
        scratch_shapes=[pltpu.VMEM((tile + SUBLANES, fchunk), F32),
                        pltpu.VMEM((2 * n_fchunks, SUBLANES, fchunk), F32),
                        pltpu.VMEM((tile, dff), BF16)],
        compiler_params=_cparams(2),
        name="merge_ffn",
    )(x, attn, rnn, sga, sgb, ga1, *merge_w, sc, sh, ga2, *ffn_w)


def _pick(seq, pref):
    t = min(seq, pref)
    assert seq % t == 0
    return t


def kernel(x, c, w_ada, b_ada, g_mix, w_in, b_in, rel_bias, conv_rnn_w, conv_rnn_b, w_rg, b_rg, w_ig, b_ig, lru_lambda, w_o_attn, w_o_rnn, w_out, g_ffn, w_up, conv_ffn_w, conv_ffn_b, w_down, g_final):
    bsz, seq, d = x.shape
    depth = w_ada.shape[0]
    assert depth == 1 and d == N_RNN_BLOCKS * LANES
    dff = w_down.shape[1]
    row_tile = _pick(seq, 512)
    tq = _pick(seq, 256)

    widths = (N_HEADS * HEAD_DIM, N_KV_HEADS * HEAD_DIM, N_KV_HEADS * HEAD_DIM, IDX_HEADS * IDX_DIM,
              IDX_DIM, IDX_HEADS, d, d, d, d)
    cuts = np.cumsum((0,) + widths)
    col = lambda a, i0, i1: a[..., cuts[i0]:cuts[i1]]

    h = x
    for l in range(depth):
        mod = _adaln(c, w_ada[l], b_ada[l])
        sh1, sc1, ga1, sh2, sc2, ga2 = [m[:, None, :] for m in jnp.split(mod, 6, axis=-1)]

        w, bias = w_in[l], b_in[l]
        t_rows = [col(w, 0, 1), col(w, 3, 4), col(w, 2, 3), col(w, 5, 6)]
        t_bias = [col(bias, 0, 1), col(bias, 3, 4), col(bias, 2, 3), col(bias, 5, 6)]
        n_t = sum(r.shape[1] for r in t_rows)
        t_pad = (-n_t) % (2 * SUBLANES)
        wt = jnp.pad(jnp.concatenate(t_rows, axis=1).T, ((0, t_pad), (0, 0))).astype(BF16)
        bt = jnp.pad(jnp.concatenate(t_bias), (0, t_pad)).reshape(-1, 1)
        k_pad = (-(widths[1] + widths[4])) % LANES
        wk = jnp.pad(jnp.concatenate([col(w, 1, 2), col(w, 4, 5)], axis=1), ((0, 0), (0, k_pad))).astype(BF16)
        bk = jnp.pad(jnp.concatenate([col(bias, 1, 2), col(bias, 4, 5)]), (0, k_pad)).reshape(1, -1)
        weights = [wt, bt, wk, bk]
        for i in range(6, 10):
            weights += [col(w, i, i + 1).astype(BF16), col(bias, i, i + 1).reshape(1, -1)]
        qt, qit, vt, wit, k, ki, xr, gy, sga, sgb = _in_proj(h, sc1, sh1, g_mix[l].reshape(1, d), weights, row_tile)

        attn = _attention(qt, qit, wit, k, vt, ki, rel_bias, tq)
        rnn = _rglru(xr, gy, conv_rnn_w[l], conv_rnn_b[l].reshape(1, d), w_rg[l].astype(BF16),
                     b_rg[l].reshape(1, d), w_ig[l].astype(BF16), b_ig[l].reshape(1, d),
                     lru_lambda[l].reshape(1, d), cblk=min(d, 512))
        wu, cw, cb = w_up[l], conv_ffn_w[l], conv_ffn_b[l].reshape(1, -1)
        h = _merge_ffn(h, attn, rnn, sga, sgb, ga1, w_o_attn[l].astype(BF16), w_o_rnn[l].astype(BF16),
                       w_out[l].astype(BF16), sc2, sh2, ga2, g_ffn[l].reshape(1, d), g_final.reshape(1, d),
                       wu[:, :dff].astype(BF16), wu[:, dff:].astype(BF16), cw[:, :dff], cw[:, dff:],
                       cb[:, :dff], cb[:, dff:], w_down[l].astype(BF16), row_tile, fchunk=256)
    return h
```

```python
import functools
import math

import numpy as np
import jax
import jax.numpy as jnp
from jax import lax
from jax.experimental import pallas as pl
from jax.experimental.pallas import tpu as pltpu

N_HEADS = 8
HEAD_DIM = 64
N_KV_HEADS = 2
IDX_HEADS = 8
IDX_DIM = 64
TOPK_MAX = 256
N_BUCKETS = 32
MAX_DISTANCE = 128
N_RNN_BLOCKS = 8
RNN_CONV = 4
LRU_C = 8.0
FFN_CONV = 3
EPS = 1e-6

LANES = 128
SUBLANES = 8
VMEM_LIMIT = 56 * 1024 * 1024

MASK_NEG = -(2.0 ** 100)
KEY_NEG_INF = -2139095041

BF16 = jnp.bfloat16
F32 = jnp.float32


def _cparams(n_grid):
    return pltpu.CompilerParams(
        dimension_semantics=("arbitrary",) * n_grid, vmem_limit_bytes=VMEM_LIMIT)


def _resident(shape):
    nd = len(shape)
    return pl.BlockSpec(shape, lambda *_: (0,) * nd, pipeline_mode=pl.Buffered(1))


def _dot(a, b):
    return jnp.dot(a, b, preferred_element_type=F32)


def _dot_nt(a, b):
    return lax.dot_general(a, b, (((1,), (1,)), ((), ())), preferred_element_type=F32)


def _adaln_kernel(c_ref, w_ref, b_ref, o_ref):
    c = c_ref[...]
    c_act = (c * jax.nn.sigmoid(c)).astype(BF16)
    o_ref[...] = _dot(c_act, w_ref[...].astype(BF16)) + b_ref[...]


def _adaln(c, w, b):
    bsz, d = c.shape
    n = w.shape[1]
    return pl.pallas_call(
        _adaln_kernel,
        grid=(n // d,),
        in_specs=[pl.BlockSpec((bsz, d), lambda j: (0, 0)),
                  pl.BlockSpec((d, d), lambda j: (0, j)),
                  pl.BlockSpec((1, d), lambda j: (0, j))],
        out_specs=pl.BlockSpec((bsz, d), lambda j: (0, j)),
        out_shape=jax.ShapeDtypeStruct((bsz, n), F32),
        compiler_params=_cparams(1),
        name="adaln",
    )(c, w, b.reshape(1, n))


def _modulated_norm(x, g, sc, sh):
    ms = jnp.mean(x * x, axis=-1, keepdims=True)
    return (x * lax.rsqrt(ms + EPS) * g) * (1.0 + sc) + sh


def _in_proj_kernel(x_ref, sc_ref, sh_ref, g_ref, wt_ref, bt_ref, wk_ref, bk_ref,
                    wr_ref, br_ref, wy_ref, by_ref, wga_ref, bga_ref, wgb_ref, bgb_ref,
                    qt_ref, qit_ref, vt_ref, wit_ref, k_ref, ki_ref, xr_ref, gy_ref, sga_ref, sgb_ref):
    xn = _modulated_norm(x_ref[...], g_ref[...], sc_ref[...], sh_ref[...]).astype(BF16)

    res = _dot_nt(wt_ref[...], xn) + bt_ref[...]
    r0 = 0
    for h in range(N_HEADS):
        qt_ref[h] = (res[r0 + h * HEAD_DIM:r0 + (h + 1) * HEAD_DIM] * (HEAD_DIM ** -0.5)).astype(BF16)
    r0 += N_HEADS * HEAD_DIM
    for h in range(IDX_HEADS):
        qit_ref[h] = (res[r0 + h * IDX_DIM:r0 + (h + 1) * IDX_DIM] * (IDX_DIM ** -0.5)).astype(BF16)
    r0 += IDX_HEADS * IDX_DIM
    for j in range(N_KV_HEADS):
        vt_ref[j] = res[r0 + j * HEAD_DIM:r0 + (j + 1) * HEAD_DIM].astype(BF16)
    r0 += N_KV_HEADS * HEAD_DIM
    wit_ref[...] = res[r0:r0 + IDX_HEADS] * (IDX_HEADS ** -0.5)

    res = _dot(xn, wk_ref[...]) + bk_ref[...]
    for j in range(N_KV_HEADS):
        k_ref[j] = res[:, j * HEAD_DIM:(j + 1) * HEAD_DIM].astype(BF16)
    ki0 = N_KV_HEADS * HEAD_DIM
    ki_ref[...] = res[:, ki0:ki0 + IDX_DIM].astype(BF16)

    xr_ref[...] = (_dot(xn, wr_ref[...]) + br_ref[...]).astype(BF16)
    gy_ref[...] = jax.nn.gelu(_dot(xn, wy_ref[...]) + by_ref[...]).astype(BF16)
    sga_ref[...] = jax.nn.sigmoid(_dot(xn, wga_ref[...]) + bga_ref[...]).astype(BF16)
    sgb_ref[...] = jax.nn.sigmoid(_dot(xn, wgb_ref[...]) + bgb_ref[...]).astype(BF16)


def _in_proj(x, sc, sh, g, weights, tile):
    bsz, seq, d = x.shape
    row = lambda w: pl.BlockSpec((None, tile, w), lambda b, s: (b, s, 0))
    heads = lambda n, w: pl.BlockSpec((None, n, tile, w), lambda b, s: (b, 0, s, 0))
    heads_t = lambda n, w: pl.BlockSpec((None, n, w, tile), lambda b, s: (b, 0, 0, s))
    per_batch = pl.BlockSpec((None, 1, d), lambda b, s: (b, 0, 0))
    in_specs = [row(d), per_batch, per_batch, _resident((1, d))]
    in_specs += [_resident(w.shape) for w in weights]
    bsd = lambda w, dt: jax.ShapeDtypeStruct((bsz, seq, w), dt)
    out_shape = [
        jax.ShapeDtypeStruct((bsz, N_HEADS, HEAD_DIM, seq), BF16),
        jax.ShapeDtypeStruct((bsz, IDX_HEADS, IDX_DIM, seq), BF16),
        jax.ShapeDtypeStruct((bsz, N_KV_HEADS, HEAD_DIM, seq), BF16),
        jax.ShapeDtypeStruct((bsz, IDX_HEADS, seq), F32),
        jax.ShapeDtypeStruct((bsz, N_KV_HEADS, seq, HEAD_DIM), BF16),
        bsd(IDX_DIM, BF16),
        bsd(d, BF16), bsd(d, BF16), bsd(d, BF16), bsd(d, BF16)]
    out_specs = [heads_t(N_HEADS, HEAD_DIM), heads_t(IDX_HEADS, IDX_DIM), heads_t(N_KV_HEADS, HEAD_DIM),
                 pl.BlockSpec((None, IDX_HEADS, tile), lambda b, s: (b, 0, s)),
                 heads(N_KV_HEADS, HEAD_DIM), row(IDX_DIM),
                 row(d), row(d), row(d), row(d)]
    return pl.pallas_call(
        _in_proj_kernel,
        grid=(bsz, seq // tile),
        in_specs=in_specs, out_specs=out_specs, out_shape=out_shape,
        compiler_params=_cparams(2),
        name="in_proj",
    )(x, sc, sh, g, *weights)


def _t5_bucket_np(rel):
    max_exact = N_BUCKETS // 2
    n = np.maximum(rel, 0)
    nf = np.maximum(n, 1).astype(np.float64)
    large = max_exact + (np.log(nf / max_exact) / math.log(MAX_DISTANCE / max_exact)
                         * (N_BUCKETS - max_exact)).astype(np.int32)
    large = np.minimum(large, N_BUCKETS - 1)
    return np.where(n < max_exact, n, large).astype(np.int32)


def _far_distance():
    d = np.arange(0, 4 * MAX_DISTANCE)
    b = _t5_bucket_np(d)
    assert b[-1] == N_BUCKETS - 1
    return int(np.max(np.nonzero(b != N_BUCKETS - 1)[0])) + 1


def _sort_key(s):
    bits = lax.bitcast_convert_type(s, jnp.int32)
    return bits ^ ((bits >> 31) & jnp.int32(0x7FFFFFFF))


def _key_to_float(k):
    return lax.bitcast_convert_type(k ^ ((k >> 31) & jnp.int32(0x7FFFFFFF)), F32)


VALUE_STEPS = 12
SEARCH_UNROLL = 4
SCORE_ROWS = 64


def _attn_kernel(qt_ref, qit_ref, wit_ref, k_ref, vt_ref, ki_ref, bucket_ref, relb_ref, tri_ref,
                 o_ref, key_ref, madd_ref, bias_ref, m_ref, acc_ref, ot_ref, da_ref, db_ref, xa_ref, xb_ref, p_ref,
                 stat_ref,
                 *, tq, n_sel, n_near):
    b = pl.program_id(0)
    qt = pl.program_id(1)
    n_chunks = qt + 1
    tk = tq
    int_max = jnp.iinfo(jnp.int32).max

    @pl.when((b == 0) & (qt == 0))
    def _():
        for dlt in range(n_near + 1):
            bucket = bucket_ref[dlt]
            for h in range(N_HEADS):
                tile = jnp.full((tk, tq), relb_ref[N_BUCKETS - 1, h], F32)
                for bk in range(N_BUCKETS - 1):
                    tile = jnp.where(bucket == bk, relb_ref[bk, h], tile)
                bias_ref[dlt * N_HEADS + h] = tile

    t_row = qt * tq + lax.broadcasted_iota(jnp.int32, (1, tq), 1)

    def split(v):
        return v.reshape(v.shape[0] // SUBLANES, SUBLANES, v.shape[1])

    n_rb = tk // SCORE_ROWS
    heads_per_rb = IDX_HEADS // n_rb
    assert n_rb * SCORE_ROWS == tk and heads_per_rb * n_rb == IDX_HEADS
    key_pos = lax.broadcasted_iota(jnp.int32, (SCORE_ROWS, tq), 0)
    qry_pos = lax.broadcasted_iota(jnp.int32, (SCORE_ROWS, tq), 1)

    def idx_dots(c, d_ref, heads):
        kic = ki_ref[pl.ds(pl.multiple_of(c * tk, tk), tk), :]
        for h in heads:
            d_ref[h] = _dot(kic, qit_ref[h])

    def score_chunk(c, d_ref, nxt, dn_ref):
        off = pl.multiple_of(c * tk, tk)
        kmin, kmax, npos, nzero = stat_ref[0], stat_ref[1], stat_ref[2], stat_ref[3]
        for rb in range(n_rb):
            idx_dots(nxt, dn_ref, range(rb * heads_per_rb, (rb + 1) * heads_per_rb))
            rows = slice(rb * SCORE_ROWS, (rb + 1) * SCORE_ROWS)
            s = None
            for h in range(IDX_HEADS):
                term = wit_ref[h:h + 1, :] * jnp.maximum(d_ref[h, rows, :], 0.0)
                s = term if s is None else s + term
            valid = (c < qt) | (key_pos + rb * SCORE_ROWS <= qry_pos)
            s = jnp.where(valid, s, -jnp.inf)
            is_zero = s == 0.0
            key = jnp.where(is_zero, 0, _sort_key(s))
            key_ref[pl.ds(pl.multiple_of(off + rb * SCORE_ROWS, SCORE_ROWS), SCORE_ROWS), :] = key
            kmin = jnp.minimum(kmin, jnp.min(split(jnp.where(valid, key, int_max)), axis=0))
            kmax = jnp.maximum(kmax, jnp.max(split(key), axis=0))
            npos = npos + jnp.sum(split(jnp.where(s > 0.0, 1, 0)), axis=0)
            nzero = nzero + jnp.sum(split(jnp.where(is_zero, 1, 0)), axis=0)
        stat_ref[0], stat_ref[1], stat_ref[2], stat_ref[3] = kmin, kmax, npos, nzero

    stat_ref[0] = jnp.full((SUBLANES, tq), int_max, jnp.int32)
    stat_ref[1] = jnp.full((SUBLANES, tq), KEY_NEG_INF, jnp.int32)
    stat_ref[2] = jnp.zeros((SUBLANES, tq), jnp.int32)
    stat_ref[3] = jnp.zeros((SUBLANES, tq), jnp.int32)
    idx_dots(0, da_ref, range(IDX_HEADS))

    def score_step(c, carry):
        nxt = jnp.minimum(c + 1, qt)

        @pl.when(c % 2 == 0)
        def _():
            score_chunk(c, da_ref, nxt, db_ref)

        @pl.when(c % 2 == 1)
        def _():
            score_chunk(c, db_ref, nxt, da_ref)
        return carry

    lax.fori_loop(0, n_chunks, score_step, 0)
    kmin, kmax, npos, nzero = stat_ref[0], stat_ref[1], stat_ref[2], stat_ref[3]
    kmin = jnp.min(kmin, axis=0, keepdims=True)
    kmax = jnp.max(kmax, axis=0, keepdims=True)
    npos = jnp.sum(npos, axis=0, keepdims=True)
    nneg0 = npos + jnp.sum(nzero, axis=0, keepdims=True)

    need = jnp.minimum(t_row + 1, n_sel)

    def count_ge(mid):
        mid8 = jnp.broadcast_to(mid, (SUBLANES, tq))

        def body(c, acc):
            off = pl.multiple_of(c * tk, tk)
            return acc + jnp.sum(jnp.where(split(key_ref[pl.ds(off, tk), :]) >= mid8[None], 1, 0), axis=0)
        acc = lax.fori_loop(0, n_chunks, body, jnp.zeros((SUBLANES, tq), jnp.int32))
        return jnp.sum(acc, axis=0, keepdims=True)

    def search_cond(st):
        return jnp.min(st[-1]) == 0

    def search_steps(st):
        for _ in range(SEARCH_UNROLL):
            st = search_step(st)
        return st

    def search_step(st):
        it, lo, hi, cnt_lo, cnt_hi, done = st
        half = (lo >> 1) + (hi >> 1) + (lo & hi & 1)
        mid_f = 0.5 * _key_to_float(lo) + 0.5 * _key_to_float(hi)
        by_value = jnp.minimum(jnp.maximum(_sort_key(mid_f), lo + 1), hi - 1)
        mid = jnp.where(it < VALUE_STEPS, by_value, half)
        cnt = count_ge(mid)
        live = done == 0
        up = live & (cnt >= need)
        dn = live & (cnt < need)
        lo = jnp.where(up, mid, lo)
        cnt_lo = jnp.where(up, cnt, cnt_lo)
        hi = jnp.where(dn, mid, hi)
        cnt_hi = jnp.where(dn, cnt, cnt_hi)
        done = jnp.where((cnt_lo == need) | (hi == lo + 1), 1, done)
        return it + 1, lo, hi, cnt_lo, cnt_hi, done

    above = need <= npos
    below = need > nneg0
    lo0 = jnp.where(above, 1, jnp.where(below, kmin, 0))
    cnt_lo0 = jnp.where(above, npos, jnp.where(below, t_row + 1, nneg0))
    hi0 = jnp.where(above, jnp.minimum(kmax, int_max - 1) + 1, jnp.where(below, 0, 1))
    cnt_hi0 = jnp.where(above, 0, jnp.where(below, nneg0, npos))
    done0 = jnp.where((cnt_lo0 == need) | (hi0 == lo0 + 1), 1, 0)
    _, lo, hi, cnt_lo, cnt_hi, _ = lax.while_loop(
        search_cond, search_steps, (jnp.int32(0), lo0, hi0, cnt_lo0, cnt_hi0, done0))

    has_tie = jnp.max(jnp.where(cnt_lo > need, 1, 0)) > 0

    @pl.when(jnp.logical_not(has_tie))
    def _():
        def body(c, carry):
            off = pl.multiple_of(c * tk, tk)
            madd_ref[pl.ds(off, tk), :] = jnp.where(key_ref[pl.ds(off, tk), :] >= lo, 0.0, MASK_NEG)
            return carry
        lax.fori_loop(0, n_chunks, body, 0)

    @pl.when(has_tie)
    def _():
        tri = tri_ref[...]
        quota = jnp.where(cnt_lo > need, need - cnt_hi, n_sel).astype(F32)

        def body(c, seen):
            off = pl.multiple_of(c * tk, tk)
            kc = key_ref[pl.ds(off, tk), :]
            eq = kc == lo
            eqf = jnp.where(eq, 1.0, 0.0)
            rank = _dot(tri, eqf.astype(BF16)) + seen
            sel = (kc > lo) | (eq & (rank <= quota))
            madd_ref[pl.ds(off, tk), :] = jnp.where(sel, 0.0, MASK_NEG)
            return seen + jnp.sum(eqf, axis=0, keepdims=True)
        lax.fori_loop(0, n_chunks, body, jnp.zeros((1, tq), F32))

    m_ref[...] = jnp.full(m_ref.shape, MASK_NEG, F32)
    acc_ref[...] = jnp.zeros(acc_ref.shape, F32)
    group = N_HEADS // N_KV_HEADS
    ones_rows = jnp.ones((2 * SUBLANES, tk), BF16)

    def masked_logits(c, h):
        off = pl.multiple_of(c * tk, tk)
        near = jnp.minimum(qt - c, n_near) * N_HEADS
        return (_dot(k_ref[h // group, pl.ds(off, tk), :], qt_ref[h])
                + madd_ref[pl.ds(off, tk), :] + bias_ref[near + h]).astype(BF16)

    def logits(c, x_ref):
        for h in range(N_HEADS):
            x_ref[h] = masked_logits(c, h)

    def softmax_pv(c, x_ref, nxt, xn_ref):
        off = pl.multiple_of(c * tk, tk)
        vt_ext = [jnp.concatenate([vt_ref[kvh, :, pl.ds(off, tk)], ones_rows], axis=0)
                  for kvh in range(N_KV_HEADS)]

        def pv(h, alpha):
            prod = _dot(vt_ext[h // group], p_ref[h])[:HEAD_DIM + SUBLANES]
            acc_ref[h] = (alpha[None] * split(acc_ref[h])).reshape(HEAD_DIM + SUBLANES, tq) + prod

        prev_alpha = None
        for h in range(N_HEADS):
            xn_ref[h] = masked_logits(nxt, h)
            packed = 2 * SUBLANES
            mx = jnp.max(x_ref[h].reshape(tk // packed, packed, tq), axis=0).astype(F32)
            mx = jnp.maximum(mx[:SUBLANES], mx[SUBLANES:])
            for shift in (4, 2, 1):
                mx = jnp.maximum(mx, pltpu.roll(mx, shift, 0))
            m_old = m_ref[h]
            m_new = jnp.maximum(m_old, mx)
            alpha = jnp.exp(m_old - m_new)
            m_ref[h] = m_new
            p_ref[h] = jnp.exp(x_ref[h] - m_new[0:1].astype(BF16))
            if h > 0:
                pv(h - 1, prev_alpha)
            prev_alpha = alpha
        pv(N_HEADS - 1, prev_alpha)

    logits(0, xa_ref)

    def attend(c, carry):
        nxt = jnp.minimum(c + 1, qt)

        @pl.when(c % 2 == 0)
        def _():
            softmax_pv(c, xa_ref, nxt, xb_ref)

        @pl.when(c % 2 == 1)
        def _():
            softmax_pv(c, xb_ref, nxt, xa_ref)
        return carry

    lax.fori_loop(0, n_chunks, attend, 0)

    for h in range(N_HEADS):
        ot_ref[h * HEAD_DIM:(h + 1) * HEAD_DIM, :] = (
            acc_ref[h, 0:HEAD_DIM, :] / acc_ref[h, HEAD_DIM:HEAD_DIM + 1, :])
    o_ref[...] = ot_ref[...].T.astype(o_ref.dtype)


def _attention(qt, qit, wit, k, vt, ki, rel_bias, tq):
    bsz, _, _, seq = qt.shape
    n_sel = min(TOPK_MAX, seq // 4)
    far = _far_distance()
    n_near = min(seq // tq, (far - 1 + tq - 1) // tq + 1)
    i = np.arange(tq)[:, None]
    j = np.arange(tq)[None, :]
    bucket = np.stack([_t5_bucket_np(dlt * tq + j - i) for dlt in range(n_near)]
                      + [np.full((tq, tq), N_BUCKETS - 1, np.int32)])
    tri = (j <= i).astype(np.float32)

    kernel = functools.partial(_attn_kernel, tq=tq, n_sel=n_sel, n_near=n_near)
    qblk = lambda n, w: pl.BlockSpec((None, n, w, tq), lambda b, t: (b, 0, 0, t))
    return pl.pallas_call(
        kernel,
        grid=(bsz, seq // tq),
        in_specs=[qblk(N_HEADS, HEAD_DIM), qblk(IDX_HEADS, IDX_DIM),
                  pl.BlockSpec((None, IDX_HEADS, tq), lambda b, t: (b, 0, t)),
                  pl.BlockSpec((None, N_KV_HEADS, seq, HEAD_DIM), lambda b, t: (b, 0, 0, 0)),
                  pl.BlockSpec((None, N_KV_HEADS, HEAD_DIM, seq), lambda b, t: (b, 0, 0, 0)),
                  pl.BlockSpec((None, seq, IDX_DIM), lambda b, t: (b, 0, 0)),
                  _resident((n_near + 1, tq, tq)),
                  pl.BlockSpec(memory_space=pltpu.SMEM),
                  _resident((tq, tq))],
        out_specs=pl.BlockSpec((None, tq, N_HEADS * HEAD_DIM), lambda b, t: (b, t, 0)),
        out_shape=jax.ShapeDtypeStruct((bsz, seq, N_HEADS * HEAD_DIM), BF16),
        scratch_shapes=[pltpu.VMEM((seq, tq), jnp.int32),
                        pltpu.VMEM((seq, tq), F32),
                        pltpu.VMEM(((n_near + 1) * N_HEADS, tq, tq), F32),
                        pltpu.VMEM((N_HEADS, SUBLANES, tq), F32),
                        pltpu.VMEM((N_HEADS, HEAD_DIM + SUBLANES, tq), F32),
                        pltpu.VMEM((N_HEADS * HEAD_DIM, tq), F32),
                        pltpu.VMEM((IDX_HEADS, tq, tq), F32),
                        pltpu.VMEM((IDX_HEADS, tq, tq), F32),
                        pltpu.VMEM((N_HEADS, tq, tq), BF16),
                        pltpu.VMEM((N_HEADS, tq, tq), BF16),
                        pltpu.VMEM((N_HEADS, tq, tq), BF16),
                        pltpu.VMEM((4, SUBLANES, tq), jnp.int32)],
        compiler_params=_cparams(2),
        name="attention",
    )(qt, qit, wit, k, vt, ki, jnp.asarray(bucket), rel_bias, jnp.asarray(tri, BF16))


N_SEG = SUBLANES


def _seg_pitch(seg_len):
    quarter = seg_len // 4 + 1
    if quarter % 2 == 0:
        quarter += 1
    return 4 * quarter


def _rglru_kernel(xr_ref, gy_ref, cw_ref, cb_ref, wrg_ref, brg_ref, wig_ref, big_ref, lam_ref,
                  o_ref, xp_ref, a_ref, u_ref, *, seq, n_grp):
    seg = seq // N_SEG
    pitch = _seg_pitch(seg)
    pad = SUBLANES

    xp_ref[0:pad, :] = jnp.zeros((pad, LANES), F32)
    for g in range(n_grp):
        ls = slice(g * LANES, (g + 1) * LANES)
        xp_ref[pad:pad + seq, :] = xr_ref[:, ls].astype(F32)
        cw = cw_ref[:, ls]
        xc = cb_ref[:, ls] + cw[0:1] * xp_ref[pl.ds(pad - 3, seq), :]
        for kk in range(1, RNN_CONV):
            xc = xc + cw[kk:kk + 1] * xp_ref[pl.ds(pad - 3 + kk, seq), :]
        xb = xc.astype(BF16)
        r = jax.nn.sigmoid(_dot(xb, wrg_ref[g]) + brg_ref[:, ls])
        ig = jax.nn.sigmoid(_dot(xb, wig_ref[g]) + big_ref[:, ls])
        z = -lam_ref[:, ls]
        softplus = jnp.maximum(z, 0.0) + jnp.log1p(jnp.exp(-jnp.abs(z)))
        log_a = (-LRU_C * softplus) * r
        a = jnp.exp(log_a)
        gain2 = -jnp.tanh(log_a) * (1.0 + a * a)
        u = xc * ig * jnp.where(gain2 > 0.0, gain2 * lax.rsqrt(gain2), 0.0)
        for j in range(N_SEG):
            a_ref[g, j * pitch:j * pitch + seg, :] = a[j * seg:(j + 1) * seg]
            u_ref[g, j * pitch:j * pitch + seg, :] = u[j * seg:(j + 1) * seg]

    def step(t, st):
        new = []
        for g in range(n_grp):
            h, p = st[g]
            a_t = a_ref[g, pl.ds(t, N_SEG, stride=pitch), :]
            u_t = u_ref[g, pl.ds(t, N_SEG, stride=pitch), :]
            h = a_t * h + u_t
            p = p * a_t
            u_ref[g, pl.ds(t, N_SEG, stride=pitch), :] = h
            a_ref[g, pl.ds(t, N_SEG, stride=pitch), :] = p
            new.append((h, p))
        return tuple(new)

    init = tuple((jnp.zeros((N_SEG, LANES), F32), jnp.ones((N_SEG, LANES), F32)) for _ in range(n_grp))
    final = lax.fori_loop(0, seg, step, init)

    for g in range(n_grp):
        ls = slice(g * LANES, (g + 1) * LANES)
        h_end, p_end = final[g]
        carry = jnp.zeros((1, LANES), F32)
        for j in range(N_SEG):
            rows = slice(j * pitch, j * pitch + seg)
            hj = u_ref[g, rows, :] + a_ref[g, rows, :] * carry
            o_ref[j * seg:(j + 1) * seg, ls] = (hj * gy_ref[j * seg:(j + 1) * seg, ls].astype(F32)).astype(o_ref.dtype)
            carry = h_end[j:j + 1] + p_end[j:j + 1] * carry


def _rglru(xr, gy, cw, cb, wrg, brg, wig, big, lam, cblk):
    bsz, seq, d = xr.shape
    n_grp = cblk // LANES
    pitch = _seg_pitch(seq // N_SEG)
    kernel = functools.partial(_rglru_kernel, seq=seq, n_grp=n_grp)
    act = pl.BlockSpec((None, seq, cblk), lambda b, c: (b, 0, c))
    vec = lambda r: pl.BlockSpec((r, cblk), lambda b, c: (0, c))
    gate_w = pl.BlockSpec((n_grp, LANES, LANES), lambda b, c: (c, 0, 0))
    return pl.pallas_call(
        kernel,
        grid=(bsz, d // cblk),
        in_specs=[act, act, vec(RNN_CONV), vec(1), gate_w, vec(1), gate_w, vec(1), vec(1)],
        out_specs=act,
        out_shape=jax.ShapeDtypeStruct((bsz, seq, d), BF16),
        scratch_shapes=[pltpu.VMEM((seq + SUBLANES, LANES), F32),
                        pltpu.VMEM((n_grp, N_SEG * pitch, LANES), F32),
                        pltpu.VMEM((n_grp, N_SEG * pitch, LANES), F32)],
        compiler_params=_cparams(2),
        name="rglru",
    )(xr, gy, cw, cb, wrg, brg, wig, big, lam)


def _merge_ffn_kernel(x_ref, attn_ref, rnn_ref, sga_ref, sgb_ref, ga1_ref, woa_ref, wor_ref, wout_ref,
                      sc_ref, sh_ref, ga2_ref, g_ref, gf_ref, wv_ref, wg_ref, cwv_ref, cwg_ref,
                      cbv_ref, cbg_ref, wd_ref, o_ref, buf_ref, carry_ref, act_ref, *, tile, fchunk, n_fchunks):
    s = pl.program_id(1)
    pad = SUBLANES

    @pl.when(s == 0)
    def _():
        carry_ref[...] = jnp.zeros(carry_ref.shape, F32)

    merged = (sga_ref[...].astype(F32) * _dot(attn_ref[...], woa_ref[...])
              + sgb_ref[...].astype(F32) * _dot(rnn_ref[...], wor_ref[...]))
    hres = x_ref[...] + ga1_ref[...] * _dot(merged.astype(BF16), wout_ref[...])
    xn = _modulated_norm(hres, g_ref[...], sc_ref[...], sh_ref[...]).astype(BF16)

    def conv(up, idx, cw, cb):
        buf_ref[0:pad, :] = carry_ref[idx]
        buf_ref[pad:pad + tile, :] = up
        carry_ref[idx] = up[tile - pad:tile]
        y = cb + cw[FFN_CONV - 1:FFN_CONV] * up
        for kk in range(FFN_CONV - 1):
            y = y + cw[kk:kk + 1] * buf_ref[pl.ds(pad - (FFN_CONV - 1) + kk, tile), :]
        return y

    for c in range(n_fchunks):
        cs = slice(c * fchunk, (c + 1) * fchunk)
        val = conv(_dot(xn, wv_ref[:, cs]), 2 * c, cwv_ref[:, cs], cbv_ref[:, cs])
        gte = conv(_dot(xn, wg_ref[:, cs]), 2 * c + 1, cwg_ref[:, cs], cbg_ref[:, cs])
        act_ref[:, cs] = ((gte * jax.nn.sigmoid(gte)) * val).astype(BF16)

    h2 = hres + ga2_ref[...] * _dot(act_ref[...], wd_ref[...])
    ms = jnp.mean(h2 * h2, axis=-1, keepdims=True)
    o_ref[...] = h2 * lax.rsqrt(ms + EPS) * gf_ref[...]


def _merge_ffn(x, attn, rnn, sga, sgb, ga1, woa, wor, wout, sc, sh, ga2, g, gf,
               wv, wg, cwv, cwg, cbv, cbg, wd, tile, fchunk):
    bsz, seq, d = x.shape
    dff = wv.shape[1]
    n_fchunks = dff // fchunk
    kernel = functools.partial(_merge_ffn_kernel, tile=tile, fchunk=fchunk, n_fchunks=n_fchunks)
    row = lambda w: pl.BlockSpec((None, tile, w), lambda b, s: (b, s, 0))
    per_batch = pl.BlockSpec((None, 1, d), lambda b, s: (b, 0, 0))
    merge_w = [woa, wor, wout]
    ffn_w = [g, gf, wv, wg, cwv, cwg, cbv, cbg, wd]
    return pl.pallas_call(
        kernel,
        grid=(bsz, seq // tile),
        in_specs=([row(d), row(attn.shape[-1]), row(d), row(d), row(d), per_batch]
                  + [_resident(a.shape) for a in merge_w]
                  + [per_batch, per_batch, per_batch]
                  + [_resident(a.shape) for a in ffn_w]),
        out_specs=row(d),
        out_shape=jax.ShapeDtypeStruct((bsz, seq, d), F32),
        scratch_shapes=[pltpu.VMEM((tile + SUBLANES, fchunk), F32),
                        pltpu.VMEM((2 * n_fchunks, SUBLANES, fchunk), F32),
                        pltpu.VMEM((tile, dff), BF16)],
        compiler_params=_cparams(2),
        name="merge_ffn",
    )(x, attn, rnn, sga, sgb, ga1, *merge_w, sc, sh, ga2, *ffn_w)


def _pick(seq, pref):
    t = min(seq, pref)
    assert seq % t == 0
    return t


def kernel(x, c, w_ada, b_ada, g_mix, w_in, b_in, rel_bias, conv_rnn_w, conv_rnn_b, w_rg, b_rg, w_ig, b_ig, lru_lambda, w_o_attn, w_o_rnn, w_out, g_ffn, w_up, conv_ffn_w, conv_ffn_b, w_down, g_final):
    bsz, seq, d = x.shape
    depth = w_ada.shape[0]
    assert depth == 1 and d == N_RNN_BLOCKS * LANES
    dff = w_down.shape[1]
    row_tile = _pick(seq, 512)
    tq = _pick(seq, 256)

    widths = (N_HEADS * HEAD_DIM, N_KV_HEADS * HEAD_DIM, N_KV_HEADS * HEAD_DIM, IDX_HEADS * IDX_DIM,
              IDX_DIM, IDX_HEADS, d, d, d, d)
    cuts = np.cumsum((0,) + widths)
    col = lambda a, i0, i1: a[..., cuts[i0]:cuts[i1]]

    h = x
    for l in range(depth):
        mod = _adaln(c, w_ada[l], b_ada[l])
        sh1, sc1, ga1, sh2, sc2, ga2 = [m[:, None, :] for m in jnp.split(mod, 6, axis=-1)]

        w, bias = w_in[l], b_in[l]
        t_rows = [col(w, 0, 1), col(w, 3, 4), col(w, 2, 3), col(w, 5, 6)]
        t_bias = [col(bias, 0, 1), col(bias, 3, 4), col(bias, 2, 3), col(bias, 5, 6)]
        n_t = sum(r.shape[1] for r in t_rows)
        t_pad = (-n_t) % (2 * SUBLANES)
        wt = jnp.pad(jnp.concatenate(t_rows, axis=1).T, ((0, t_pad), (0, 0))).astype(BF16)
        bt = jnp.pad(jnp.concatenate(t_bias), (0, t_pad)).reshape(-1, 1)
        k_pad = (-(widths[1] + widths[4])) % LANES
        wk = jnp.pad(jnp.concatenate([col(w, 1, 2), col(w, 4, 5)], axis=1), ((0, 0), (0, k_pad))).astype(BF16)
        bk = jnp.pad(jnp.concatenate([col(bias, 1, 2), col(bias, 4, 5)]), (0, k_pad)).reshape(1, -1)
        weights = [wt, bt, wk, bk]
        for i in range(6, 10):
            weights += [col(w, i, i + 1).astype(BF16), col(bias, i, i + 1).reshape(1, -1)]
        qt, qit, vt, wit, k, ki, xr, gy, sga, sgb = _in_proj(h, sc1, sh1, g_mix[l].reshape(1, d), weights, row_tile)

        attn = _attention(qt, qit, wit, k, vt, ki, rel_bias, tq)
        rnn = _rglru(xr, gy, conv_rnn_w[l], conv_rnn_b[l].reshape(1, d), w_rg[l].astype(BF16),
                     b_rg[l].reshape(1, d), w_ig[l].astype(BF16), b_ig[l].reshape(1, d),
                     lru_lambda[l].reshape(1, d), cblk=min(d, 512))
        wu, cw, cb = w_up[l], conv_ffn_w[l], conv_ffn_b[l].reshape(1, -1)
        h = _merge_ffn(h, attn, rnn, sga, sgb, ga1, w_o_attn[l].astype(BF16), w_o_rnn[l].astype(BF16),
                       w_out[l].astype(BF16), sc2, sh2, ga2, g_ffn[l].reshape(1, d), g_final.reshape(1, d),
                       wu[:, :dff].astype(BF16), wu[:, dff:].astype(BF16), cw[:, :dff], cw[:, dff:],
                       cb[:, :dff], cb[:, dff:], w_down[l].astype(BF16), row_tile, fchunk=256)
    return h
```

```python
import functools
import math

import numpy as np
import jax
import jax.numpy as jnp
from jax import lax
from jax.experimental import pallas as pl
from jax.experimental.pallas import tpu as pltpu

N_HEADS = 8
HEAD_DIM = 64
N_KV_HEADS = 2
IDX_HEADS = 8
IDX_DIM = 64
TOPK_MAX = 256
N_BUCKETS = 32
MAX_DISTANCE = 128
N_RNN_BLOCKS = 8
RNN_CONV = 4
LRU_C = 8.0
FFN_CONV = 3
EPS = 1e-6

LANES = 128
SUBLANES = 8
VMEM_LIMIT = 56 * 1024 * 1024

MASK_NEG = -(2.0 ** 100)
KEY_NEG_INF = -2139095041

BF16 = jnp.bfloat16
F32 = jnp.float32


def _cparams(n_grid):
    return pltpu.CompilerParams(
        dimension_semantics=("arbitrary",) * n_grid, vmem_limit_bytes=VMEM_LIMIT)


def _resident(shape):
    nd = len(shape)
    return pl.BlockSpec(shape, lambda *_: (0,) * nd, pipeline_mode=pl.Buffered(1))


def _dot(a, b):
    return jnp.dot(a, b, preferred_element_type=F32)


def _dot_nt(a, b):
    return lax.dot_general(a, b, (((1,), (1,)), ((), ())), preferred_element_type=F32)


def _adaln_kernel(c_ref, w_ref, b_ref, o_ref):
    c = c_ref[...]
    c_act = (c * jax.nn.sigmoid(c)).astype(BF16)
    o_ref[...] = _dot(c_act, w_ref[...].astype(BF16)) + b_ref[...]


def _adaln(c, w, b):
    bsz, d = c.shape
    n = w.shape[1]
    return pl.pallas_call(
        _adaln_kernel,
        grid=(n // d,),
        in_specs=[pl.BlockSpec((bsz, d), lambda j: (0, 0)),
                  pl.BlockSpec((d, d), lambda j: (0, j)),
                  pl.BlockSpec((1, d), lambda j: (0, j))],
        out_specs=pl.BlockSpec((bsz, d), lambda j: (0, j)),
        out_shape=jax.ShapeDtypeStruct((bsz, n), F32),
        compiler_params=_cparams(1),
        name="adaln",
    )(c, w, b.reshape(1, n))


def _modulated_norm(x, g, sc, sh):
    ms = jnp.mean(x * x, axis=-1, keepdims=True)
    return (x * lax.rsqrt(ms + EPS) * g) * (1.0 + sc) + sh


def _in_proj_kernel(x_ref, sc_ref, sh_ref, g_ref, wt_ref, bt_ref, wk_ref, bk_ref,
                    wr_ref, br_ref, wy_ref, by_ref, wga_ref, bga_ref, wgb_ref, bgb_ref,
                    qt_ref, qit_ref, vt_ref, wit_ref, k_ref, ki_ref, xr_ref, gy_ref, sga_ref, sgb_ref):
    xn = _modulated_norm(x_ref[...], g_ref[...], sc_ref[...], sh_ref[...]).astype(BF16)

    res = _dot_nt(wt_ref[...], xn) + bt_ref[...]
    r0 = 0
    for h in range(N_HEADS):
        qt_ref[h] = (res[r0 + h * HEAD_DIM:r0 + (h + 1) * HEAD_DIM] * (HEAD_DIM ** -0.5)).astype(BF16)
    r0 += N_HEADS * HEAD_DIM
    for h in range(IDX_HEADS):
        qit_ref[h] = (res[r0 + h * IDX_DIM:r0 + (h + 1) * IDX_DIM] * (IDX_DIM ** -0.5)).astype(BF16)
    r0 += IDX_HEADS * IDX_DIM
    for j in range(N_KV_HEADS):
        vt_ref[j] = res[r0 + j * HEAD_DIM:r0 + (j + 1) * HEAD_DIM].astype(BF16)
    r0 += N_KV_HEADS * HEAD_DIM
    wit_ref[...] = res[r0:r0 + IDX_HEADS] * (IDX_HEADS ** -0.5)

    res = _dot(xn, wk_ref[...]) + bk_ref[...]
    for j in range(N_KV_HEADS):
        k_ref[j] = res[:, j * HEAD_DIM:(j + 1) * HEAD_DIM].astype(BF16)
    ki0 = N_KV_HEADS * HEAD_DIM
    ki_ref[...] = res[:, ki0:ki0 + IDX_DIM].astype(BF16)

    xr_ref[...] = (_dot(xn, wr_ref[...]) + br_ref[...]).astype(BF16)
    gy_ref[...] = jax.nn.gelu(_dot(xn, wy_ref[...]) + by_ref[...]).astype(BF16)
    sga_ref[...] = jax.nn.sigmoid(_dot(xn, wga_ref[...]) + bga_ref[...]).astype(BF16)
    sgb_ref[...] = jax.nn.sigmoid(_dot(xn, wgb_ref[...]) + bgb_ref[...]).astype(BF16)


def _in_proj(x, sc, sh, g, weights, tile):
    bsz, seq, d = x.shape
    row = lambda w: pl.BlockSpec((None, tile, w), lambda b, s: (b, s, 0))
    heads = lambda n, w: pl.BlockSpec((None, n, tile, w), lambda b, s: (b, 0, s, 0))
    heads_t = lambda n, w: pl.BlockSpec((None, n, w, tile), lambda b, s: (b, 0, 0, s))
    per_batch = pl.BlockSpec((None, 1, d), lambda b, s: (b, 0, 0))
    in_specs = [row(d), per_batch, per_batch, _resident((1, d))]
    in_specs += [_resident(w.shape) for w in weights]
    bsd = lambda w, dt: jax.ShapeDtypeStruct((bsz, seq, w), dt)
    out_shape = [
        jax.ShapeDtypeStruct((bsz, N_HEADS, HEAD_DIM, seq), BF16),
        jax.ShapeDtypeStruct((bsz, IDX_HEADS, IDX_DIM, seq), BF16),
        jax.ShapeDtypeStruct((bsz, N_KV_HEADS, HEAD_DIM, seq), BF16),
        jax.ShapeDtypeStruct((bsz, IDX_HEADS, seq), F32),
        jax.ShapeDtypeStruct((bsz, N_KV_HEADS, seq, HEAD_DIM), BF16),
        bsd(IDX_DIM, BF16),
        bsd(d, BF16), bsd(d, BF16), bsd(d, BF16), bsd(d, BF16)]
    out_specs = [heads_t(N_HEADS, HEAD_DIM), heads_t(IDX_HEADS, IDX_DIM), heads_t(N_KV_HEADS, HEAD_DIM),
                 pl.BlockSpec((None, IDX_HEADS, tile), lambda b, s: (b, 0, s)),
                 heads(N_KV_HEADS, HEAD_DIM), row(IDX_DIM),
                 row(d), row(d), row(d), row(d)]
    return pl.pallas_call(
        _in_proj_kernel,
        grid=(bsz, seq // tile),
        in_specs=in_specs, out_specs=out_specs, out_shape=out_shape,
        compiler_params=_cparams(2),
        name="in_proj",
    )(x, sc, sh, g, *weights)


def _t5_bucket_np(rel):
    max_exact = N_BUCKETS // 2
    n = np.maximum(rel, 0)
    nf = np.maximum(n, 1).astype(np.float64)
    large = max_exact + (np.log(nf / max_exact) / math.log(MAX_DISTANCE / max_exact)
                         * (N_BUCKETS - max_exact)).astype(np.int32)
    large = np.minimum(large, N_BUCKETS - 1)
    return np.where(n < max_exact, n, large).astype(np.int32)


def _far_distance():
    d = np.arange(0, 4 * MAX_DISTANCE)
    b = _t5_bucket_np(d)
    assert b[-1] == N_BUCKETS - 1
    return int(np.max(np.nonzero(b != N_BUCKETS - 1)[0])) + 1


def _sort_key(s):
    bits = lax.bitcast_convert_type(s, jnp.int32)
    return bits ^ ((bits >> 31) & jnp.int32(0x7FFFFFFF))


def _key_to_float(k):
    return lax.bitcast_convert_type(k ^ ((k >> 31) & jnp.int32(0x7FFFFFFF)), F32)


VALUE_STEPS = 12
SEARCH_UNROLL = 4
SCORE_ROWS = 64


def _attn_kernel(qt_ref, qit_ref, wit_ref, k_ref, vt_ref, ki_ref, bucket_ref, relb_ref, tri_ref,
                 o_ref, key_ref, madd_ref, bias_ref, m_ref, acc_ref, ot_ref, da_ref, db_ref, xa_ref, xb_ref, p_ref,
                 stat_ref,
                 *, tq, n_sel, n_near):
    b = pl.program_id(0)
    qt = pl.program_id(1)
    n_chunks = qt + 1
    tk = tq
    int_max = jnp.iinfo(jnp.int32).max

    @pl.when((b == 0) & (qt == 0))
    def _():
        for dlt in range(n_near + 1):
            bucket = bucket_ref[dlt]
            for h in range(N_HEADS):
                tile = jnp.full((tk, tq), relb_ref[N_BUCKETS - 1, h], F32)
                for bk in range(N_BUCKETS - 1):
                    tile = jnp.where(bucket == bk, relb_ref[bk, h], tile)
                bias_ref[dlt * N_HEADS + h] = tile

    t_row = qt * tq + lax.broadcasted_iota(jnp.int32, (1, tq), 1)

    def split(v):
        return v.reshape(v.shape[0] // SUBLANES, SUBLANES, v.shape[1])

    n_rb = tk // SCORE_ROWS
    heads_per_rb = IDX_HEADS // n_rb
    assert n_rb * SCORE_ROWS == tk and heads_per_rb * n_rb == IDX_HEADS
    key_pos = lax.broadcasted_iota(jnp.int32, (SCORE_ROWS, tq), 0)
    qry_pos = lax.broadcasted_iota(jnp.int32, (SCORE_ROWS, tq), 1)

    def idx_dots(c, d_ref, heads):
        kic = ki_ref[pl.ds(pl.multiple_of(c * tk, tk), tk), :]
        for h in heads:
            d_ref[h] = _dot(kic, qit_ref[h])

    def score_chunk(c, d_ref, nxt, dn_ref):
        off = pl.multiple_of(c * tk, tk)
        kmin, kmax, npos, nzero = stat_ref[0], stat_ref[1], stat_ref[2], stat_ref[3]
        for rb in range(n_rb):
            idx_dots(nxt, dn_ref, range(rb * heads_per_rb, (rb + 1) * heads_per_rb))
            rows = slice(rb * SCORE_ROWS, (rb + 1) * SCORE_ROWS)
            s = None
            for h in range(IDX_HEADS):
                term = wit_ref[h:h + 1, :] * jnp.maximum(d_ref[h, rows, :], 0.0)
                s = term if s is None else s + term
            valid = (c < qt) | (key_pos + rb * SCORE_ROWS <= qry_pos)
            s = jnp.where(valid, s, -jnp.inf)
            is_zero = s == 0.0
            key = jnp.where(is_zero, 0, _sort_key(s))
            key_ref[pl.ds(pl.multiple_of(off + rb * SCORE_ROWS, SCORE_ROWS), SCORE_ROWS), :] = key
            kmin = jnp.minimum(kmin, jnp.min(split(jnp.where(valid, key, int_max)), axis=0))
            kmax = jnp.maximum(kmax, jnp.max(split(key), axis=0))
            npos = npos + jnp.sum(split(jnp.where(s > 0.0, 1, 0)), axis=0)
            nzero = nzero + jnp.sum(split(jnp.where(is_zero, 1, 0)), axis=0)
        stat_ref[0], stat_ref[1], stat_ref[2], stat_ref[3] = kmin, kmax, npos, nzero

    stat_ref[0] = jnp.full((SUBLANES, tq), int_max, jnp.int32)
    stat_ref[1] = jnp.full((SUBLANES, tq), KEY_NEG_INF, jnp.int32)
    stat_ref[2] = jnp.zeros((SUBLANES, tq), jnp.int32)
    stat_ref[3] = jnp.zeros((SUBLANES, tq), jnp.int32)
    idx_dots(0, da_ref, range(IDX_HEADS))

    def score_step(c, carry):
        nxt = jnp.minimum(c + 1, qt)

        @pl.when(c % 2 == 0)
        def _():
            score_chunk(c, da_ref, nxt, db_ref)

        @pl.when(c % 2 == 1)
        def _():
            score_chunk(c, db_ref, nxt, da_ref)
        return carry

    lax.fori_loop(0, n_chunks, score_step, 0)
    kmin, kmax, npos, nzero = stat_ref[0], stat_ref[1], stat_ref[2], stat_ref[3]
    kmin = jnp.min(kmin, axis=0, keepdims=True)
    kmax = jnp.max(kmax, axis=0, keepdims=True)
    npos = jnp.sum(npos, axis=0, keepdims=True)
    nneg0 = npos + jnp.sum(nzero, axis=0, keepdims=True)

    need = jnp.minimum(t_row + 1, n_sel)

    def count_ge(mid):
        mid8 = jnp.broadcast_to(mid, (SUBLANES, tq))

        def body(c, acc):
            off = pl.multiple_of(c * tk, tk)
            return acc + jnp.sum(jnp.where(split(key_ref[pl.ds(off, tk), :]) >= mid8[None], 1, 0), axis=0)
        acc = lax.fori_loop(0, n_chunks, body, jnp.zeros((SUBLANES, tq), jnp.int32))
        return jnp.sum(acc, axis=0, keepdims=True)

    def search_cond(st):
        return jnp.min(st[-1]) == 0

    def search_steps(st):
        for _ in range(SEARCH_UNROLL):
            st = search_step(st)
        return st

    def search_step(st):
        it, lo, hi, cnt_lo, cnt_hi, done = st
        half = (lo >> 1) + (hi >> 1) + (lo & hi & 1)
        mid_f = 0.5 * _key_to_float(lo) + 0.5 * _key_to_float(hi)
        by_value = jnp.minimum(jnp.maximum(_sort_key(mid_f), lo + 1), hi - 1)
        mid = jnp.where(it < VALUE_STEPS, by_value, half)
        cnt = count_ge(mid)
        live = done == 0
        up = live & (cnt >= need)
        dn = live & (cnt < need)
        lo = jnp.where(up, mid, lo)
        cnt_lo = jnp.where(up, cnt, cnt_lo)
        hi = jnp.where(dn, mid, hi)
        cnt_hi = jnp.where(dn, cnt, cnt_hi)
        done = jnp.where(stops(lo, hi, cnt_lo, cnt_hi), 1, done)
        return it + 1, lo, hi, cnt_lo, cnt_hi, done

    def stops(lo, hi, cnt_lo, cnt_hi):
        return (cnt_lo == need) | (hi == lo + 1) | (cnt_lo - cnt_hi <= 2)

    above = need <= npos
    below = need > nneg0
    lo0 = jnp.where(above, 1, jnp.where(below, kmin, 0))
    cnt_lo0 = jnp.where(above, npos, jnp.where(below, t_row + 1, nneg0))
    hi0 = jnp.where(above, jnp.minimum(kmax, int_max - 1) + 1, jnp.where(below, 0, 1))
    cnt_hi0 = jnp.where(above, 0, jnp.where(below, nneg0, npos))
    done0 = jnp.where(stops(lo0, hi0, cnt_lo0, cnt_hi0), 1, 0)
    _, lo, hi, cnt_lo, cnt_hi, _ = lax.while_loop(
        search_cond, search_steps, (jnp.int32(0), lo0, hi0, cnt_lo0, cnt_hi0, done0))

    pending = (cnt_lo > need) & (hi > lo + 1)
    stat_ref[0] = jnp.broadcast_to(lo, (SUBLANES, tq))
    stat_ref[1] = jnp.broadcast_to(cnt_lo, (SUBLANES, tq))

    @pl.when(jnp.max(jnp.where(pending, 1, 0)) > 0)
    def _():
        lo8 = jnp.broadcast_to(lo, (SUBLANES, tq))
        hi8 = jnp.broadcast_to(hi, (SUBLANES, tq))

        def body(c, carry):
            big, small = carry
            kc = split(key_ref[pl.ds(pl.multiple_of(c * tk, tk), tk), :])
            big = jnp.maximum(big, jnp.max(jnp.where(kc < hi8[None], kc, KEY_NEG_INF), axis=0))
            small = jnp.minimum(small, jnp.min(jnp.where(kc >= lo8[None], kc, int_max), axis=0))
            return big, small
        big, small = lax.fori_loop(
            0, n_chunks, body,
            (jnp.full((SUBLANES, tq), KEY_NEG_INF, jnp.int32), jnp.full((SUBLANES, tq), int_max, jnp.int32)))
        big = jnp.max(big, axis=0, keepdims=True)
        small = jnp.min(small, axis=0, keepdims=True)
        stat_ref[0] = jnp.broadcast_to(jnp.where(pending, big, lo), (SUBLANES, tq))
        stat_ref[1] = jnp.broadcast_to(
            jnp.where(pending, cnt_hi + jnp.where(big == small, 2, 1), cnt_lo), (SUBLANES, tq))

    lo = stat_ref[0][0:1]
    cnt_lo = stat_ref[1][0:1]

    has_tie = jnp.max(jnp.where(cnt_lo > need, 1, 0)) > 0

    @pl.when(jnp.logical_not(has_tie))
    def _():
        def body(c, carry):
            off = pl.multiple_of(c * tk, tk)
            madd_ref[pl.ds(off, tk), :] = jnp.where(key_ref[pl.ds(off, tk), :] >= lo, 0.0, MASK_NEG)
            return carry
        lax.fori_loop(0, n_chunks, body, 0)

    @pl.when(has_tie)
    def _():
        tri = tri_ref[...]
        quota = jnp.where(cnt_lo > need, need - cnt_hi, n_sel).astype(F32)

        def body(c, seen):
            off = pl.multiple_of(c * tk, tk)
            kc = key_ref[pl.ds(off, tk), :]
            eq = kc == lo
            eqf = jnp.where(eq, 1.0, 0.0)
            rank = _dot(tri, eqf.astype(BF16)) + seen
            sel = (kc > lo) | (eq & (rank <= quota))
            madd_ref[pl.ds(off, tk), :] = jnp.where(sel, 0.0, MASK_NEG)
            return seen + jnp.sum(eqf, axis=0, keepdims=True)
        lax.fori_loop(0, n_chunks, body, jnp.zeros((1, tq), F32))

    m_ref[...] = jnp.full(m_ref.shape, MASK_NEG, F32)
    acc_ref[...] = jnp.zeros(acc_ref.shape, F32)
    group = N_HEADS // N_KV_HEADS
    ones_rows = jnp.ones((2 * SUBLANES, tk), BF16)

    def masked_logits(c, h):
        off = pl.multiple_of(c * tk, tk)
        near = jnp.minimum(qt - c, n_near) * N_HEADS
        return (_dot(k_ref[h // group, pl.ds(off, tk), :], qt_ref[h])
                + madd_ref[pl.ds(off, tk), :] + bias_ref[near + h]).astype(BF16)

    def logits(c, x_ref):
        for h in range(N_HEADS):
            x_ref[h] = masked_logits(c, h)

    def softmax_pv(c, x_ref, nxt, xn_ref):
        off = pl.multiple_of(c * tk, tk)
        vt_ext = [jnp.concatenate([vt_ref[kvh, :, pl.ds(off, tk)], ones_rows], axis=0)
                  for kvh in range(N_KV_HEADS)]

        def pv(h, alpha):
            prod = _dot(vt_ext[h // group], p_ref[h])[:HEAD_DIM + SUBLANES]
            acc_ref[h] = (alpha[None] * split(acc_ref[h])).reshape(HEAD_DIM + SUBLANES, tq) + prod

        prev_alpha = None
        for h in range(N_HEADS):
            xn_ref[h] = masked_logits(nxt, h)
            packed = 2 * SUBLANES
            mx = jnp.max(x_ref[h].reshape(tk // packed, packed, tq), axis=0).astype(F32)
            mx = jnp.maximum(mx[:SUBLANES], mx[SUBLANES:])
            for shift in (4, 2, 1):
                mx = jnp.maximum(mx, pltpu.roll(mx, shift, 0))
            m_old = m_ref[h]
            m_new = jnp.maximum(m_old, mx)
            alpha = jnp.exp(m_old - m_new)
            m_ref[h] = m_new
            p_ref[h] = jnp.exp(x_ref[h] - m_new[0:1].astype(BF16))
            if h > 0:
                pv(h - 1, prev_alpha)
            prev_alpha = alpha
        pv(N_HEADS - 1, prev_alpha)

    logits(0, xa_ref)

    def attend(c, carry):
        nxt = jnp.minimum(c + 1, qt)

        @pl.when(c % 2 == 0)
        def _():
            softmax_pv(c, xa_ref, nxt, xb_ref)

        @pl.when(c % 2 == 1)
        def _():
            softmax_pv(c, xb_ref, nxt, xa_ref)
        return carry

    lax.fori_loop(0, n_chunks, attend, 0)

    for h in range(N_HEADS):
        ot_ref[h * HEAD_DIM:(h + 1) * HEAD_DIM, :] = (
            acc_ref[h, 0:HEAD_DIM, :] / acc_ref[h, HEAD_DIM:HEAD_DIM + 1, :])
    o_ref[...] = ot_ref[...].T.astype(o_ref.dtype)


def _attention(qt, qit, wit, k, vt, ki, rel_bias, tq):
    bsz, _, _, seq = qt.shape
    n_sel = min(TOPK_MAX, seq // 4)
    far = _far_distance()
    n_near = min(seq // tq, (far - 1 + tq - 1) // tq + 1)
    i = np.arange(tq)[:, None]
    j = np.arange(tq)[None, :]
    bucket = np.stack([_t5_bucket_np(dlt * tq + j - i) for dlt in range(n_near)]
                      + [np.full((tq, tq), N_BUCKETS - 1, np.int32)])
    tri = (j <= i).astype(np.float32)

    kernel = functools.partial(_attn_kernel, tq=tq, n_sel=n_sel, n_near=n_near)
    qblk = lambda n, w: pl.BlockSpec((None, n, w, tq), lambda b, t: (b, 0, 0, t))
    return pl.pallas_call(
        kernel,
        grid=(bsz, seq // tq),
        in_specs=[qblk(N_HEADS, HEAD_DIM), qblk(IDX_HEADS, IDX_DIM),
                  pl.BlockSpec((None, IDX_HEADS, tq), lambda b, t: (b, 0, t)),
                  pl.BlockSpec((None, N_KV_HEADS, seq, HEAD_DIM), lambda b, t: (b, 0, 0, 0)),
                  pl.BlockSpec((None, N_KV_HEADS, HEAD_DIM, seq), lambda b, t: (b, 0, 0, 0)),
                  pl.BlockSpec((None, seq, IDX_DIM), lambda b, t: (b, 0, 0)),
                  _resident((n_near + 1, tq, tq)),
                  pl.BlockSpec(memory_space=pltpu.SMEM),
                  _resident((tq, tq))],
        out_specs=pl.BlockSpec((None, tq, N_HEADS * HEAD_DIM), lambda b, t: (b, t, 0)),
        out_shape=jax.ShapeDtypeStruct((bsz, seq, N_HEADS * HEAD_DIM), BF16),
        scratch_shapes=[pltpu.VMEM((seq, tq), jnp.int32),
                        pltpu.VMEM((seq, tq), F32),
                        pltpu.VMEM(((n_near + 1) * N_HEADS, tq, tq), F32),
                        pltpu.VMEM((N_HEADS, SUBLANES, tq), F32),
                        pltpu.VMEM((N_HEADS, HEAD_DIM + SUBLANES, tq), F32),
                        pltpu.VMEM((N_HEADS * HEAD_DIM, tq), F32),
                        pltpu.VMEM((IDX_HEADS, tq, tq), F32),
                        pltpu.VMEM((IDX_HEADS, tq, tq), F32),
                        pltpu.VMEM((N_HEADS, tq, tq), BF16),
                        pltpu.VMEM((N_HEADS, tq, tq), BF16),
                        pltpu.VMEM((N_HEADS, tq, tq), BF16),
                        pltpu.VMEM((4, SUBLANES, tq), jnp.int32)],
        compiler_params=_cparams(2),
        name="attention",
    )(qt, qit, wit, k, vt, ki, jnp.asarray(bucket), rel_bias, jnp.asarray(tri, BF16))


N_SEG = SUBLANES


def _seg_pitch(seg_len):
    quarter = seg_len // 4 + 1
    if quarter % 2 == 0:
        quarter += 1
    return 4 * quarter


def _rglru_kernel(xr_ref, gy_ref, cw_ref, cb_ref, wrg_ref, brg_ref, wig_ref, big_ref, lam_ref,
                  o_ref, xp_ref, a_ref, u_ref, *, seq, n_grp):
    seg = seq // N_SEG
    pitch = _seg_pitch(seg)
    pad = SUBLANES

    xp_ref[0:pad, :] = jnp.zeros((pad, LANES), F32)
    for g in range(n_grp):
        ls = slice(g * LANES, (g + 1) * LANES)
        xp_ref[pad:pad + seq, :] = xr_ref[:, ls].astype(F32)
        cw = cw_ref[:, ls]
        xc = cb_ref[:, ls] + cw[0:1] * xp_ref[pl.ds(pad - 3, seq), :]
        for kk in range(1, RNN_CONV):
            xc = xc + cw[kk:kk + 1] * xp_ref[pl.ds(pad - 3 + kk, seq), :]
        xb = xc.astype(BF16)
        r = jax.nn.sigmoid(_dot(xb, wrg_ref[g]) + brg_ref[:, ls])
        ig = jax.nn.sigmoid(_dot(xb, wig_ref[g]) + big_ref[:, ls])
        z = -lam_ref[:, ls]
        softplus = jnp.maximum(z, 0.0) + jnp.log1p(jnp.exp(-jnp.abs(z)))
        log_a = (-LRU_C * softplus) * r
        a = jnp.exp(log_a)
        gain2 = -jnp.tanh(log_a) * (1.0 + a * a)
        u = xc * ig * jnp.where(gain2 > 0.0, gain2 * lax.rsqrt(gain2), 0.0)
        for j in range(N_SEG):
            a_ref[g, j * pitch:j * pitch + seg, :] = a[j * seg:(j + 1) * seg]
            u_ref[g, j * pitch:j * pitch + seg, :] = u[j * seg:(j + 1) * seg]

    def step(t, st):
        new = []
        for g in range(n_grp):
            h, p = st[g]
            a_t = a_ref[g, pl.ds(t, N_SEG, stride=pitch), :]
            u_t = u_ref[g, pl.ds(t, N_SEG, stride=pitch), :]
            h = a_t * h + u_t
            p = p * a_t
            u_ref[g, pl.ds(t, N_SEG, stride=pitch), :] = h
            a_ref[g, pl.ds(t, N_SEG, stride=pitch), :] = p
            new.append((h, p))
        return tuple(new)

    init = tuple((jnp.zeros((N_SEG, LANES), F32), jnp.ones((N_SEG, LANES), F32)) for _ in range(n_grp))
    final = lax.fori_loop(0, seg, step, init)

    for g in range(n_grp):
        ls = slice(g * LANES, (g + 1) * LANES)
        h_end, p_end = final[g]
        carry = jnp.zeros((1, LANES), F32)
        for j in range(N_SEG):
            rows = slice(j * pitch, j * pitch + seg)
            hj = u_ref[g, rows, :] + a_ref[g, rows, :] * carry
            o_ref[j * seg:(j + 1) * seg, ls] = (hj * gy_ref[j * seg:(j + 1) * seg, ls].astype(F32)).astype(o_ref.dtype)
            carry = h_end[j:j + 1] + p_end[j:j + 1] * carry


def _rglru(xr, gy, cw, cb, wrg, brg, wig, big, lam, cblk):
    bsz, seq, d = xr.shape
    n_grp = cblk // LANES
    pitch = _seg_pitch(seq // N_SEG)
    kernel = functools.partial(_rglru_kernel, seq=seq, n_grp=n_grp)
    act = pl.BlockSpec((None, seq, cblk), lambda b, c: (b, 0, c))
    vec = lambda r: pl.BlockSpec((r, cblk), lambda b, c: (0, c))
    gate_w = pl.BlockSpec((n_grp, LANES, LANES), lambda b, c: (c, 0, 0))
    return pl.pallas_call(
        kernel,
        grid=(bsz, d // cblk),
        in_specs=[act, act, vec(RNN_CONV), vec(1), gate_w, vec(1), gate_w, vec(1), vec(1)],
        out_specs=act,
        out_shape=jax.ShapeDtypeStruct((bsz, seq, d), BF16),
        scratch_shapes=[pltpu.VMEM((seq + SUBLANES, LANES), F32),
                        pltpu.VMEM((n_grp, N_SEG * pitch, LANES), F32),
                        pltpu.VMEM((n_grp, N_SEG * pitch, LANES), F32)],
        compiler_params=_cparams(2),
        name="rglru",
    )(xr, gy, cw, cb, wrg, brg, wig, big, lam)


def _merge_ffn_kernel(x_ref, attn_ref, rnn_ref, sga_ref, sgb_ref, ga1_ref, woa_ref, wor_ref, wout_ref,
                      sc_ref, sh_ref, ga2_ref, g_ref, gf_ref, wv_ref, wg_ref, cwv_ref, cwg_ref,
                      cbv_ref, cbg_ref, wd_ref, o_ref, buf_ref, carry_ref, act_ref, *, tile, fchunk, n_fchunks):
    s = pl.program_id(1)
    pad = SUBLANES

    @pl.when(s == 0)
    def _():
        carry_ref[...] = jnp.zeros(carry_ref.shape, F32)

    merged = (sga_ref[...].astype(F32) * _dot(attn_ref[...], woa_ref[...])
              + sgb_ref[...].astype(F32) * _dot(rnn_ref[...], wor_ref[...]))
    hres = x_ref[...] + ga1_ref[...] * _dot(merged.astype(BF16), wout_ref[...])
    xn = _modulated_norm(hres, g_ref[...], sc_ref[...], sh_ref[...]).astype(BF16)

    def conv(up, idx, cw, cb):
        buf_ref[0:pad, :] = carry_ref[idx]
        buf_ref[pad:pad + tile, :] = up
        carry_ref[idx] = up[tile - pad:tile]
        y = cb + cw[FFN_CONV - 1:FFN_CONV] * up
        for kk in range(FFN_CONV - 1):
            y = y + cw[kk:kk + 1] * buf_ref[pl.ds(pad - (FFN_CONV - 1) + kk, tile), :]
        return y

    for c in range(n_fchunks):
        cs = slice(c * fchunk, (c + 1) * fchunk)
        val = conv(_dot(xn, wv_ref[:, cs]), 2 * c, cwv_ref[:, cs], cbv_ref[:, cs])
        gte = conv(_dot(xn, wg_ref[:, cs]), 2 * c + 1, cwg_ref[:, cs], cbg_ref[:, cs])
        act_ref[:, cs] = ((gte * jax.nn.sigmoid(gte)) * val).astype(BF16)

    h2 = hres + ga2_ref[...] * _dot(act_ref[...], wd_ref[...])
    ms = jnp.mean(h2 * h2, axis=-1, keepdims=True)
    o_ref[...] = h2 * lax.rsqrt(ms + EPS) * gf_ref[...]


def _merge_ffn(x, attn, rnn, sga, sgb, ga1, woa, wor, wout, sc, sh, ga2, g, gf,
               wv, wg, cwv, cwg, cbv, cbg, wd, tile, fchunk):
    bsz, seq, d = x.shape
    dff = wv.shape[1]
    n_fchunks = dff // fchunk
    kernel = functools.partial(_merge_ffn_kernel, tile=tile, fchunk=fchunk, n_fchunks=n_fchunks)
    row = lambda w: pl.BlockSpec((None, tile, w), lambda b, s: (b, s, 0))
    per_batch = pl.BlockSpec((None, 1, d), lambda b, s: (b, 0, 0))
    merge_w = [woa, wor, wout]
    ffn_w = [g, gf, wv, wg, cwv, cwg, cbv, cbg, wd]
    return pl.pallas_call(
        kernel,
        grid=(bsz, seq // tile),
        in_specs=([row(d), row(attn.shape[-1]), row(d), row(d), row(d), per_batch]
                  + [_resident(a.shape) for a in merge_w]
                  + [per_batch, per_batch, per_batch]
                  + [_resident(a.shape) for a in ffn_w]),
        out_specs=row(d),
        out_shape=jax.ShapeDtypeStruct((bsz, seq, d), F32),
        scratch_shapes=[pltpu.VMEM((tile + SUBLANES, fchunk), F32),
                        pltpu.VMEM((2 * n_fchunks, SUBLANES, fchunk), F32),
                        pltpu.VMEM((tile, dff), BF16)],
        compiler_params=_cparams(2),
        name="merge_ffn",
    )(x, attn, rnn, sga, sgb, ga1, *merge_w, sc, sh, ga2, *ffn_w)


def _pick(seq, pref):
    t = min(seq, pref)
    assert seq % t == 0
    return t


def kernel(x, c, w_ada, b_ada, g_mix, w_in, b_in, rel_bias, conv_rnn_w, conv_rnn_b, w_rg, b_rg, w_ig, b_ig, lru_lambda, w_o_attn, w_o_rnn, w_out, g_ffn, w_up, conv_ffn_w, conv_ffn_b, w_down, g_final):
    bsz, seq, d = x.shape
    depth = w_ada.shape[0]
    assert depth == 1 and d == N_RNN_BLOCKS * LANES
    dff = w_down.shape[1]
    row_tile = _pick(seq, 512)
    tq = _pick(seq, 256)

    widths = (N_HEADS * HEAD_DIM, N_KV_HEADS * HEAD_DIM, N_KV_HEADS * HEAD_DIM, IDX_HEADS * IDX_DIM,
              IDX_DIM, IDX_HEADS, d, d, d, d)
    cuts = np.cumsum((0,) + widths)
    col = lambda a, i0, i1: a[..., cuts[i0]:cuts[i1]]

    h = x
    for l in range(depth):
        mod = _adaln(c, w_ada[l], b_ada[l])
        sh1, sc1, ga1, sh2, sc2, ga2 = [m[:, None, :] for m in jnp.split(mod, 6, axis=-1)]

        w, bias = w_in[l], b_in[l]
        t_rows = [col(w, 0, 1), col(w, 3, 4), col(w, 2, 3), col(w, 5, 6)]
        t_bias = [col(bias, 0, 1), col(bias, 3, 4), col(bias, 2, 3), col(bias, 5, 6)]
        n_t = sum(r.shape[1] for r in t_rows)
        t_pad = (-n_t) % (2 * SUBLANES)
        wt = jnp.pad(jnp.concatenate(t_rows, axis=1).T, ((0, t_pad), (0, 0))).astype(BF16)
        bt = jnp.pad(jnp.concatenate(t_bias), (0, t_pad)).reshape(-1, 1)
        k_pad = (-(widths[1] + widths[4])) % LANES
        wk = jnp.pad(jnp.concatenate([col(w, 1, 2), col(w, 4, 5)], axis=1), ((0, 0), (0, k_pad))).astype(BF16)
        bk = jnp.pad(jnp.concatenate([col(bias, 1, 2), col(bias, 4, 5)]), (0, k_pad)).reshape(1, -1)
        weights = [wt, bt, wk, bk]
        for i in range(6, 10):
            weights += [col(w, i, i + 1).astype(BF16), col(bias, i, i + 1).reshape(1, -1)]
        qt, qit, vt, wit, k, ki, xr, gy, sga, sgb = _in_proj(h, sc1, sh1, g_mix[l].reshape(1, d), weights, row_tile)

        attn = _attention(qt, qit, wit, k, vt, ki, rel_bias, tq)
        rnn = _rglru(xr, gy, conv_rnn_w[l], conv_rnn_b[l].reshape(1, d), w_rg[l].astype(BF16),
                     b_rg[l].reshape(1, d), w_ig[l].astype(BF16), b_ig[l].reshape(1, d),
                     lru_lambda[l].reshape(1, d), cblk=min(d, 512))
        wu, cw, cb = w_up[l], conv_ffn_w[l], conv_ffn_b[l].reshape(1, -1)
        h = _merge_ffn(h, attn, rnn, sga, sgb, ga1, w_o_attn[l].astype(BF16), w_o_rnn[l].astype(BF16),
                       w_out[l].astype(BF16), sc2, sh2, ga2, g_ffn[l].reshape(1, d), g_final.reshape(1, d),
                       wu[:, :dff].astype(BF16), wu[:, dff:].astype(BF16), cw[:, :dff], cw[:, dff:],
                       cb[:, :dff], cb[:, dff:], w_down[l].astype(BF16), row_tile, fchunk=256)
    return h
```

```python
import functools
import math

import numpy as np
import jax
import jax.numpy as jnp
from jax import lax
from jax.experimental import pallas as pl
from jax.experimental.pallas import tpu as pltpu

N_HEADS = 8
HEAD_DIM = 64
N_KV_HEADS = 2
IDX_HEADS = 8
IDX_DIM = 64
TOPK_MAX = 256
N_BUCKETS = 32
MAX_DISTANCE = 128
N_RNN_BLOCKS = 8
RNN_CONV = 4
LRU_C = 8.0
FFN_CONV = 3
EPS = 1e-6

LANES = 128
SUBLANES = 8
VMEM_LIMIT = 56 * 1024 * 1024

MASK_NEG = -(2.0 ** 100)
KEY_NEG_INF = -2139095041

BF16 = jnp.bfloat16
F32 = jnp.float32


def _cparams(n_grid):
    return pltpu.CompilerParams(
        dimension_semantics=("arbitrary",) * n_grid, vmem_limit_bytes=VMEM_LIMIT)


def _resident(shape):
    nd = len(shape)
    return pl.BlockSpec(shape, lambda *_: (0,) * nd, pipeline_mode=pl.Buffered(1))


def _dot(a, b):
    return jnp.dot(a, b, preferred_element_type=F32)


def _dot_nt(a, b):
    return lax.dot_general(a, b, (((1,), (1,)), ((), ())), preferred_element_type=F32)


def _adaln_kernel(c_ref, w_ref, b_ref, o_ref):
    c = c_ref[...]
    c_act = (c * jax.nn.sigmoid(c)).astype(BF16)
    o_ref[...] = _dot(c_act, w_ref[...].astype(BF16)) + b_ref[...]


def _adaln(c, w, b):
    bsz, d = c.shape
    n = w.shape[1]
    return pl.pallas_call(
        _adaln_kernel,
        grid=(n // d,),
        in_specs=[pl.BlockSpec((bsz, d), lambda j: (0, 0)),
                  pl.BlockSpec((d, d), lambda j: (0, j)),
                  pl.BlockSpec((1, d), lambda j: (0, j))],
        out_specs=pl.BlockSpec((bsz, d), lambda j: (0, j)),
        out_shape=jax.ShapeDtypeStruct((bsz, n), F32),
        compiler_params=_cparams(1),
        name="adaln",
    )(c, w, b.reshape(1, n))


def _modulated_norm(x, g, sc, sh):
    ms = jnp.mean(x * x, axis=-1, keepdims=True)
    return (x * lax.rsqrt(ms + EPS) * g) * (1.0 + sc) + sh


def _in_proj_kernel(x_ref, sc_ref, sh_ref, g_ref, wt_ref, bt_ref, wk_ref, bk_ref,
                    wr_ref, br_ref, wy_ref, by_ref, wga_ref, bga_ref, wgb_ref, bgb_ref,
                    qt_ref, qit_ref, vt_ref, wit_ref, k_ref, ki_ref, xr_ref, gy_ref, sga_ref, sgb_ref):
    xn = _modulated_norm(x_ref[...], g_ref[...], sc_ref[...], sh_ref[...]).astype(BF16)

    res = _dot_nt(wt_ref[...], xn) + bt_ref[...]
    r0 = 0
    for h in range(N_HEADS):
        qt_ref[h] = (res[r0 + h * HEAD_DIM:r0 + (h + 1) * HEAD_DIM] * (HEAD_DIM ** -0.5)).astype(BF16)
    r0 += N_HEADS * HEAD_DIM
    for h in range(IDX_HEADS):
        qit_ref[h] = (res[r0 + h * IDX_DIM:r0 + (h + 1) * IDX_DIM] * (IDX_DIM ** -0.5)).astype(BF16)
    r0 += IDX_HEADS * IDX_DIM
    for j in range(N_KV_HEADS):
        vt_ref[j] = res[r0 + j * HEAD_DIM:r0 + (j + 1) * HEAD_DIM].astype(BF16)
    r0 += N_KV_HEADS * HEAD_DIM
    wit_ref[...] = res[r0:r0 + IDX_HEADS] * (IDX_HEADS ** -0.5)

    res = _dot(xn, wk_ref[...]) + bk_ref[...]
    for j in range(N_KV_HEADS):
        k_ref[j] = res[:, j * HEAD_DIM:(j + 1) * HEAD_DIM].astype(BF16)
    ki0 = N_KV_HEADS * HEAD_DIM
    ki_ref[...] = res[:, ki0:ki0 + IDX_DIM].astype(BF16)

    xr_ref[...] = (_dot(xn, wr_ref[...]) + br_ref[...]).astype(BF16)
    gy_ref[...] = jax.nn.gelu(_dot(xn, wy_ref[...]) + by_ref[...]).astype(BF16)
    sga_ref[...] = jax.nn.sigmoid(_dot(xn, wga_ref[...]) + bga_ref[...]).astype(BF16)
    sgb_ref[...] = jax.nn.sigmoid(_dot(xn, wgb_ref[...]) + bgb_ref[...]).astype(BF16)


def _in_proj(x, sc, sh, g, weights, tile):
    bsz, seq, d = x.shape
    row = lambda w: pl.BlockSpec((None, tile, w), lambda b, s: (b, s, 0))
    heads = lambda n, w: pl.BlockSpec((None, n, tile, w), lambda b, s: (b, 0, s, 0))
    heads_t = lambda n, w: pl.BlockSpec((None, n, w, tile), lambda b, s: (b, 0, 0, s))
    per_batch = pl.BlockSpec((None, 1, d), lambda b, s: (b, 0, 0))
    in_specs = [row(d), per_batch, per_batch, _resident((1, d))]
    in_specs += [_resident(w.shape) for w in weights]
    bsd = lambda w, dt: jax.ShapeDtypeStruct((bsz, seq, w), dt)
    out_shape = [
        jax.ShapeDtypeStruct((bsz, N_HEADS, HEAD_DIM, seq), BF16),
        jax.ShapeDtypeStruct((bsz, IDX_HEADS, IDX_DIM, seq), BF16),
        jax.ShapeDtypeStruct((bsz, N_KV_HEADS, HEAD_DIM, seq), BF16),
        jax.ShapeDtypeStruct((bsz, IDX_HEADS, seq), F32),
        jax.ShapeDtypeStruct((bsz, N_KV_HEADS, seq, HEAD_DIM), BF16),
        bsd(IDX_DIM, BF16),
        bsd(d, BF16), bsd(d, BF16), bsd(d, BF16), bsd(d, BF16)]
    out_specs = [heads_t(N_HEADS, HEAD_DIM), heads_t(IDX_HEADS, IDX_DIM), heads_t(N_KV_HEADS, HEAD_DIM),
                 pl.BlockSpec((None, IDX_HEADS, tile), lambda b, s: (b, 0, s)),
                 heads(N_KV_HEADS, HEAD_DIM), row(IDX_DIM),
                 row(d), row(d), row(d), row(d)]
    return pl.pallas_call(
        _in_proj_kernel,
        grid=(bsz, seq // tile),
        in_specs=in_specs, out_specs=out_specs, out_shape=out_shape,
        compiler_params=_cparams(2),
        name="in_proj",
    )(x, sc, sh, g, *weights)


def _t5_bucket_np(rel):
    max_exact = N_BUCKETS // 2
    n = np.maximum(rel, 0)
    nf = np.maximum(n, 1).astype(np.float64)
    large = max_exact + (np.log(nf / max_exact) / math.log(MAX_DISTANCE / max_exact)
                         * (N_BUCKETS - max_exact)).astype(np.int32)
    large = np.minimum(large, N_BUCKETS - 1)
    return np.where(n < max_exact, n, large).astype(np.int32)


def _far_distance():
    d = np.arange(0, 4 * MAX_DISTANCE)
    b = _t5_bucket_np(d)
    assert b[-1] == N_BUCKETS - 1
    return int(np.max(np.nonzero(b != N_BUCKETS - 1)[0])) + 1


def _sort_key(s):
    bits = lax.bitcast_convert_type(s, jnp.int32)
    return bits ^ ((bits >> 31) & jnp.int32(0x7FFFFFFF))


def _key_to_float(k):
    return lax.bitcast_convert_type(k ^ ((k >> 31) & jnp.int32(0x7FFFFFFF)), F32)


VALUE_STEPS = 12
SEARCH_FIXED = 16
SEARCH_UNROLL = 4
SCORE_ROWS = 64


def _attn_kernel(qt_ref, qit_ref, wit_ref, k_ref, vt_ref, ki_ref, bucket_ref, relb_ref, tri_ref,
                 o_ref, key_ref, madd_ref, bias_ref, m_ref, acc_ref, ot_ref, da_ref, db_ref, xa_ref, xb_ref, p_ref,
                 stat_ref,
                 *, tq, n_sel, n_near):
    b = pl.program_id(0)
    qt = pl.program_id(1)
    n_chunks = qt + 1
    tk = tq
    int_max = jnp.iinfo(jnp.int32).max

    @pl.when((b == 0) & (qt == 0))
    def _():
        for dlt in range(n_near + 1):
            bucket = bucket_ref[dlt]
            for h in range(N_HEADS):
                tile = jnp.full((tk, tq), relb_ref[N_BUCKETS - 1, h], F32)
                for bk in range(N_BUCKETS - 1):
                    tile = jnp.where(bucket == bk, relb_ref[bk, h], tile)
                bias_ref[dlt * N_HEADS + h] = tile

    t_row = qt * tq + lax.broadcasted_iota(jnp.int32, (1, tq), 1)

    def split(v):
        return v.reshape(v.shape[0] // SUBLANES, SUBLANES, v.shape[1])

    n_rb = tk // SCORE_ROWS
    heads_per_rb = IDX_HEADS // n_rb
    assert n_rb * SCORE_ROWS == tk and heads_per_rb * n_rb == IDX_HEADS
    key_pos = lax.broadcasted_iota(jnp.int32, (SCORE_ROWS, tq), 0)
    qry_pos = lax.broadcasted_iota(jnp.int32, (SCORE_ROWS, tq), 1)

    def idx_dots(c, d_ref, heads):
        kic = ki_ref[pl.ds(pl.multiple_of(c * tk, tk), tk), :]
        for h in heads:
            d_ref[h] = _dot(kic, qit_ref[h])

    def score_chunk(c, d_ref, nxt, dn_ref):
        off = pl.multiple_of(c * tk, tk)
        kmin, kmax, npos, nzero = stat_ref[0], stat_ref[1], stat_ref[2], stat_ref[3]
        for rb in range(n_rb):
            idx_dots(nxt, dn_ref, range(rb * heads_per_rb, (rb + 1) * heads_per_rb))
            rows = slice(rb * SCORE_ROWS, (rb + 1) * SCORE_ROWS)
            s = None
            for h in range(IDX_HEADS):
                term = wit_ref[h:h + 1, :] * jnp.maximum(d_ref[h, rows, :], 0.0)
                s = term if s is None else s + term
            valid = (c < qt) | (key_pos + rb * SCORE_ROWS <= qry_pos)
            s = jnp.where(valid, s, -jnp.inf)
            is_zero = s == 0.0
            key = jnp.where(is_zero, 0, _sort_key(s))
            key_ref[pl.ds(pl.multiple_of(off + rb * SCORE_ROWS, SCORE_ROWS), SCORE_ROWS), :] = key
            kmin = jnp.minimum(kmin, jnp.min(split(jnp.where(valid, key, int_max)), axis=0))
            kmax = jnp.maximum(kmax, jnp.max(split(key), axis=0))
            npos = npos + jnp.sum(split(jnp.where(s > 0.0, 1, 0)), axis=0)
            nzero = nzero + jnp.sum(split(jnp.where(is_zero, 1, 0)), axis=0)
        stat_ref[0], stat_ref[1], stat_ref[2], stat_ref[3] = kmin, kmax, npos, nzero

    stat_ref[0] = jnp.full((SUBLANES, tq), int_max, jnp.int32)
    stat_ref[1] = jnp.full((SUBLANES, tq), KEY_NEG_INF, jnp.int32)
    stat_ref[2] = jnp.zeros((SUBLANES, tq), jnp.int32)
    stat_ref[3] = jnp.zeros((SUBLANES, tq), jnp.int32)
    idx_dots(0, da_ref, range(IDX_HEADS))

    def score_step(c, carry):
        nxt = jnp.minimum(c + 1, qt)

        @pl.when(c % 2 == 0)
        def _():
            score_chunk(c, da_ref, nxt, db_ref)

        @pl.when(c % 2 == 1)
        def _():
            score_chunk(c, db_ref, nxt, da_ref)
        return carry

    lax.fori_loop(0, n_chunks, score_step, 0)
    kmin, kmax, npos, nzero = stat_ref[0], stat_ref[1], stat_ref[2], stat_ref[3]
    kmin = jnp.min(kmin, axis=0, keepdims=True)
    kmax = jnp.max(kmax, axis=0, keepdims=True)
    npos = jnp.sum(npos, axis=0, keepdims=True)
    nneg0 = npos + jnp.sum(nzero, axis=0, keepdims=True)

    need = jnp.minimum(t_row + 1, n_sel)

    def count_ge(mid):
        mid8 = jnp.broadcast_to(mid, (SUBLANES, tq))

        def body(c, acc):
            off = pl.multiple_of(c * tk, tk)
            return acc + jnp.sum(jnp.where(split(key_ref[pl.ds(off, tk), :]) >= mid8[None], 1, 0), axis=0)
        acc = lax.fori_loop(0, n_chunks, body, jnp.zeros((SUBLANES, tq), jnp.int32))
        return jnp.sum(acc, axis=0, keepdims=True)

    def search_cond(st):
        return jnp.min(st[-1]) == 0

    def search_steps(st):
        for _ in range(SEARCH_UNROLL):
            st = search_step(st)
        return st

    def search_step(st):
        it, lo, hi, cnt_lo, cnt_hi, done = st
        half = (lo >> 1) + (hi >> 1) + (lo & hi & 1)
        mid_f = 0.5 * _key_to_float(lo) + 0.5 * _key_to_float(hi)
        by_value = jnp.minimum(jnp.maximum(_sort_key(mid_f), lo + 1), hi - 1)
        mid = jnp.where(it < VALUE_STEPS, by_value, half)
        cnt = count_ge(mid)
        live = done == 0
        up = live & (cnt >= need)
        dn = live & (cnt < need)
        lo = jnp.where(up, mid, lo)
        cnt_lo = jnp.where(up, cnt, cnt_lo)
        hi = jnp.where(dn, mid, hi)
        cnt_hi = jnp.where(dn, cnt, cnt_hi)
        done = jnp.where(stops(lo, hi, cnt_lo, cnt_hi), 1, done)
        return it + 1, lo, hi, cnt_lo, cnt_hi, done

    def stops(lo, hi, cnt_lo, cnt_hi):
        return (cnt_lo == need) | (hi == lo + 1) | (cnt_lo - cnt_hi <= 2)

    above = need <= npos
    below = need > nneg0
    lo0 = jnp.where(above, 1, jnp.where(below, kmin, 0))
    cnt_lo0 = jnp.where(above, npos, jnp.where(below, t_row + 1, nneg0))
    hi0 = jnp.where(above, jnp.minimum(kmax, int_max - 1) + 1, jnp.where(below, 0, 1))
    cnt_hi0 = jnp.where(above, 0, jnp.where(below, nneg0, npos))
    done0 = jnp.where(stops(lo0, hi0, cnt_lo0, cnt_hi0), 1, 0)
    searching = (qt + 1) * tq > n_sel
    state = lax.fori_loop(0, jnp.where(searching, SEARCH_FIXED, 0), lambda _, st: search_step(st),
                          (jnp.int32(0), lo0, hi0, cnt_lo0, cnt_hi0, done0))
    _, lo, hi, cnt_lo, cnt_hi, _ = lax.while_loop(search_cond, search_steps, state)

    pending = (cnt_lo > need) & (hi > lo + 1)
    stat_ref[0] = jnp.broadcast_to(lo, (SUBLANES, tq))
    stat_ref[1] = jnp.broadcast_to(cnt_lo, (SUBLANES, tq))

    @pl.when(searching)
    def _():
        lo8 = jnp.broadcast_to(lo, (SUBLANES, tq))
        hi8 = jnp.broadcast_to(hi, (SUBLANES, tq))

        def body(c, carry):
            big, small = carry
            kc = split(key_ref[pl.ds(pl.multiple_of(c * tk, tk), tk), :])
            big = jnp.maximum(big, jnp.max(jnp.where(kc < hi8[None], kc, KEY_NEG_INF), axis=0))
            small = jnp.minimum(small, jnp.min(jnp.where(kc >= lo8[None], kc, int_max), axis=0))
            return big, small
        big, small = lax.fori_loop(
            0, n_chunks, body,
            (jnp.full((SUBLANES, tq), KEY_NEG_INF, jnp.int32), jnp.full((SUBLANES, tq), int_max, jnp.int32)))
        big = jnp.max(big, axis=0, keepdims=True)
        small = jnp.min(small, axis=0, keepdims=True)
        stat_ref[0] = jnp.broadcast_to(jnp.where(pending, big, lo), (SUBLANES, tq))
        stat_ref[1] = jnp.broadcast_to(
            jnp.where(pending, cnt_hi + jnp.where(big == small, 2, 1), cnt_lo), (SUBLANES, tq))

    lo = stat_ref[0][0:1]
    cnt_lo = stat_ref[1][0:1]

    has_tie = jnp.max(jnp.where(cnt_lo > need, 1, 0)) > 0

    @pl.when(jnp.logical_not(has_tie))
    def _():
        def body(c, carry):
            off = pl.multiple_of(c * tk, tk)
            madd_ref[pl.ds(off, tk), :] = jnp.where(key_ref[pl.ds(off, tk), :] >= lo, 0.0, MASK_NEG)
            return carry
        lax.fori_loop(0, n_chunks, body, 0)

    @pl.when(has_tie)
    def _():
        tri = tri_ref[...]
        quota = jnp.where(cnt_lo > need, need - cnt_hi, n_sel).astype(F32)

        def body(c, seen):
            off = pl.multiple_of(c * tk, tk)
            kc = key_ref[pl.ds(off, tk), :]
            eq = kc == lo
            eqf = jnp.where(eq, 1.0, 0.0)
            rank = _dot(tri, eqf.astype(BF16)) + seen
            sel = (kc > lo) | (eq & (rank <= quota))
            madd_ref[pl.ds(off, tk), :] = jnp.where(sel, 0.0, MASK_NEG)
            return seen + jnp.sum(eqf, axis=0, keepdims=True)
        lax.fori_loop(0, n_chunks, body, jnp.zeros((1, tq), F32))

    m_ref[...] = jnp.full(m_ref.shape, MASK_NEG, F32)
    acc_ref[...] = jnp.zeros(acc_ref.shape, F32)
    group = N_HEADS // N_KV_HEADS
    ones_rows = jnp.ones((2 * SUBLANES, tk), BF16)

    def masked_logits(c, h):
        off = pl.multiple_of(c * tk, tk)
        near = jnp.minimum(qt - c, n_near) * N_HEADS
        return (_dot(k_ref[h // group, pl.ds(off, tk), :], qt_ref[h])
                + madd_ref[pl.ds(off, tk), :] + bias_ref[near + h]).astype(BF16)

    def logits(c, x_ref):
        for h in range(N_HEADS):
            x_ref[h] = masked_logits(c, h)

    def softmax_pv(c, x_ref, nxt, xn_ref):
        off = pl.multiple_of(c * tk, tk)
        vt_ext = [jnp.concatenate([vt_ref[kvh, :, pl.ds(off, tk)], ones_rows], axis=0)
                  for kvh in range(N_KV_HEADS)]

        def pv(h, alpha):
            prod = _dot(vt_ext[h // group], p_ref[h])[:HEAD_DIM + SUBLANES]
            acc_ref[h] = (alpha[None] * split(acc_ref[h])).reshape(HEAD_DIM + SUBLANES, tq) + prod

        prev_alpha = None
        for h in range(N_HEADS):
            xn_ref[h] = masked_logits(nxt, h)
            packed = 2 * SUBLANES
            mx = jnp.max(x_ref[h].reshape(tk // packed, packed, tq), axis=0).astype(F32)
            mx = jnp.maximum(mx[:SUBLANES], mx[SUBLANES:])
            for shift in (4, 2, 1):
                mx = jnp.maximum(mx, pltpu.roll(mx, shift, 0))
            m_old = m_ref[h]
            m_new = jnp.maximum(m_old, mx)
            alpha = jnp.exp(m_old - m_new)
            m_ref[h] = m_new
            p_ref[h] = jnp.exp(x_ref[h] - m_new[0:1].astype(BF16))
            if h > 0:
                pv(h - 1, prev_alpha)
            prev_alpha = alpha
        pv(N_HEADS - 1, prev_alpha)

    logits(0, xa_ref)

    def attend(c, carry):
        nxt = jnp.minimum(c + 1, qt)

        @pl.when(c % 2 == 0)
        def _():
            softmax_pv(c, xa_ref, nxt, xb_ref)

        @pl.when(c % 2 == 1)
        def _():
            softmax_pv(c, xb_ref, nxt, xa_ref)
        return carry

    lax.fori_loop(0, n_chunks, attend, 0)

    for h in range(N_HEADS):
        ot_ref[h * HEAD_DIM:(h + 1) * HEAD_DIM, :] = (
            acc_ref[h, 0:HEAD_DIM, :] / acc_ref[h, HEAD_DIM:HEAD_DIM + 1, :])
    o_ref[...] = ot_ref[...].T.astype(o_ref.dtype)


def _attention(qt, qit, wit, k, vt, ki, rel_bias, tq):
    bsz, _, _, seq = qt.shape
    n_sel = min(TOPK_MAX, seq // 4)
    far = _far_distance()
    n_near = min(seq // tq, (far - 1 + tq - 1) // tq + 1)
    i = np.arange(tq)[:, None]
    j = np.arange(tq)[None, :]
    bucket = np.stack([_t5_bucket_np(dlt * tq + j - i) for dlt in range(n_near)]
                      + [np.full((tq, tq), N_BUCKETS - 1, np.int32)])
    tri = (j <= i).astype(np.float32)

    kernel = functools.partial(_attn_kernel, tq=tq, n_sel=n_sel, n_near=n_near)
    qblk = lambda n, w: pl.BlockSpec((None, n, w, tq), lambda b, t: (b, 0, 0, t))
    return pl.pallas_call(
        kernel,
        grid=(bsz, seq // tq),
        in_specs=[qblk(N_HEADS, HEAD_DIM), qblk(IDX_HEADS, IDX_DIM),
                  pl.BlockSpec((None, IDX_HEADS, tq), lambda b, t: (b, 0, t)),
                  pl.BlockSpec((None, N_KV_HEADS, seq, HEAD_DIM), lambda b, t: (b, 0, 0, 0)),
                  pl.BlockSpec((None, N_KV_HEADS, HEAD_DIM, seq), lambda b, t: (b, 0, 0, 0)),
                  pl.BlockSpec((None, seq, IDX_DIM), lambda b, t: (b, 0, 0)),
                  _resident((n_near + 1, tq, tq)),
                  pl.BlockSpec(memory_space=pltpu.SMEM),
                  _resident((tq, tq))],
        out_specs=pl.BlockSpec((None, tq, N_HEADS * HEAD_DIM), lambda b, t: (b, t, 0)),
        out_shape=jax.ShapeDtypeStruct((bsz, seq, N_HEADS * HEAD_DIM), BF16),
        scratch_shapes=[pltpu.VMEM((seq, tq), jnp.int32),
                        pltpu.VMEM((seq, tq), F32),
                        pltpu.VMEM(((n_near + 1) * N_HEADS, tq, tq), F32),
                        pltpu.VMEM((N_HEADS, SUBLANES, tq), F32),
                        pltpu.VMEM((N_HEADS, HEAD_DIM + SUBLANES, tq), F32),
                        pltpu.VMEM((N_HEADS * HEAD_DIM, tq), F32),
                        pltpu.VMEM((IDX_HEADS, tq, tq), F32),
                        pltpu.VMEM((IDX_HEADS, tq, tq), F32),
                        pltpu.VMEM((N_HEADS, tq, tq), BF16),
                        pltpu.VMEM((N_HEADS, tq, tq), BF16),
                        pltpu.VMEM((N_HEADS, tq, tq), BF16),
                        pltpu.VMEM((4, SUBLANES, tq), jnp.int32)],
        compiler_params=_cparams(2),
        name="attention",
    )(qt, qit, wit, k, vt, ki, jnp.asarray(bucket), rel_bias, jnp.asarray(tri, BF16))


N_SEG = SUBLANES


def _seg_pitch(seg_len):
    quarter = seg_len // 4 + 1
    if quarter % 2 == 0:
        quarter += 1
    return 4 * quarter


def _rglru_kernel(xr_ref, gy_ref, cw_ref, cb_ref, wrg_ref, brg_ref, wig_ref, big_ref, lam_ref,
                  o_ref, xp_ref, a_ref, u_ref, *, seq, n_grp):
    seg = seq // N_SEG
    pitch = _seg_pitch(seg)
    pad = SUBLANES

    xp_ref[0:pad, :] = jnp.zeros((pad, LANES), F32)
    for g in range(n_grp):
        ls = slice(g * LANES, (g + 1) * LANES)
        xp_ref[pad:pad + seq, :] = xr_ref[:, ls].astype(F32)
        cw = cw_ref[:, ls]
        xc = cb_ref[:, ls] + cw[0:1] * xp_ref[pl.ds(pad - 3, seq), :]
        for kk in range(1, RNN_CONV):
            xc = xc + cw[kk:kk + 1] * xp_ref[pl.ds(pad - 3 + kk, seq), :]
        xb = xc.astype(BF16)
        r = jax.nn.sigmoid(_dot(xb, wrg_ref[g]) + brg_ref[:, ls])
        ig = jax.nn.sigmoid(_dot(xb, wig_ref[g]) + big_ref[:, ls])
        z = -lam_ref[:, ls]
        softplus = jnp.maximum(z, 0.0) + jnp.log1p(jnp.exp(-jnp.abs(z)))
        log_a = (-LRU_C * softplus) * r
        a = jnp.exp(log_a)
        gain2 = -jnp.tanh(log_a) * (1.0 + a * a)
        u = xc * ig * jnp.where(gain2 > 0.0, gain2 * lax.rsqrt(gain2), 0.0)
        for j in range(N_SEG):
            a_ref[g, j * pitch:j * pitch + seg, :] = a[j * seg:(j + 1) * seg]
            u_ref[g, j * pitch:j * pitch + seg, :] = u[j * seg:(j + 1) * seg]

    def step(t, st):
        new = []
        for g in range(n_grp):
            h, p = st[g]
            a_t = a_ref[g, pl.ds(t, N_SEG, stride=pitch), :]
            u_t = u_ref[g, pl.ds(t, N_SEG, stride=pitch), :]
            h = a_t * h + u_t
            p = p * a_t
            u_ref[g, pl.ds(t, N_SEG, stride=pitch), :] = h
            a_ref[g, pl.ds(t, N_SEG, stride=pitch), :] = p
            new.append((h, p))
        return tuple(new)

    init = tuple((jnp.zeros((N_SEG, LANES), F32), jnp.ones((N_SEG, LANES), F32)) for _ in range(n_grp))
    final = lax.fori_loop(0, seg, step, init)

    for g in range(n_grp):
        ls = slice(g * LANES, (g + 1) * LANES)
        h_end, p_end = final[g]
        carry = jnp.zeros((1, LANES), F32)
        for j in range(N_SEG):
            rows = slice(j * pitch, j * pitch + seg)
            hj = u_ref[g, rows, :] + a_ref[g, rows, :] * carry
            o_ref[j * seg:(j + 1) * seg, ls] = (hj * gy_ref[j * seg:(j + 1) * seg, ls].astype(F32)).astype(o_ref.dtype)
            carry = h_end[j:j + 1] + p_end[j:j + 1] * carry


def _rglru(xr, gy, cw, cb, wrg, brg, wig, big, lam, cblk):
    bsz, seq, d = xr.shape
    n_grp = cblk // LANES
    pitch = _seg_pitch(seq // N_SEG)
    kernel = functools.partial(_rglru_kernel, seq=seq, n_grp=n_grp)
    act = pl.BlockSpec((None, seq, cblk), lambda b, c: (b, 0, c))
    vec = lambda r: pl.BlockSpec((r, cblk), lambda b, c: (0, c))
    gate_w = pl.BlockSpec((n_grp, LANES, LANES), lambda b, c: (c, 0, 0))
    return pl.pallas_call(
        kernel,
        grid=(bsz, d // cblk),
        in_specs=[act, act, vec(RNN_CONV), vec(1), gate_w, vec(1), gate_w, vec(1), vec(1)],
        out_specs=act,
        out_shape=jax.ShapeDtypeStruct((bsz, seq, d), BF16),
        scratch_shapes=[pltpu.VMEM((seq + SUBLANES, LANES), F32),
                        pltpu.VMEM((n_grp, N_SEG * pitch, LANES), F32),
                        pltpu.VMEM((n_grp, N_SEG * pitch, LANES), F32)],
        compiler_params=_cparams(2),
        name="rglru",
    )(xr, gy, cw, cb, wrg, brg, wig, big, lam)


def _merge_ffn_kernel(x_ref, attn_ref, rnn_ref, sga_ref, sgb_ref, ga1_ref, woa_ref, wor_ref, wout_ref,
                      sc_ref, sh_ref, ga2_ref, g_ref, gf_ref, wv_ref, wg_ref, cwv_ref, cwg_ref,
                      cbv_ref, cbg_ref, wd_ref, o_ref, buf_ref, carry_ref, act_ref, *, tile, fchunk, n_fchunks):
    s = pl.program_id(1)
    pad = SUBLANES

    @pl.when(s == 0)
    def _():
        carry_ref[...] = jnp.zeros(carry_ref.shape, F32)

    merged = (sga_ref[...].astype(F32) * _dot(attn_ref[...], woa_ref[...])
              + sgb_ref[...].astype(F32) * _dot(rnn_ref[...], wor_ref[...]))
    hres = x_ref[...] + ga1_ref[...] * _dot(merged.astype(BF16), wout_ref[...])
    xn = _modulated_norm(hres, g_ref[...], sc_ref[...], sh_ref[...]).astype(BF16)

    def conv(up, idx, cw, cb):
        buf_ref[0:pad, :] = carry_ref[idx]
        buf_ref[pad:pad + tile, :] = up
        carry_ref[idx] = up[tile - pad:tile]
        y = cb + cw[FFN_CONV - 1:FFN_CONV] * up
        for kk in range(FFN_CONV - 1):
            y = y + cw[kk:kk + 1] * buf_ref[pl.ds(pad - (FFN_CONV - 1) + kk, tile), :]
        return y

    for c in range(n_fchunks):
        cs = slice(c * fchunk, (c + 1) * fchunk)
        val = conv(_dot(xn, wv_ref[:, cs]), 2 * c, cwv_ref[:, cs], cbv_ref[:, cs])
        gte = conv(_dot(xn, wg_ref[:, cs]), 2 * c + 1, cwg_ref[:, cs], cbg_ref[:, cs])
        act_ref[:, cs] = ((gte * jax.nn.sigmoid(gte)) * val).astype(BF16)

    h2 = hres + ga2_ref[...] * _dot(act_ref[...], wd_ref[...])
    ms = jnp.mean(h2 * h2, axis=-1, keepdims=True)
    o_ref[...] = h2 * lax.rsqrt(ms + EPS) * gf_ref[...]


def _merge_ffn(x, attn, rnn, sga, sgb, ga1, woa, wor, wout, sc, sh, ga2, g, gf,
               wv, wg, cwv, cwg, cbv, cbg, wd, tile, fchunk):
    bsz, seq, d = x.shape
    dff = wv.shape[1]
    n_fchunks = dff // fchunk
    kernel = functools.partial(_merge_ffn_kernel, tile=tile, fchunk=fchunk, n_fchunks=n_fchunks)
    row = lambda w: pl.BlockSpec((None, tile, w), lambda b, s: (b, s, 0))
    per_batch = pl.BlockSpec((None, 1, d), lambda b, s: (b, 0, 0))
    merge_w = [woa, wor, wout]
    ffn_w = [g, gf, wv, wg, cwv, cwg, cbv, cbg, wd]
    return pl.pallas_call(
        kernel,
        grid=(bsz, seq // tile),
        in_specs=([row(d), row(attn.shape[-1]), row(d), row(d), row(d), per_batch]
                  + [_resident(a.shape) for a in merge_w]
                  + [per_batch, per_batch, per_batch]
                  + [_resident(a.shape) for a in ffn_w]),
        out_specs=row(d),
        out_shape=jax.ShapeDtypeStruct((bsz, seq, d), F32),
        scratch_shapes=[pltpu.VMEM((tile + SUBLANES, fchunk), F32),
                        pltpu.VMEM((2 * n_fchunks, SUBLANES, fchunk), F32),
                        pltpu.VMEM((tile, dff), BF16)],
        compiler_params=_cparams(2),
        name="merge_ffn",
    )(x, attn, rnn, sga, sgb, ga1, *merge_w, sc, sh, ga2, *ffn_w)


def _pick(seq, pref):
    t = min(seq, pref)
    assert seq % t == 0
    return t


def kernel(x, c, w_ada, b_ada, g_mix, w_in, b_in, rel_bias, conv_rnn_w, conv_rnn_b, w_rg, b_rg, w_ig, b_ig, lru_lambda, w_o_attn, w_o_rnn, w_out, g_ffn, w_up, conv_ffn_w, conv_ffn_b, w_down, g_final):
    bsz, seq, d = x.shape
    depth = w_ada.shape[0]
    assert depth == 1 and d == N_RNN_BLOCKS * LANES
    dff = w_down.shape[1]
    row_tile = _pick(seq, 512)
    tq = _pick(seq, 256)

    widths = (N_HEADS * HEAD_DIM, N_KV_HEADS * HEAD_DIM, N_KV_HEADS * HEAD_DIM, IDX_HEADS * IDX_DIM,
              IDX_DIM, IDX_HEADS, d, d, d, d)
    cuts = np.cumsum((0,) + widths)
    col = lambda a, i0, i1: a[..., cuts[i0]:cuts[i1]]

    h = x
    for l in range(depth):
        mod = _adaln(c, w_ada[l], b_ada[l])
        sh1, sc1, ga1, sh2, sc2, ga2 = [m[:, None, :] for m in jnp.split(mod, 6, axis=-1)]

        w, bias = w_in[l], b_in[l]
        t_rows = [col(w, 0, 1), col(w, 3, 4), col(w, 2, 3), col(w, 5, 6)]
        t_bias = [col(bias, 0, 1), col(bias, 3, 4), col(bias, 2, 3), col(bias, 5, 6)]
        n_t = sum(r.shape[1] for r in t_rows)
        t_pad = (-n_t) % (2 * SUBLANES)
        wt = jnp.pad(jnp.concatenate(t_rows, axis=1).T, ((0, t_pad), (0, 0))).astype(BF16)
        bt = jnp.pad(jnp.concatenate(t_bias), (0, t_pad)).reshape(-1, 1)
        k_pad = (-(widths[1] + widths[4])) % LANES
        wk = jnp.pad(jnp.concatenate([col(w, 1, 2), col(w, 4, 5)], axis=1), ((0, 0), (0, k_pad))).astype(BF16)
        bk = jnp.pad(jnp.concatenate([col(bias, 1, 2), col(bias, 4, 5)]), (0, k_pad)).reshape(1, -1)
        weights = [wt, bt, wk, bk]
        for i in range(6, 10):
            weights += [col(w, i, i + 1).astype(BF16), col(bias, i, i + 1).reshape(1, -1)]
        qt, qit, vt, wit, k, ki, xr, gy, sga, sgb = _in_proj(h, sc1, sh1, g_mix[l].reshape(1, d), weights, row_tile)

        attn = _attention(qt, qit, wit, k, vt, ki, rel_bias, tq)
        rnn = _rglru(xr, gy, conv_rnn_w[l], conv_rnn_b[l].reshape(1, d), w_rg[l].astype(BF16),
                     b_rg[l].reshape(1, d), w_ig[l].astype(BF16), b_ig[l].reshape(1, d),
                     lru_lambda[l].reshape(1, d), cblk=min(d, 512))
        wu, cw, cb = w_up[l], conv_ffn_w[l], conv_ffn_b[l].reshape(1, -1)
        h = _merge_ffn(h, attn, rnn, sga, sgb, ga1, w_o_attn[l].astype(BF16), w_o_rnn[l].astype(BF16),
                       w_out[l].astype(BF16), sc2, sh2, ga2, g_ffn[l].reshape(1, d), g_final.reshape(1, d),
                       wu[:, :dff].astype(BF16), wu[:, dff:].astype(BF16), cw[:, :dff], cw[:, dff:],
                       cb[:, :dff], cb[:, dff:], w_down[l].astype(BF16), row_tile, fchunk=256)
    return h
```

```python
import functools
import math

import numpy as np
import jax
import jax.numpy as jnp
from jax import lax
from jax.experimental import pallas as pl
from jax.experimental.pallas import tpu as pltpu

N_HEADS = 8
HEAD_DIM = 64
N_KV_HEADS = 2
IDX_HEADS = 8
IDX_DIM = 64
TOPK_MAX = 256
N_BUCKETS = 32
MAX_DISTANCE = 128
N_RNN_BLOCKS = 8
RNN_CONV = 4
LRU_C = 8.0
FFN_CONV = 3
EPS = 1e-6

LANES = 128
SUBLANES = 8
VMEM_LIMIT = 56 * 1024 * 1024

MASK_NEG = -(2.0 ** 100)
KEY_NEG_INF = -2139095041

BF16 = jnp.bfloat16
F32 = jnp.float32


def _cparams(n_grid):
    return pltpu.CompilerParams(
        dimension_semantics=("arbitrary",) * n_grid, vmem_limit_bytes=VMEM_LIMIT)


def _resident(shape):
    nd = len(shape)
    return pl.BlockSpec(shape, lambda *_: (0,) * nd, pipeline_mode=pl.Buffered(1))


def _dot(a, b):
    return jnp.dot(a, b, preferred_element_type=F32)


def _dot_nt(a, b):
    return lax.dot_general(a, b, (((1,), (1,)), ((), ())), preferred_element_type=F32)


def _adaln_kernel(c_ref, w_ref, b_ref, o_ref):
    c = c_ref[...]
    c_act = (c * jax.nn.sigmoid(c)).astype(BF16)
    o_ref[...] = _dot(c_act, w_ref[...].astype(BF16)) + b_ref[...]


def _adaln(c, w, b):
    bsz, d = c.shape
    n = w.shape[1]
    return pl.pallas_call(
        _adaln_kernel,
        grid=(n // d,),
        in_specs=[pl.BlockSpec((bsz, d), lambda j: (0, 0)),
                  pl.BlockSpec((d, d), lambda j: (0, j)),
                  pl.BlockSpec((1, d), lambda j: (0, j))],
        out_specs=pl.BlockSpec((bsz, d), lambda j: (0, j)),
        out_shape=jax.ShapeDtypeStruct((bsz, n), F32),
        compiler_params=_cparams(1),
        name="adaln",
    )(c, w, b.reshape(1, n))


def _modulated_norm(x, g, sc, sh):
    ms = jnp.mean(x * x, axis=-1, keepdims=True)
    return (x * lax.rsqrt(ms + EPS) * g) * (1.0 + sc) + sh


def _in_proj_kernel(x_ref, sc_ref, sh_ref, g_ref, wt_ref, bt_ref, wk_ref, bk_ref,
                    wr_ref, br_ref, wy_ref, by_ref, wga_ref, bga_ref, wgb_ref, bgb_ref,
                    qt_ref, qit_ref, vt_ref, wit_ref, k_ref, ki_ref, xr_ref, gy_ref, sga_ref, sgb_ref):
    xn = _modulated_norm(x_ref[...], g_ref[...], sc_ref[...], sh_ref[...]).astype(BF16)

    res = _dot_nt(wt_ref[...], xn) + bt_ref[...]
    r0 = 0
    for h in range(N_HEADS):
        qt_ref[h] = (res[r0 + h * HEAD_DIM:r0 + (h + 1) * HEAD_DIM] * (HEAD_DIM ** -0.5)).astype(BF16)
    r0 += N_HEADS * HEAD_DIM
    for h in range(IDX_HEADS):
        qit_ref[h] = (res[r0 + h * IDX_DIM:r0 + (h + 1) * IDX_DIM] * (IDX_DIM ** -0.5)).astype(BF16)
    r0 += IDX_HEADS * IDX_DIM
    for j in range(N_KV_HEADS):
        vt_ref[j] = res[r0 + j * HEAD_DIM:r0 + (j + 1) * HEAD_DIM].astype(BF16)
    r0 += N_KV_HEADS * HEAD_DIM
    wit_ref[...] = res[r0:r0 + IDX_HEADS] * (IDX_HEADS ** -0.5)

    res = _dot(xn, wk_ref[...]) + bk_ref[...]
    for j in range(N_KV_HEADS):
        k_ref[j] = res[:, j * HEAD_DIM:(j + 1) * HEAD_DIM].astype(BF16)
    ki0 = N_KV_HEADS * HEAD_DIM
    ki_ref[...] = res[:, ki0:ki0 + IDX_DIM].astype(BF16)

    xr_ref[...] = (_dot(xn, wr_ref[...]) + br_ref[...]).astype(BF16)
    gy_ref[...] = jax.nn.gelu(_dot(xn, wy_ref[...]) + by_ref[...]).astype(BF16)
    sga_ref[...] = jax.nn.sigmoid(_dot(xn, wga_ref[...]) + bga_ref[...]).astype(BF16)
    sgb_ref[...] = jax.nn.sigmoid(_dot(xn, wgb_ref[...]) + bgb_ref[...]).astype(BF16)


def _in_proj(x, sc, sh, g, weights, tile):
    bsz, seq, d = x.shape
    row = lambda w: pl.BlockSpec((None, tile, w), lambda b, s: (b, s, 0))
    heads = lambda n, w: pl.BlockSpec((None, n, tile, w), lambda b, s: (b, 0, s, 0))
    heads_t = lambda n, w: pl.BlockSpec((None, n, w, tile), lambda b, s: (b, 0, 0, s))
    per_batch = pl.BlockSpec((None, 1, d), lambda b, s: (b, 0, 0))
    in_specs = [row(d), per_batch, per_batch, _resident((1, d))]
    in_specs += [_resident(w.shape) for w in weights]
    bsd = lambda w, dt: jax.ShapeDtypeStruct((bsz, seq, w), dt)
    out_shape = [
        jax.ShapeDtypeStruct((bsz, N_HEADS, HEAD_DIM, seq), BF16),
        jax.ShapeDtypeStruct((bsz, IDX_HEADS, IDX_DIM, seq), BF16),
        jax.ShapeDtypeStruct((bsz, N_KV_HEADS, HEAD_DIM, seq), BF16),
        jax.ShapeDtypeStruct((bsz, IDX_HEADS, seq), F32),
        jax.ShapeDtypeStruct((bsz, N_KV_HEADS, seq, HEAD_DIM), BF16),
        bsd(IDX_DIM, BF16),
        bsd(d, BF16), bsd(d, BF16), bsd(d, BF16), bsd(d, BF16)]
    out_specs = [heads_t(N_HEADS, HEAD_DIM), heads_t(IDX_HEADS, IDX_DIM), heads_t(N_KV_HEADS, HEAD_DIM),
                 pl.BlockSpec((None, IDX_HEADS, tile), lambda b, s: (b, 0, s)),
                 heads(N_KV_HEADS, HEAD_DIM), row(IDX_DIM),
                 row(d), row(d), row(d), row(d)]
    return pl.pallas_call(
        _in_proj_kernel,
        grid=(bsz, seq // tile),
        in_specs=in_specs, out_specs=out_specs, out_shape=out_shape,
        compiler_params=_cparams(2),
        name="in_proj",
    )(x, sc, sh, g, *weights)


def _t5_bucket_np(rel):
    max_exact = N_BUCKETS // 2
    n = np.maximum(rel, 0)
    nf = np.maximum(n, 1).astype(np.float64)
    large = max_exact + (np.log(nf / max_exact) / math.log(MAX_DISTANCE / max_exact)
                         * (N_BUCKETS - max_exact)).astype(np.int32)
    large = np.minimum(large, N_BUCKETS - 1)
    return np.where(n < max_exact, n, large).astype(np.int32)


def _far_distance():
    d = np.arange(0, 4 * MAX_DISTANCE)
    b = _t5_bucket_np(d)
    assert b[-1] == N_BUCKETS - 1
    return int(np.max(np.nonzero(b != N_BUCKETS - 1)[0])) + 1


def _sort_key(s):
    bits = lax.bitcast_convert_type(s, jnp.int32)
    return bits ^ ((bits >> 31) & jnp.int32(0x7FFFFFFF))


def _key_to_float(k):
    return lax.bitcast_convert_type(k ^ ((k >> 31) & jnp.int32(0x7FFFFFFF)), F32)


VALUE_STEPS = 12
SEARCH_FIXED = 16
SEARCH_UNROLL = 4
SCORE_ROWS = 64


def _attn_kernel(qt_ref, qit_ref, wit_ref, k_ref, vt_ref, ki_ref, bucket_ref, relb_ref, tri_ref,
                 o_ref, key_ref, madd_ref, bias_ref, m_ref, acc_ref, ot_ref, da_ref, db_ref, xa_ref, xb_ref, p_ref,
                 stat_ref,
                 *, tq, n_sel, n_near):
    b = pl.program_id(0)
    qt = pl.program_id(1)
    n_chunks = qt + 1
    tk = tq
    int_max = jnp.iinfo(jnp.int32).max

    @pl.when((b == 0) & (qt == 0))
    def _():
        for dlt in range(n_near + 1):
            bucket = bucket_ref[dlt]
            for h in range(N_HEADS):
                tile = jnp.full((tk, tq), relb_ref[N_BUCKETS - 1, h], F32)
                for bk in range(N_BUCKETS - 1):
                    tile = jnp.where(bucket == bk, relb_ref[bk, h], tile)
                bias_ref[dlt * N_HEADS + h] = tile

    t_row = qt * tq + lax.broadcasted_iota(jnp.int32, (1, tq), 1)

    def split(v):
        return v.reshape(v.shape[0] // SUBLANES, SUBLANES, v.shape[1])

    n_rb = tk // SCORE_ROWS
    heads_per_rb = IDX_HEADS // n_rb
    assert n_rb * SCORE_ROWS == tk and heads_per_rb * n_rb == IDX_HEADS
    key_pos = lax.broadcasted_iota(jnp.int32, (SCORE_ROWS, tq), 0)
    qry_pos = lax.broadcasted_iota(jnp.int32, (SCORE_ROWS, tq), 1)

    def idx_dots(c, d_ref, heads):
        kic = ki_ref[pl.ds(pl.multiple_of(c * tk, tk), tk), :]
        for h in heads:
            d_ref[h] = _dot(kic, qit_ref[h])

    def score_chunk(c, d_ref, nxt, dn_ref):
        off = pl.multiple_of(c * tk, tk)
        kmin, kmax, npos, nzero = stat_ref[0], stat_ref[1], stat_ref[2], stat_ref[3]
        for rb in range(n_rb):
            idx_dots(nxt, dn_ref, range(rb * heads_per_rb, (rb + 1) * heads_per_rb))
            rows = slice(rb * SCORE_ROWS, (rb + 1) * SCORE_ROWS)
            s = None
            for h in range(IDX_HEADS):
                term = wit_ref[h:h + 1, :] * jnp.maximum(d_ref[h, rows, :], 0.0)
                s = term if s is None else s + term
            valid = (c < qt) | (key_pos + rb * SCORE_ROWS <= qry_pos)
            s = jnp.where(valid, s, -jnp.inf)
            is_zero = s == 0.0
            key = jnp.where(is_zero, 0, _sort_key(s))
            key_ref[pl.ds(pl.multiple_of(off + rb * SCORE_ROWS, SCORE_ROWS), SCORE_ROWS), :] = key
            kmin = jnp.minimum(kmin, jnp.min(split(jnp.where(valid, key, int_max)), axis=0))
            kmax = jnp.maximum(kmax, jnp.max(split(key), axis=0))
            npos = npos + jnp.sum(split(jnp.where(s > 0.0, 1, 0)), axis=0)
            nzero = nzero + jnp.sum(split(jnp.where(is_zero, 1, 0)), axis=0)
        stat_ref[0], stat_ref[1], stat_ref[2], stat_ref[3] = kmin, kmax, npos, nzero

    stat_ref[0] = jnp.full((SUBLANES, tq), int_max, jnp.int32)
    stat_ref[1] = jnp.full((SUBLANES, tq), KEY_NEG_INF, jnp.int32)
    stat_ref[2] = jnp.zeros((SUBLANES, tq), jnp.int32)
    stat_ref[3] = jnp.zeros((SUBLANES, tq), jnp.int32)
    idx_dots(0, da_ref, range(IDX_HEADS))

    def score_step(c, carry):
        nxt = jnp.minimum(c + 1, qt)

        @pl.when(c % 2 == 0)
        def _():
            score_chunk(c, da_ref, nxt, db_ref)

        @pl.when(c % 2 == 1)
        def _():
            score_chunk(c, db_ref, nxt, da_ref)
        return carry

    lax.fori_loop(0, n_chunks, score_step, 0)
    kmin, kmax, npos, nzero = stat_ref[0], stat_ref[1], stat_ref[2], stat_ref[3]
    kmin = jnp.min(kmin, axis=0, keepdims=True)
    kmax = jnp.max(kmax, axis=0, keepdims=True)
    npos = jnp.sum(npos, axis=0, keepdims=True)
    nneg0 = npos + jnp.sum(nzero, axis=0, keepdims=True)

    need = jnp.minimum(t_row + 1, n_sel)

    def count_ge(mid):
        mid8 = jnp.broadcast_to(mid, (SUBLANES, tq))
        n_acc = tk // SCORE_ROWS

        def body(c, acc):
            off = pl.multiple_of(c * tk, tk)
            new = []
            for i in range(n_acc):
                rows = pl.ds(pl.multiple_of(off + i * SCORE_ROWS, SCORE_ROWS), SCORE_ROWS)
                new.append(acc[i] + jnp.sum(jnp.where(split(key_ref[rows, :]) >= mid8[None], 1, 0), axis=0))
            return tuple(new)
        acc = lax.fori_loop(0, n_chunks, body,
                            tuple(jnp.zeros((SUBLANES, tq), jnp.int32) for _ in range(n_acc)))
        return jnp.sum(sum(acc), axis=0, keepdims=True)

    def search_cond(st):
        return jnp.min(st[-1]) == 0

    def search_steps(st):
        for _ in range(SEARCH_UNROLL):
            st = search_step(st)
        return st

    def search_step(st):
        it, lo, hi, cnt_lo, cnt_hi, done = st
        half = (lo >> 1) + (hi >> 1) + (lo & hi & 1)
        mid_f = 0.5 * _key_to_float(lo) + 0.5 * _key_to_float(hi)
        by_value = jnp.minimum(jnp.maximum(_sort_key(mid_f), lo + 1), hi - 1)
        mid = jnp.where(it < VALUE_STEPS, by_value, half)
        cnt = count_ge(mid)
        live = done == 0
        up = live & (cnt >= need)
        dn = live & (cnt < need)
        lo = jnp.where(up, mid, lo)
        cnt_lo = jnp.where(up, cnt, cnt_lo)
        hi = jnp.where(dn, mid, hi)
        cnt_hi = jnp.where(dn, cnt, cnt_hi)
        done = jnp.where(stops(lo, hi, cnt_lo, cnt_hi), 1, done)
        return it + 1, lo, hi, cnt_lo, cnt_hi, done

    def stops(lo, hi, cnt_lo, cnt_hi):
        return (cnt_lo == need) | (hi == lo + 1) | (cnt_lo - cnt_hi <= 2)

    above = need <= npos
    below = need > nneg0
    lo0 = jnp.where(above, 1, jnp.where(below, kmin, 0))
    cnt_lo0 = jnp.where(above, npos, jnp.where(below, t_row + 1, nneg0))
    hi0 = jnp.where(above, jnp.minimum(kmax, int_max - 1) + 1, jnp.where(below, 0, 1))
    cnt_hi0 = jnp.where(above, 0, jnp.where(below, nneg0, npos))
    done0 = jnp.where(stops(lo0, hi0, cnt_lo0, cnt_hi0), 1, 0)
    searching = (qt + 1) * tq > n_sel

    state = lax.fori_loop(0, jnp.where(searching, SEARCH_FIXED, 0), lambda _, st: search_step(st),
                          (jnp.int32(0), lo0, hi0, cnt_lo0, cnt_hi0, done0))
    _, lo, hi, cnt_lo, cnt_hi, _ = lax.while_loop(search_cond, search_steps, state)

    pending = (cnt_lo > need) & (hi > lo + 1)
    stat_ref[0] = jnp.broadcast_to(lo, (SUBLANES, tq))
    stat_ref[1] = jnp.broadcast_to(cnt_lo, (SUBLANES, tq))

    @pl.when(searching)
    def _():
        lo8 = jnp.broadcast_to(lo, (SUBLANES, tq))
        hi8 = jnp.broadcast_to(hi, (SUBLANES, tq))

        def body(c, carry):
            big, small = carry
            kc = split(key_ref[pl.ds(pl.multiple_of(c * tk, tk), tk), :])
            big = jnp.maximum(big, jnp.max(jnp.where(kc < hi8[None], kc, KEY_NEG_INF), axis=0))
            small = jnp.minimum(small, jnp.min(jnp.where(kc >= lo8[None], kc, int_max), axis=0))
            return big, small
        big, small = lax.fori_loop(
            0, n_chunks, body,
            (jnp.full((SUBLANES, tq), KEY_NEG_INF, jnp.int32), jnp.full((SUBLANES, tq), int_max, jnp.int32)))
        big = jnp.max(big, axis=0, keepdims=True)
        small = jnp.min(small, axis=0, keepdims=True)
        stat_ref[0] = jnp.broadcast_to(jnp.where(pending, big, lo), (SUBLANES, tq))
        stat_ref[1] = jnp.broadcast_to(
            jnp.where(pending, cnt_hi + jnp.where(big == small, 2, 1), cnt_lo), (SUBLANES, tq))

    lo = stat_ref[0][0:1]
    cnt_lo = stat_ref[1][0:1]

    has_tie = jnp.max(jnp.where(cnt_lo > need, 1, 0)) > 0

    @pl.when(jnp.logical_not(has_tie))
    def _():
        def body(c, carry):
            off = pl.multiple_of(c * tk, tk)
            madd_ref[pl.ds(off, tk), :] = jnp.where(key_ref[pl.ds(off, tk), :] >= lo, 0.0, MASK_NEG)
            return carry
        lax.fori_loop(0, n_chunks, body, 0)

    @pl.when(has_tie)
    def _():
        tri = tri_ref[...]
        quota = jnp.where(cnt_lo > need, need - cnt_hi, n_sel).astype(F32)

        def body(c, seen):
            off = pl.multiple_of(c * tk, tk)
            kc = key_ref[pl.ds(off, tk), :]
            eq = kc == lo
            eqf = jnp.where(eq, 1.0, 0.0)
            rank = _dot(tri, eqf.astype(BF16)) + seen
            sel = (kc > lo) | (eq & (rank <= quota))
            madd_ref[pl.ds(off, tk), :] = jnp.where(sel, 0.0, MASK_NEG)
            return seen + jnp.sum(eqf, axis=0, keepdims=True)
        lax.fori_loop(0, n_chunks, body, jnp.zeros((1, tq), F32))

    m_ref[...] = jnp.full(m_ref.shape, MASK_NEG, F32)
    acc_ref[...] = jnp.zeros(acc_ref.shape, F32)
    group = N_HEADS // N_KV_HEADS
    ones_rows = jnp.ones((2 * SUBLANES, tk), BF16)

    def masked_logits(c, h):
        off = pl.multiple_of(c * tk, tk)
        near = jnp.minimum(qt - c, n_near) * N_HEADS
        return (_dot(k_ref[h // group, pl.ds(off, tk), :], qt_ref[h])
                + madd_ref[pl.ds(off, tk), :] + bias_ref[near + h]).astype(BF16)

    def logits(c, x_ref):
        for h in range(N_HEADS):
            x_ref[h] = masked_logits(c, h)

    def softmax_pv(c, x_ref, nxt, xn_ref):
        off = pl.multiple_of(c * tk, tk)
        vt_ext = [jnp.concatenate([vt_ref[kvh, :, pl.ds(off, tk)], ones_rows], axis=0)
                  for kvh in range(N_KV_HEADS)]

        def pv(h, alpha):
            prod = _dot(vt_ext[h // group], p_ref[h])[:HEAD_DIM + SUBLANES]
            acc_ref[h] = (alpha[None] * split(acc_ref[h])).reshape(HEAD_DIM + SUBLANES, tq) + prod

        prev_alpha = None
        for h in range(N_HEADS):
            xn_ref[h] = masked_logits(nxt, h)
            packed = 2 * SUBLANES
            mx = jnp.max(x_ref[h].reshape(tk // packed, packed, tq), axis=0).astype(F32)
            mx = jnp.maximum(mx[:SUBLANES], mx[SUBLANES:])
            for shift in (4, 2, 1):
                mx = jnp.maximum(mx, pltpu.roll(mx, shift, 0))
            m_old = m_ref[h]
            m_new = jnp.maximum(m_old, mx)
            alpha = jnp.exp(m_old - m_new)
            m_ref[h] = m_new
            p_ref[h] = jnp.exp(x_ref[h] - m_new[0:1].astype(BF16))
            if h > 0:
                pv(h - 1, prev_alpha)
            prev_alpha = alpha
        pv(N_HEADS - 1, prev_alpha)

    logits(0, xa_ref)

    def attend(c, carry):
        nxt = jnp.minimum(c + 1, qt)

        @pl.when(c % 2 == 0)
        def _():
            softmax_pv(c, xa_ref, nxt, xb_ref)

        @pl.when(c % 2 == 1)
        def _():
            softmax_pv(c, xb_ref, nxt, xa_ref)
        return carry

    lax.fori_loop(0, n_chunks, attend, 0)

    for h in range(N_HEADS):
        ot_ref[h * HEAD_DIM:(h + 1) * HEAD_DIM, :] = (
            acc_ref[h, 0:HEAD_DIM, :] / acc_ref[h, HEAD_DIM:HEAD_DIM + 1, :])
    o_ref[...] = ot_ref[...].T.astype(o_ref.dtype)


def _attention(qt, qit, wit, k, vt, ki, rel_bias, tq):
    bsz, _, _, seq = qt.shape
    n_sel = min(TOPK_MAX, seq // 4)
    far = _far_distance()
    n_near = min(seq // tq, (far - 1 + tq - 1) // tq + 1)
    i = np.arange(tq)[:, None]
    j = np.arange(tq)[None, :]
    bucket = np.stack([_t5_bucket_np(dlt * tq + j - i) for dlt in range(n_near)]
                      + [np.full((tq, tq), N_BUCKETS - 1, np.int32)])
    tri = (j <= i).astype(np.float32)

    kernel = functools.partial(_attn_kernel, tq=tq, n_sel=n_sel, n_near=n_near)
    qblk = lambda n, w: pl.BlockSpec((None, n, w, tq), lambda b, t: (b, 0, 0, t))
    return pl.pallas_call(
        kernel,
        grid=(bsz, seq // tq),
        in_specs=[qblk(N_HEADS, HEAD_DIM), qblk(IDX_HEADS, IDX_DIM),
                  pl.BlockSpec((None, IDX_HEADS, tq), lambda b, t: (b, 0, t)),
                  pl.BlockSpec((None, N_KV_HEADS, seq, HEAD_DIM), lambda b, t: (b, 0, 0, 0)),
                  pl.BlockSpec((None, N_KV_HEADS, HEAD_DIM, seq), lambda b, t: (b, 0, 0, 0)),
                  pl.BlockSpec((None, seq, IDX_DIM), lambda b, t: (b, 0, 0)),
                  _resident((n_near + 1, tq, tq)),
                  pl.BlockSpec(memory_space=pltpu.SMEM),
                  _resident((tq, tq))],
        out_specs=pl.BlockSpec((None, tq, N_HEADS * HEAD_DIM), lambda b, t: (b, t, 0)),
        out_shape=jax.ShapeDtypeStruct((bsz, seq, N_HEADS * HEAD_DIM), BF16),
        scratch_shapes=[pltpu.VMEM((seq, tq), jnp.int32),
                        pltpu.VMEM((seq, tq), F32),
                        pltpu.VMEM(((n_near + 1) * N_HEADS, tq, tq), F32),
                        pltpu.VMEM((N_HEADS, SUBLANES, tq), F32),
                        pltpu.VMEM((N_HEADS, HEAD_DIM + SUBLANES, tq), F32),
                        pltpu.VMEM((N_HEADS * HEAD_DIM, tq), F32),
                        pltpu.VMEM((IDX_HEADS, tq, tq), F32),
                        pltpu.VMEM((IDX_HEADS, tq, tq), F32),
                        pltpu.VMEM((N_HEADS, tq, tq), BF16),
                        pltpu.VMEM((N_HEADS, tq, tq), BF16),
                        pltpu.VMEM((N_HEADS, tq, tq), BF16),
                        pltpu.VMEM((4, SUBLANES, tq), jnp.int32)],
        compiler_params=_cparams(2),
        name="attention",
    )(qt, qit, wit, k, vt, ki, jnp.asarray(bucket), rel_bias, jnp.asarray(tri, BF16))


N_SEG = SUBLANES


def _seg_pitch(seg_len):
    quarter = seg_len // 4 + 1
    if quarter % 2 == 0:
        quarter += 1
    return 4 * quarter


def _rglru_kernel(xr_ref, gy_ref, cw_ref, cb_ref, wrg_ref, brg_ref, wig_ref, big_ref, lam_ref,
                  o_ref, xp_ref, a_ref, u_ref, *, seq, n_grp):
    seg = seq // N_SEG
    pitch = _seg_pitch(seg)
    pad = SUBLANES

    xp_ref[0:pad, :] = jnp.zeros((pad, LANES), F32)
    for g in range(n_grp):
        ls = slice(g * LANES, (g + 1) * LANES)
        xp_ref[pad:pad + seq, :] = xr_ref[:, ls].astype(F32)
        cw = cw_ref[:, ls]
        xc = cb_ref[:, ls] + cw[0:1] * xp_ref[pl.ds(pad - 3, seq), :]
        for kk in range(1, RNN_CONV):
            xc = xc + cw[kk:kk + 1] * xp_ref[pl.ds(pad - 3 + kk, seq), :]
        xb = xc.astype(BF16)
        r = jax.nn.sigmoid(_dot(xb, wrg_ref[g]) + brg_ref[:, ls])
        ig = jax.nn.sigmoid(_dot(xb, wig_ref[g]) + big_ref[:, ls])
        z = -lam_ref[:, ls]
        softplus = jnp.maximum(z, 0.0) + jnp.log1p(jnp.exp(-jnp.abs(z)))
        log_a = (-LRU_C * softplus) * r
        a = jnp.exp(log_a)
        gain2 = -jnp.tanh(log_a) * (1.0 + a * a)
        u = xc * ig * jnp.where(gain2 > 0.0, gain2 * lax.rsqrt(gain2), 0.0)
        for j in range(N_SEG):
            a_ref[g, j * pitch:j * pitch + seg, :] = a[j * seg:(j + 1) * seg]
            u_ref[g, j * pitch:j * pitch + seg, :] = u[j * seg:(j + 1) * seg]

    def step(t, st):
        new = []
        for g in range(n_grp):
            h, p = st[g]
            a_t = a_ref[g, pl.ds(t, N_SEG, stride=pitch), :]
            u_t = u_ref[g, pl.ds(t, N_SEG, stride=pitch), :]
            h = a_t * h + u_t
            p = p * a_t
            u_ref[g, pl.ds(t, N_SEG, stride=pitch), :] = h
            a_ref[g, pl.ds(t, N_SEG, stride=pitch), :] = p
            new.append((h, p))
        return tuple(new)

    init = tuple((jnp.zeros((N_SEG, LANES), F32), jnp.ones((N_SEG, LANES), F32)) for _ in range(n_grp))
    final = lax.fori_loop(0, seg, step, init)

    for g in range(n_grp):
        ls = slice(g * LANES, (g + 1) * LANES)
        h_end, p_end = final[g]
        carry = jnp.zeros((1, LANES), F32)
        for j in range(N_SEG):
            rows = slice(j * pitch, j * pitch + seg)
            hj = u_ref[g, rows, :] + a_ref[g, rows, :] * carry
            o_ref[j * seg:(j + 1) * seg, ls] = (hj * gy_ref[j * seg:(j + 1) * seg, ls].astype(F32)).astype(o_ref.dtype)
            carry = h_end[j:j + 1] + p_end[j:j + 1] * carry


def _rglru(xr, gy, cw, cb, wrg, brg, wig, big, lam, cblk):
    bsz, seq, d = xr.shape
    n_grp = cblk // LANES
    pitch = _seg_pitch(seq // N_SEG)
    kernel = functools.partial(_rglru_kernel, seq=seq, n_grp=n_grp)
    act = pl.BlockSpec((None, seq, cblk), lambda b, c: (b, 0, c))
    vec = lambda r: pl.BlockSpec((r, cblk), lambda b, c: (0, c))
    gate_w = pl.BlockSpec((n_grp, LANES, LANES), lambda b, c: (c, 0, 0))
    return pl.pallas_call(
        kernel,
        grid=(bsz, d // cblk),
        in_specs=[act, act, vec(RNN_CONV), vec(1), gate_w, vec(1), gate_w, vec(1), vec(1)],
        out_specs=act,
        out_shape=jax.ShapeDtypeStruct((bsz, seq, d), BF16),
        scratch_shapes=[pltpu.VMEM((seq + SUBLANES, LANES), F32),
                        pltpu.VMEM((n_grp, N_SEG * pitch, LANES), F32),
                        pltpu.VMEM((n_grp, N_SEG * pitch, LANES), F32)],
        compiler_params=_cparams(2),
        name="rglru",
    )(xr, gy, cw, cb, wrg, brg, wig, big, lam)


def _merge_ffn_kernel(x_ref, attn_ref, rnn_ref, sga_ref, sgb_ref, ga1_ref, woa_ref, wor_ref, wout_ref,
                      sc_ref, sh_ref, ga2_ref, g_ref, gf_ref, wv_ref, wg_ref, cwv_ref, cwg_ref,
                      cbv_ref, cbg_ref, wd_ref, o_ref, buf_ref, carry_ref, act_ref, *, tile, fchunk, n_fchunks):
    s = pl.program_id(1)
    pad = SUBLANES

    @pl.when(s == 0)
    def _():
        carry_ref[...] = jnp.zeros(carry_ref.shape, F32)

    merged = (sga_ref[...].astype(F32) * _dot(attn_ref[...], woa_ref[...])
              + sgb_ref[...].astype(F32) * _dot(rnn_ref[...], wor_ref[...]))
    hres = x_ref[...] + ga1_ref[...] * _dot(merged.astype(BF16), wout_ref[...])
    xn = _modulated_norm(hres, g_ref[...], sc_ref[...], sh_ref[...]).astype(BF16)

    def conv(up, idx, cw, cb):
        buf_ref[0:pad, :] = carry_ref[idx]
        buf_ref[pad:pad + tile, :] = up
        carry_ref[idx] = up[tile - pad:tile]
        y = cb + cw[FFN_CONV - 1:FFN_CONV] * up
        for kk in range(FFN_CONV - 1):
            y = y + cw[kk:kk + 1] * buf_ref[pl.ds(pad - (FFN_CONV - 1) + kk, tile), :]
        return y

    for c in range(n_fchunks):
        cs = slice(c * fchunk, (c + 1) * fchunk)
        val = conv(_dot(xn, wv_ref[:, cs]), 2 * c, cwv_ref[:, cs], cbv_ref[:, cs])
        gte = conv(_dot(xn, wg_ref[:, cs]), 2 * c + 1, cwg_ref[:, cs], cbg_ref[:, cs])
        act_ref[:, cs] = ((gte * jax.nn.sigmoid(gte)) * val).astype(BF16)

    h2 = hres + ga2_ref[...] * _dot(act_ref[...], wd_ref[...])
    ms = jnp.mean(h2 * h2, axis=-1, keepdims=True)
    o_ref[...] = h2 * lax.rsqrt(ms + EPS) * gf_ref[...]


def _merge_ffn(x, attn, rnn, sga, sgb, ga1, woa, wor, wout, sc, sh, ga2, g, gf,
               wv, wg, cwv, cwg, cbv, cbg, wd, tile, fchunk):
    bsz, seq, d = x.shape
    dff = wv.shape[1]
    n_fchunks = dff // fchunk
    kernel = functools.partial(_merge_ffn_kernel, tile=tile, fchunk=fchunk, n_fchunks=n_fchunks)
    row = lambda w: pl.BlockSpec((None, tile, w), lambda b, s: (b, s, 0))
    per_batch = pl.BlockSpec((None, 1, d), lambda b, s: (b, 0, 0))
    merge_w = [woa, wor, wout]
    ffn_w = [g, gf, wv, wg, cwv, cwg, cbv, cbg, wd]
    return pl.pallas_call(
        kernel,
        grid=(bsz, seq // tile),
        in_specs=([row(d), row(attn.shape[-1]), row(d), row(d), row(d), per_batch]
                  + [_resident(a.shape) for a in merge_w]
                  + [per_batch, per_batch, per_batch]
                  + [_resident(a.shape) for a in ffn_w]),
        out_specs=row(d),
        out_shape=jax.ShapeDtypeStruct((bsz, seq, d), F32),
        scratch_shapes=[pltpu.VMEM((tile + SUBLANES, fchunk), F32),
                        pltpu.VMEM((2 * n_fchunks, SUBLANES, fchunk), F32),
                        pltpu.VMEM((tile, dff), BF16)],
        compiler_params=_cparams(2),
        name="merge_ffn",
    )(x, attn, rnn, sga, sgb, ga1, *merge_w, sc, sh, ga2, *ffn_w)


def _pick(seq, pref):
    t = min(seq, pref)
    assert seq % t == 0
    return t


def kernel(x, c, w_ada, b_ada, g_mix, w_in, b_in, rel_bias, conv_rnn_w, conv_rnn_b, w_rg, b_rg, w_ig, b_ig, lru_lambda, w_o_attn, w_o_rnn, w_out, g_ffn, w_up, conv_ffn_w, conv_ffn_b, w_down, g_final):
    bsz, seq, d = x.shape
    depth = w_ada.shape[0]
    assert depth == 1 and d == N_RNN_BLOCKS * LANES
    dff = w_down.shape[1]
    row_tile = _pick(seq, 512)
    tq = _pick(seq, 256)

    widths = (N_HEADS * HEAD_DIM, N_KV_HEADS * HEAD_DIM, N_KV_HEADS * HEAD_DIM, IDX_HEADS * IDX_DIM,
              IDX_DIM, IDX_HEADS, d, d, d, d)
    cuts = np.cumsum((0,) + widths)
    col = lambda a, i0, i1: a[..., cuts[i0]:cuts[i1]]

    h = x
    for l in range(depth):
        mod = _adaln(c, w_ada[l], b_ada[l])
        sh1, sc1, ga1, sh2, sc2, ga2 = [m[:, None, :] for m in jnp.split(mod, 6, axis=-1)]

        w, bias = w_in[l], b_in[l]
        t_rows = [col(w, 0, 1), col(w, 3, 4), col(w, 2, 3), col(w, 5, 6)]
        t_bias = [col(bias, 0, 1), col(bias, 3, 4), col(bias, 2, 3), col(bias, 5, 6)]
        n_t = sum(r.shape[1] for r in t_rows)
        t_pad = (-n_t) % (2 * SUBLANES)
        wt = jnp.pad(jnp.concatenate(t_rows, axis=1).T, ((0, t_pad), (0, 0))).astype(BF16)
        bt = jnp.pad(jnp.concatenate(t_bias), (0, t_pad)).reshape(-1, 1)
        k_pad = (-(widths[1] + widths[4])) % LANES
        wk = jnp.pad(jnp.concatenate([col(w, 1, 2), col(w, 4, 5)], axis=1), ((0, 0), (0, k_pad))).astype(BF16)
        bk = jnp.pad(jnp.concatenate([col(bias, 1, 2), col(bias, 4, 5)]), (0, k_pad)).reshape(1, -1)
        weights = [wt, bt, wk, bk]
        for i in range(6, 10):
            weights += [col(w, i, i + 1).astype(BF16), col(bias, i, i + 1).reshape(1, -1)]
        qt, qit, vt, wit, k, ki, xr, gy, sga, sgb = _in_proj(h, sc1, sh1, g_mix[l].reshape(1, d), weights, row_tile)

        attn = _attention(qt, qit, wit, k, vt, ki, rel_bias, tq)
        rnn = _rglru(xr, gy, conv_rnn_w[l], conv_rnn_b[l].reshape(1, d), w_rg[l].astype(BF16),
                     b_rg[l].reshape(1, d), w_ig[l].astype(BF16), b_ig[l].reshape(1, d),
                     lru_lambda[l].reshape(1, d), cblk=min(d, 512))
        wu, cw, cb = w_up[l], conv_ffn_w[l], conv_ffn_b[l].reshape(1, -1)
        h = _merge_ffn(h, attn, rnn, sga, sgb, ga1, w_o_attn[l].astype(BF16), w_o_rnn[l].astype(BF16),
                       w_out[l].astype(BF16), sc2, sh2, ga2, g_ffn[l].reshape(1, d), g_final.reshape(1, d),
                       wu[:, :dff].astype(BF16), wu[:, dff:].astype(BF16), cw[:, :dff], cw[:, dff:],
                       cb[:, :dff], cb[:, dff:], w_down[l].astype(BF16), row_tile, fchunk=256)
    return h
```

```python
import functools
import math

import numpy as np
import jax
import jax.numpy as jnp
from jax import lax
from jax.experimental import pallas as pl
from jax.experimental.pallas import tpu as pltpu

N_HEADS = 8
HEAD_DIM = 64
N_KV_HEADS = 2
IDX_HEADS = 8
IDX_DIM = 64
TOPK_MAX = 256
N_BUCKETS = 32
MAX_DISTANCE = 128
N_RNN_BLOCKS = 8
RNN_CONV = 4
LRU_C = 8.0
FFN_CONV = 3
EPS = 1e-6

LANES = 128
SUBLANES = 8
VMEM_LIMIT = 56 * 1024 * 1024

MASK_NEG = -(2.0 ** 100)
KEY_NEG_INF = -2139095041

BF16 = jnp.bfloat16
F32 = jnp.float32


def _cparams(n_grid):
    return pltpu.CompilerParams(
        dimension_semantics=("arbitrary",) * n_grid, vmem_limit_bytes=VMEM_LIMIT)


def _resident(shape):
    nd = len(shape)
    return pl.BlockSpec(shape, lambda *_: (0,) * nd, pipeline_mode=pl.Buffered(1))


def _dot(a, b):
    return jnp.dot(a, b, preferred_element_type=F32)


def _dot_nt(a, b):
    return lax.dot_general(a, b, (((1,), (1,)), ((), ())), preferred_element_type=F32)


def _adaln_kernel(c_ref, w_ref, b_ref, o_ref):
    c = c_ref[...]
    c_act = (c * jax.nn.sigmoid(c)).astype(BF16)
    o_ref[...] = _dot(c_act, w_ref[...].astype(BF16)) + b_ref[...]


def _adaln(c, w, b):
    bsz, d = c.shape
    n = w.shape[1]
    return pl.pallas_call(
        _adaln_kernel,
        grid=(n // d,),
        in_specs=[pl.BlockSpec((bsz, d), lambda j: (0, 0)),
                  pl.BlockSpec((d, d), lambda j: (0, j)),
                  pl.BlockSpec((1, d), lambda j: (0, j))],
        out_specs=pl.BlockSpec((bsz, d), lambda j: (0, j)),
        out_shape=jax.ShapeDtypeStruct((bsz, n), F32),
        compiler_params=_cparams(1),
        name="adaln",
    )(c, w, b.reshape(1, n))


def _modulated_norm(x, g, sc, sh):
    ms = jnp.mean(x * x, axis=-1, keepdims=True)
    return (x * lax.rsqrt(ms + EPS) * g) * (1.0 + sc) + sh


def _in_proj_kernel(x_ref, sc_ref, sh_ref, g_ref, wt_ref, bt_ref, wk_ref, bk_ref,
                    wr_ref, br_ref, wy_ref, by_ref, wga_ref, bga_ref, wgb_ref, bgb_ref,
                    qt_ref, qit_ref, vt_ref, wit_ref, k_ref, ki_ref, xr_ref, gy_ref, sga_ref, sgb_ref):
    xn = _modulated_norm(x_ref[...], g_ref[...], sc_ref[...], sh_ref[...]).astype(BF16)

    res = _dot_nt(wt_ref[...], xn) + bt_ref[...]
    r0 = 0
    for h in range(N_HEADS):
        qt_ref[h] = (res[r0 + h * HEAD_DIM:r0 + (h + 1) * HEAD_DIM] * (HEAD_DIM ** -0.5)).astype(BF16)
    r0 += N_HEADS * HEAD_DIM
    for h in range(IDX_HEADS):
        qit_ref[h] = (res[r0 + h * IDX_DIM:r0 + (h + 1) * IDX_DIM] * (IDX_DIM ** -0.5)).astype(BF16)
    r0 += IDX_HEADS * IDX_DIM
    for j in range(N_KV_HEADS):
        vt_ref[j] = res[r0 + j * HEAD_DIM:r0 + (j + 1) * HEAD_DIM].astype(BF16)
    r0 += N_KV_HEADS * HEAD_DIM
    wit_ref[...] = res[r0:r0 + IDX_HEADS] * (IDX_HEADS ** -0.5)

    res = _dot(xn, wk_ref[...]) + bk_ref[...]
    for j in range(N_KV_HEADS):
        k_ref[j] = res[:, j * HEAD_DIM:(j + 1) * HEAD_DIM].astype(BF16)
    ki0 = N_KV_HEADS * HEAD_DIM
    ki_ref[...] = res[:, ki0:ki0 + IDX_DIM].astype(BF16)

    xr_ref[...] = (_dot(xn, wr_ref[...]) + br_ref[...]).astype(BF16)
    gy_ref[...] = jax.nn.gelu(_dot(xn, wy_ref[...]) + by_ref[...]).astype(BF16)
    sga_ref[...] = jax.nn.sigmoid(_dot(xn, wga_ref[...]) + bga_ref[...]).astype(BF16)
    sgb_ref[...] = jax.nn.sigmoid(_dot(xn, wgb_ref[...]) + bgb_ref[...]).astype(BF16)


def _in_proj(x, sc, sh, g, weights, tile):
    bsz, seq, d = x.shape
    row = lambda w: pl.BlockSpec((None, tile, w), lambda b, s: (b, s, 0))
    heads = lambda n, w: pl.BlockSpec((None, n, tile, w), lambda b, s: (b, 0, s, 0))
    heads_t = lambda n, w: pl.BlockSpec((None, n, w, tile), lambda b, s: (b, 0, 0, s))
    per_batch = pl.BlockSpec((None, 1, d), lambda b, s: (b, 0, 0))
    in_specs = [row(d), per_batch, per_batch, _resident((1, d))]
    in_specs += [_resident(w.shape) for w in weights]
    bsd = lambda w, dt: jax.ShapeDtypeStruct((bsz, seq, w), dt)
    out_shape = [
        jax.ShapeDtypeStruct((bsz, N_HEADS, HEAD_DIM, seq), BF16),
        jax.ShapeDtypeStruct((bsz, IDX_HEADS, IDX_DIM, seq), BF16),
        jax.ShapeDtypeStruct((bsz, N_KV_HEADS, HEAD_DIM, seq), BF16),
        jax.ShapeDtypeStruct((bsz, IDX_HEADS, seq), F32),
        jax.ShapeDtypeStruct((bsz, N_KV_HEADS, seq, HEAD_DIM), BF16),
        bsd(IDX_DIM, BF16),
        bsd(d, BF16), bsd(d, BF16), bsd(d, BF16), bsd(d, BF16)]
    out_specs = [heads_t(N_HEADS, HEAD_DIM), heads_t(IDX_HEADS, IDX_DIM), heads_t(N_KV_HEADS, HEAD_DIM),
                 pl.BlockSpec((None, IDX_HEADS, tile), lambda b, s: (b, 0, s)),
                 heads(N_KV_HEADS, HEAD_DIM), row(IDX_DIM),
                 row(d), row(d), row(d), row(d)]
    return pl.pallas_call(
        _in_proj_kernel,
        grid=(bsz, seq // tile),
        in_specs=in_specs, out_specs=out_specs, out_shape=out_shape,
        compiler_params=_cparams(2),
        name="in_proj",
    )(x, sc, sh, g, *weights)


def _t5_bucket_np(rel):
    max_exact = N_BUCKETS // 2
    n = np.maximum(rel, 0)
    nf = np.maximum(n, 1).astype(np.float64)
    large = max_exact + (np.log(nf / max_exact) / math.log(MAX_DISTANCE / max_exact)
                         * (N_BUCKETS - max_exact)).astype(np.int32)
    large = np.minimum(large, N_BUCKETS - 1)
    return np.where(n < max_exact, n, large).astype(np.int32)


def _far_distance():
    d = np.arange(0, 4 * MAX_DISTANCE)
    b = _t5_bucket_np(d)
    assert b[-1] == N_BUCKETS - 1
    return int(np.max(np.nonzero(b != N_BUCKETS - 1)[0])) + 1


def _sort_key(s):
    bits = lax.bitcast_convert_type(s, jnp.int32)
    return bits ^ ((bits >> 31) & jnp.int32(0x7FFFFFFF))


def _key_to_float(k):
    return lax.bitcast_convert_type(k ^ ((k >> 31) & jnp.int32(0x7FFFFFFF)), F32)


VALUE_STEPS = 12
SEARCH_FIXED = 16
SEARCH_UNROLL = 4
SCORE_ROWS = 64
LOGIT_AHEAD = 4
LOGIT_BUFS = 8


def _attn_kernel(qt_ref, qit_ref, wit_ref, k_ref, vt_ref, ki_ref, bucket_ref, relb_ref, tri_ref,
                 o_ref, key_ref, madd_ref, bias_ref, m_ref, acc_ref, ot_ref, da_ref, db_ref, x_ref, p_ref, stat_ref,
                 *, tq, n_sel, n_near):
    b = pl.program_id(0)
    qt = pl.program_id(1)
    n_chunks = qt + 1
    tk = tq
    int_max = jnp.iinfo(jnp.int32).max

    @pl.when((b == 0) & (qt == 0))
    def _():
        for dlt in range(n_near + 1):
            bucket = bucket_ref[dlt]
            for h in range(N_HEADS):
                tile = jnp.full((tk, tq), relb_ref[N_BUCKETS - 1, h], F32)
                for bk in range(N_BUCKETS - 1):
                    tile = jnp.where(bucket == bk, relb_ref[bk, h], tile)
                bias_ref[dlt * N_HEADS + h] = tile

    t_row = qt * tq + lax.broadcasted_iota(jnp.int32, (1, tq), 1)

    def split(v):
        return v.reshape(v.shape[0] // SUBLANES, SUBLANES, v.shape[1])

    n_rb = tk // SCORE_ROWS
    heads_per_rb = IDX_HEADS // n_rb
    assert n_rb * SCORE_ROWS == tk and heads_per_rb * n_rb == IDX_HEADS
    key_pos = lax.broadcasted_iota(jnp.int32, (SCORE_ROWS, tq), 0)
    qry_pos = lax.broadcasted_iota(jnp.int32, (SCORE_ROWS, tq), 1)

    def idx_dots(c, d_ref, heads):
        kic = ki_ref[pl.ds(pl.multiple_of(c * tk, tk), tk), :]
        for h in heads:
            d_ref[h] = _dot(kic, qit_ref[h])

    def score_chunk(c, d_ref, nxt, dn_ref):
        off = pl.multiple_of(c * tk, tk)
        kmin, kmax, npos, nzero = stat_ref[0], stat_ref[1], stat_ref[2], stat_ref[3]
        for rb in range(n_rb):
            idx_dots(nxt, dn_ref, range(rb * heads_per_rb, (rb + 1) * heads_per_rb))
            rows = slice(rb * SCORE_ROWS, (rb + 1) * SCORE_ROWS)
            s = None
            for h in range(IDX_HEADS):
                term = wit_ref[h:h + 1, :] * jnp.maximum(d_ref[h, rows, :], 0.0)
                s = term if s is None else s + term
            valid = (c < qt) | (key_pos + rb * SCORE_ROWS <= qry_pos)
            s = jnp.where(valid, s, -jnp.inf)
            is_zero = s == 0.0
            key = jnp.where(is_zero, 0, _sort_key(s))
            key_ref[pl.ds(pl.multiple_of(off + rb * SCORE_ROWS, SCORE_ROWS), SCORE_ROWS), :] = key
            kmin = jnp.minimum(kmin, jnp.min(split(jnp.where(valid, key, int_max)), axis=0))
            kmax = jnp.maximum(kmax, jnp.max(split(key), axis=0))
            npos = npos + jnp.sum(split(jnp.where(s > 0.0, 1, 0)), axis=0)
            nzero = nzero + jnp.sum(split(jnp.where(is_zero, 1, 0)), axis=0)
        stat_ref[0], stat_ref[1], stat_ref[2], stat_ref[3] = kmin, kmax, npos, nzero

    stat_ref[0] = jnp.full((SUBLANES, tq), int_max, jnp.int32)
    stat_ref[1] = jnp.full((SUBLANES, tq), KEY_NEG_INF, jnp.int32)
    stat_ref[2] = jnp.zeros((SUBLANES, tq), jnp.int32)
    stat_ref[3] = jnp.zeros((SUBLANES, tq), jnp.int32)
    idx_dots(0, da_ref, range(IDX_HEADS))

    def score_step(c, carry):
        nxt = jnp.minimum(c + 1, qt)

        @pl.when(c % 2 == 0)
        def _():
            score_chunk(c, da_ref, nxt, db_ref)

        @pl.when(c % 2 == 1)
        def _():
            score_chunk(c, db_ref, nxt, da_ref)
        return carry

    lax.fori_loop(0, n_chunks, score_step, 0)
    kmin, kmax, npos, nzero = stat_ref[0], stat_ref[1], stat_ref[2], stat_ref[3]
    kmin = jnp.min(kmin, axis=0, keepdims=True)
    kmax = jnp.max(kmax, axis=0, keepdims=True)
    npos = jnp.sum(npos, axis=0, keepdims=True)
    nneg0 = npos + jnp.sum(nzero, axis=0, keepdims=True)

    need = jnp.minimum(t_row + 1, n_sel)

    def count_ge(mid):
        mid8 = jnp.broadcast_to(mid, (SUBLANES, tq))
        n_acc = tk // SCORE_ROWS

        def body(c, acc):
            off = pl.multiple_of(c * tk, tk)
            new = []
            for i in range(n_acc):
                rows = pl.ds(pl.multiple_of(off + i * SCORE_ROWS, SCORE_ROWS), SCORE_ROWS)
                new.append(acc[i] + jnp.sum(jnp.where(split(key_ref[rows, :]) >= mid8[None], 1, 0), axis=0))
            return tuple(new)
        acc = lax.fori_loop(0, n_chunks, body,
                            tuple(jnp.zeros((SUBLANES, tq), jnp.int32) for _ in range(n_acc)))
        return jnp.sum(sum(acc), axis=0, keepdims=True)

    def search_cond(st):
        return jnp.min(st[-1]) == 0

    def search_steps(st):
        for _ in range(SEARCH_UNROLL):
            st = search_step(st)
        return st

    def search_step(st):
        it, lo, hi, cnt_lo, cnt_hi, done = st
        half = (lo >> 1) + (hi >> 1) + (lo & hi & 1)
        mid_f = 0.5 * _key_to_float(lo) + 0.5 * _key_to_float(hi)
        by_value = jnp.minimum(jnp.maximum(_sort_key(mid_f), lo + 1), hi - 1)
        mid = jnp.where(it < VALUE_STEPS, by_value, half)
        cnt = count_ge(mid)
        live = done == 0
        up = live & (cnt >= need)
        dn = live & (cnt < need)
        lo = jnp.where(up, mid, lo)
        cnt_lo = jnp.where(up, cnt, cnt_lo)
        hi = jnp.where(dn, mid, hi)
        cnt_hi = jnp.where(dn, cnt, cnt_hi)
        done = jnp.where(stops(lo, hi, cnt_lo, cnt_hi), 1, done)
        return it + 1, lo, hi, cnt_lo, cnt_hi, done

    def stops(lo, hi, cnt_lo, cnt_hi):
        return (cnt_lo == need) | (hi == lo + 1) | (cnt_lo - cnt_hi <= 2)

    above = need <= npos
    below = need > nneg0
    lo0 = jnp.where(above, 1, jnp.where(below, kmin, 0))
    cnt_lo0 = jnp.where(above, npos, jnp.where(below, t_row + 1, nneg0))
    hi0 = jnp.where(above, jnp.minimum(kmax, int_max - 1) + 1, jnp.where(below, 0, 1))
    cnt_hi0 = jnp.where(above, 0, jnp.where(below, nneg0, npos))
    done0 = jnp.where(stops(lo0, hi0, cnt_lo0, cnt_hi0), 1, 0)
    searching = (qt + 1) * tq > n_sel

    state = lax.fori_loop(0, jnp.where(searching, SEARCH_FIXED, 0), lambda _, st: search_step(st),
                          (jnp.int32(0), lo0, hi0, cnt_lo0, cnt_hi0, done0))
    _, lo, hi, cnt_lo, cnt_hi, _ = lax.while_loop(search_cond, search_steps, state)

    pending = (cnt_lo > need) & (hi > lo + 1)
    stat_ref[0] = jnp.broadcast_to(lo, (SUBLANES, tq))
    stat_ref[1] = jnp.broadcast_to(cnt_lo, (SUBLANES, tq))

    @pl.when(searching)
    def _():
        lo8 = jnp.broadcast_to(lo, (SUBLANES, tq))
        hi8 = jnp.broadcast_to(hi, (SUBLANES, tq))

        def body(c, carry):
            big, small = carry
            kc = split(key_ref[pl.ds(pl.multiple_of(c * tk, tk), tk), :])
            big = jnp.maximum(big, jnp.max(jnp.where(kc < hi8[None], kc, KEY_NEG_INF), axis=0))
            small = jnp.minimum(small, jnp.min(jnp.where(kc >= lo8[None], kc, int_max), axis=0))
            return big, small
        big, small = lax.fori_loop(
            0, n_chunks, body,
            (jnp.full((SUBLANES, tq), KEY_NEG_INF, jnp.int32), jnp.full((SUBLANES, tq), int_max, jnp.int32)))
        big = jnp.max(big, axis=0, keepdims=True)
        small = jnp.min(small, axis=0, keepdims=True)
        stat_ref[0] = jnp.broadcast_to(jnp.where(pending, big, lo), (SUBLANES, tq))
        stat_ref[1] = jnp.broadcast_to(
            jnp.where(pending, cnt_hi + jnp.where(big == small, 2, 1), cnt_lo), (SUBLANES, tq))

    lo = stat_ref[0][0:1]
    cnt_lo = stat_ref[1][0:1]

    has_tie = jnp.max(jnp.where(cnt_lo > need, 1, 0)) > 0

    @pl.when(jnp.logical_not(has_tie))
    def _():
        def body(c, carry):
            off = pl.multiple_of(c * tk, tk)
            madd_ref[pl.ds(off, tk), :] = jnp.where(key_ref[pl.ds(off, tk), :] >= lo, 0.0, MASK_NEG)
            return carry
        lax.fori_loop(0, n_chunks, body, 0)

    @pl.when(has_tie)
    def _():
        tri = tri_ref[...]
        quota = jnp.where(cnt_lo > need, need - cnt_hi, n_sel).astype(F32)

        def body(c, seen):
            off = pl.multiple_of(c * tk, tk)
            kc = key_ref[pl.ds(off, tk), :]
            eq = kc == lo
            eqf = jnp.where(eq, 1.0, 0.0)
            rank = _dot(tri, eqf.astype(BF16)) + seen
            sel = (kc > lo) | (eq & (rank <= quota))
            madd_ref[pl.ds(off, tk), :] = jnp.where(sel, 0.0, MASK_NEG)
            return seen + jnp.sum(eqf, axis=0, keepdims=True)
        lax.fori_loop(0, n_chunks, body, jnp.zeros((1, tq), F32))

    m_ref[...] = jnp.full(m_ref.shape, MASK_NEG, F32)
    acc_ref[...] = jnp.zeros(acc_ref.shape, F32)
    group = N_HEADS // N_KV_HEADS
    ones_rows = jnp.ones((2 * SUBLANES, tk), BF16)

    def masked_logits(c, h):
        off = pl.multiple_of(c * tk, tk)
        near = jnp.minimum(qt - c, n_near) * N_HEADS
        return (_dot(k_ref[h // group, pl.ds(off, tk), :], qt_ref[h])
                + madd_ref[pl.ds(off, tk), :] + bias_ref[near + h]).astype(BF16)

    def softmax_pv(c, nxt):
        off = pl.multiple_of(c * tk, tk)
        vt_ext = [jnp.concatenate([vt_ref[kvh, :, pl.ds(off, tk)], ones_rows], axis=0)
                  for kvh in range(N_KV_HEADS)]

        def pv(h, alpha):
            prod = _dot(vt_ext[h // group], p_ref[h])[:HEAD_DIM + SUBLANES]
            acc_ref[h] = (alpha[None] * split(acc_ref[h])).reshape(HEAD_DIM + SUBLANES, tq) + prod

        prev_alpha = None
        for h in range(N_HEADS):
            ahead = h + LOGIT_AHEAD
            x_ref[ahead % LOGIT_BUFS] = (masked_logits(c, ahead) if ahead < N_HEADS
                                         else masked_logits(nxt, ahead - N_HEADS))
            xh = x_ref[h % LOGIT_BUFS]
            packed = 2 * SUBLANES
            mx = jnp.max(xh.reshape(tk // packed, packed, tq), axis=0).astype(F32)
            mx = jnp.maximum(mx[:SUBLANES], mx[SUBLANES:])
            for shift in (4, 2, 1):
                mx = jnp.maximum(mx, pltpu.roll(mx, shift, 0))
            m_old = m_ref[h]
            m_new = jnp.maximum(m_old, mx)
            alpha = jnp.exp(m_old - m_new)
            m_ref[h] = m_new
            p_ref[h] = jnp.exp(x_ref[h % LOGIT_BUFS] - m_new[0:1].astype(BF16))
            if h > 0:
                pv(h - 1, prev_alpha)
            prev_alpha = alpha
        pv(N_HEADS - 1, prev_alpha)

    for h in range(LOGIT_AHEAD):
        x_ref[h] = masked_logits(0, h)

    def attend(c, carry):
        softmax_pv(c, jnp.minimum(c + 1, qt))
        return carry

    lax.fori_loop(0, n_chunks, attend, 0)

    for h in range(N_HEADS):
        ot_ref[h * HEAD_DIM:(h + 1) * HEAD_DIM, :] = (
            acc_ref[h, 0:HEAD_DIM, :] / acc_ref[h, HEAD_DIM:HEAD_DIM + 1, :])
    o_ref[...] = ot_ref[...].T.astype(o_ref.dtype)


def _attention(qt, qit, wit, k, vt, ki, rel_bias, tq):
    bsz, _, _, seq = qt.shape
    n_sel = min(TOPK_MAX, seq // 4)
    far = _far_distance()
    n_near = min(seq // tq, (far - 1 + tq - 1) // tq + 1)
    i = np.arange(tq)[:, None]
    j = np.arange(tq)[None, :]
    bucket = np.stack([_t5_bucket_np(dlt * tq + j - i) for dlt in range(n_near)]
                      + [np.full((tq, tq), N_BUCKETS - 1, np.int32)])
    tri = (j <= i).astype(np.float32)

    kernel = functools.partial(_attn_kernel, tq=tq, n_sel=n_sel, n_near=n_near)
    qblk = lambda n, w: pl.BlockSpec((None, n, w, tq), lambda b, t: (b, 0, 0, t))
    return pl.pallas_call(
        kernel,
        grid=(bsz, seq // tq),
        in_specs=[qblk(N_HEADS, HEAD_DIM), qblk(IDX_HEADS, IDX_DIM),
                  pl.BlockSpec((None, IDX_HEADS, tq), lambda b, t: (b, 0, t)),
                  pl.BlockSpec((None, N_KV_HEADS, seq, HEAD_DIM), lambda b, t: (b, 0, 0, 0)),
                  pl.BlockSpec((None, N_KV_HEADS, HEAD_DIM, seq), lambda b, t: (b, 0, 0, 0)),
                  pl.BlockSpec((None, seq, IDX_DIM), lambda b, t: (b, 0, 0)),
                  _resident((n_near + 1, tq, tq)),
                  pl.BlockSpec(memory_space=pltpu.SMEM),
                  _resident((tq, tq))],
        out_specs=pl.BlockSpec((None, tq, N_HEADS * HEAD_DIM), lambda b, t: (b, t, 0)),
        out_shape=jax.ShapeDtypeStruct((bsz, seq, N_HEADS * HEAD_DIM), BF16),
        scratch_shapes=[pltpu.VMEM((seq, tq), jnp.int32),
                        pltpu.VMEM((seq, tq), F32),
                        pltpu.VMEM(((n_near + 1) * N_HEADS, tq, tq), F32),
                        pltpu.VMEM((N_HEADS, SUBLANES, tq), F32),
                        pltpu.VMEM((N_HEADS, HEAD_DIM + SUBLANES, tq), F32),
                        pltpu.VMEM((N_HEADS * HEAD_DIM, tq), F32),
                        pltpu.VMEM((IDX_HEADS, tq, tq), F32),
                        pltpu.VMEM((IDX_HEADS, tq, tq), F32),
                        pltpu.VMEM((LOGIT_BUFS, tq, tq), BF16),
                        pltpu.VMEM((N_HEADS, tq, tq), BF16),
                        pltpu.VMEM((4, SUBLANES, tq), jnp.int32)],
        compiler_params=_cparams(2),
        name="attention",
    )(qt, qit, wit, k, vt, ki, jnp.asarray(bucket), rel_bias, jnp.asarray(tri, BF16))


N_SEG = SUBLANES


def _seg_pitch(seg_len):
    quarter = seg_len // 4 + 1
    if quarter % 2 == 0:
        quarter += 1
    return 4 * quarter


def _rglru_kernel(xr_ref, gy_ref, cw_ref, cb_ref, wrg_ref, brg_ref, wig_ref, big_ref, lam_ref,
                  o_ref, xp_ref, a_ref, u_ref, *, seq, n_grp):
    seg = seq // N_SEG
    pitch = _seg_pitch(seg)
    pad = SUBLANES

    xp_ref[0:pad, :] = jnp.zeros((pad, LANES), F32)
    for g in range(n_grp):
        ls = slice(g * LANES, (g + 1) * LANES)
        xp_ref[pad:pad + seq, :] = xr_ref[:, ls].astype(F32)
        cw = cw_ref[:, ls]
        xc = cb_ref[:, ls] + cw[0:1] * xp_ref[pl.ds(pad - 3, seq), :]
        for kk in range(1, RNN_CONV):
            xc = xc + cw[kk:kk + 1] * xp_ref[pl.ds(pad - 3 + kk, seq), :]
        xb = xc.astype(BF16)
        r = jax.nn.sigmoid(_dot(xb, wrg_ref[g]) + brg_ref[:, ls])
        ig = jax.nn.sigmoid(_dot(xb, wig_ref[g]) + big_ref[:, ls])
        z = -lam_ref[:, ls]
        softplus = jnp.maximum(z, 0.0) + jnp.log1p(jnp.exp(-jnp.abs(z)))
        log_a = (-LRU_C * softplus) * r
        a = jnp.exp(log_a)
        gain2 = -jnp.tanh(log_a) * (1.0 + a * a)
        u = xc * ig * jnp.where(gain2 > 0.0, gain2 * lax.rsqrt(gain2), 0.0)
        for j in range(N_SEG):
            a_ref[g, j * pitch:j * pitch + seg, :] = a[j * seg:(j + 1) * seg]
            u_ref[g, j * pitch:j * pitch + seg, :] = u[j * seg:(j + 1) * seg]

    def step(t, st):
        new = []
        for g in range(n_grp):
            h, p = st[g]
            a_t = a_ref[g, pl.ds(t, N_SEG, stride=pitch), :]
            u_t = u_ref[g, pl.ds(t, N_SEG, stride=pitch), :]
            h = a_t * h + u_t
            p = p * a_t
            u_ref[g, pl.ds(t, N_SEG, stride=pitch), :] = h
            a_ref[g, pl.ds(t, N_SEG, stride=pitch), :] = p
            new.append((h, p))
        return tuple(new)

    init = tuple((jnp.zeros((N_SEG, LANES), F32), jnp.ones((N_SEG, LANES), F32)) for _ in range(n_grp))
    final = lax.fori_loop(0, seg, step, init)

    for g in range(n_grp):
        ls = slice(g * LANES, (g + 1) * LANES)
        h_end, p_end = final[g]
        carry = jnp.zeros((1, LANES), F32)
        for j in range(N_SEG):
            rows = slice(j * pitch, j * pitch + seg)
            hj = u_ref[g, rows, :] + a_ref[g, rows, :] * carry
            o_ref[j * seg:(j + 1) * seg, ls] = (hj * gy_ref[j * seg:(j + 1) * seg, ls].astype(F32)).astype(o_ref.dtype)
            carry = h_end[j:j + 1] + p_end[j:j + 1] * carry


def _rglru(xr, gy, cw, cb, wrg, brg, wig, big, lam, cblk):
    bsz, seq, d = xr.shape
    n_grp = cblk // LANES
    pitch = _seg_pitch(seq // N_SEG)
    kernel = functools.partial(_rglru_kernel, seq=seq, n_grp=n_grp)
    act = pl.BlockSpec((None, seq, cblk), lambda b, c: (b, 0, c))
    vec = lambda r: pl.BlockSpec((r, cblk), lambda b, c: (0, c))
    gate_w = pl.BlockSpec((n_grp, LANES, LANES), lambda b, c: (c, 0, 0))
    return pl.pallas_call(
        kernel,
        grid=(bsz, d // cblk),
        in_specs=[act, act, vec(RNN_CONV), vec(1), gate_w, vec(1), gate_w, vec(1), vec(1)],
        out_specs=act,
        out_shape=jax.ShapeDtypeStruct((bsz, seq, d), BF16),
        scratch_shapes=[pltpu.VMEM((seq + SUBLANES, LANES), F32),
                        pltpu.VMEM((n_grp, N_SEG * pitch, LANES), F32),
                        pltpu.VMEM((n_grp, N_SEG * pitch, LANES), F32)],
        compiler_params=_cparams(2),
        name="rglru",
    )(xr, gy, cw, cb, wrg, brg, wig, big, lam)


def _merge_ffn_kernel(x_ref, attn_ref, rnn_ref, sga_ref, sgb_ref, ga1_ref, woa_ref, wor_ref, wout_ref,
                      sc_ref, sh_ref, ga2_ref, g_ref, gf_ref, wv_ref, wg_ref, cwv_ref, cwg_ref,
                      cbv_ref, cbg_ref, wd_ref, o_ref, buf_ref, carry_ref, act_ref, *, tile, fchunk, n_fchunks):
    s = pl.program_id(1)
    pad = SUBLANES

    @pl.when(s == 0)
    def _():
        carry_ref[...] = jnp.zeros(carry_ref.shape, F32)

    merged = (sga_ref[...].astype(F32) * _dot(attn_ref[...], woa_ref[...])
              + sgb_ref[...].astype(F32) * _dot(rnn_ref[...], wor_ref[...]))
    hres = x_ref[...] + ga1_ref[...] * _dot(merged.astype(BF16), wout_ref[...])
    xn = _modulated_norm(hres, g_ref[...], sc_ref[...], sh_ref[...]).astype(BF16)

    def conv(up, idx, cw, cb):
        buf_ref[0:pad, :] = carry_ref[idx]
        buf_ref[pad:pad + tile, :] = up
        carry_ref[idx] = up[tile - pad:tile]
        y = cb + cw[FFN_CONV - 1:FFN_CONV] * up
        for kk in range(FFN_CONV - 1):
            y = y + cw[kk:kk + 1] * buf_ref[pl.ds(pad - (FFN_CONV - 1) + kk, tile), :]
        return y

    for c in range(n_fchunks):
        cs = slice(c * fchunk, (c + 1) * fchunk)
        val = conv(_dot(xn, wv_ref[:, cs]), 2 * c, cwv_ref[:, cs], cbv_ref[:, cs])
        gte = conv(_dot(xn, wg_ref[:, cs]), 2 * c + 1, cwg_ref[:, cs], cbg_ref[:, cs])
        act_ref[:, cs] = ((gte * jax.nn.sigmoid(gte)) * val).astype(BF16)

    h2 = hres + ga2_ref[...] * _dot(act_ref[...], wd_ref[...])
    ms = jnp.mean(h2 * h2, axis=-1, keepdims=True)
    o_ref[...] = h2 * lax.rsqrt(ms + EPS) * gf_ref[...]


def _merge_ffn(x, attn, rnn, sga, sgb, ga1, woa, wor, wout, sc, sh, ga2, g, gf,
               wv, wg, cwv, cwg, cbv, cbg, wd, tile, fchunk):
    bsz, seq, d = x.shape
    dff = wv.shape[1]
    n_fchunks = dff // fchunk
    kernel = functools.partial(_merge_ffn_kernel, tile=tile, fchunk=fchunk, n_fchunks=n_fchunks)
    row = lambda w: pl.BlockSpec((None, tile, w), lambda b, s: (b, s, 0))
    per_batch = pl.BlockSpec((None, 1, d), lambda b, s: (b, 0, 0))
    merge_w = [woa, wor, wout]
    ffn_w = [g, gf, wv, wg, cwv, cwg, cbv, cbg, wd]
    return pl.pallas_call(
        kernel,
        grid=(bsz, seq // tile),
        in_specs=([row(d), row(attn.shape[-1]), row(d), row(d), row(d), per_batch]
                  + [_resident(a.shape) for a in merge_w]
                  + [per_batch, per_batch, per_batch]
                  + [_resident(a.shape) for a in ffn_w]),
        out_specs=row(d),
        out_shape=jax.ShapeDtypeStruct((bsz, seq, d), F32),
        scratch_shapes=[pltpu.VMEM((tile + SUBLANES, fchunk), F32),
                        pltpu.VMEM((2 * n_fchunks, SUBLANES, fchunk), F32),
                        pltpu.VMEM((tile, dff), BF16)],
        compiler_params=_cparams(2),
        name="merge_ffn",
    )(x, attn, rnn, sga, sgb, ga1, *merge_w, sc, sh, ga2, *ffn_w)


def _pick(seq, pref):
    t = min(seq, pref)
    assert seq % t == 0
    return t


def kernel(x, c, w_ada, b_ada, g_mix, w_in, b_in, rel_bias, conv_rnn_w, conv_rnn_b, w_rg, b_rg, w_ig, b_ig, lru_lambda, w_o_attn, w_o_rnn, w_out, g_ffn, w_up, conv_ffn_w, conv_ffn_b, w_down, g_final):
    bsz, seq, d = x.shape
    depth = w_ada.shape[0]
    assert depth == 1 and d == N_RNN_BLOCKS * LANES
    dff = w_down.shape[1]
    row_tile = _pick(seq, 512)
    tq = _pick(seq, 256)

    widths = (N_HEADS * HEAD_DIM, N_KV_HEADS * HEAD_DIM, N_KV_HEADS * HEAD_DIM, IDX_HEADS * IDX_DIM,
              IDX_DIM, IDX_HEADS, d, d, d, d)
    cuts = np.cumsum((0,) + widths)
    col = lambda a, i0, i1: a[..., cuts[i0]:cuts[i1]]

    h = x
    for l in range(depth):
        mod = _adaln(c, w_ada[l], b_ada[l])
        sh1, sc1, ga1, sh2, sc2, ga2 = [m[:, None, :] for m in jnp.split(mod, 6, axis=-1)]

        w, bias = w_in[l], b_in[l]
        t_rows = [col(w, 0, 1), col(w, 3, 4), col(w, 2, 3), col(w, 5, 6)]
        t_bias = [col(bias, 0, 1), col(bias, 3, 4), col(bias, 2, 3), col(bias, 5, 6)]
        n_t = sum(r.shape[1] for r in t_rows)
        t_pad = (-n_t) % (2 * SUBLANES)
        wt = jnp.pad(jnp.concatenate(t_rows, axis=1).T, ((0, t_pad), (0, 0))).astype(BF16)
        bt = jnp.pad(jnp.concatenate(t_bias), (0, t_pad)).reshape(-1, 1)
        k_pad = (-(widths[1] + widths[4])) % LANES
        wk = jnp.pad(jnp.concatenate([col(w, 1, 2), col(w, 4, 5)], axis=1), ((0, 0), (0, k_pad))).astype(BF16)
        bk = jnp.pad(jnp.concatenate([col(bias, 1, 2), col(bias, 4, 5)]), (0, k_pad)).reshape(1, -1)
        weights = [wt, bt, wk, bk]
        for i in range(6, 10):
            weights += [col(w, i, i + 1).astype(BF16), col(bias, i, i + 1).reshape(1, -1)]
        qt, qit, vt, wit, k, ki, xr, gy, sga, sgb = _in_proj(h, sc1, sh1, g_mix[l].reshape(1, d), weights, row_tile)

        attn = _attention(qt, qit, wit, k, vt, ki, rel_bias, tq)
        rnn = _rglru(xr, gy, conv_rnn_w[l], conv_rnn_b[l].reshape(1, d), w_rg[l].astype(BF16),
                     b_rg[l].reshape(1, d), w_ig[l].astype(BF16), b_ig[l].reshape(1, d),
                     lru_lambda[l].reshape(1, d), cblk=min(d, 512))
        wu, cw, cb = w_up[l], conv_ffn_w[l], conv_ffn_b[l].reshape(1, -1)
        h = _merge_ffn(h, attn, rnn, sga, sgb, ga1, w_o_attn[l].astype(BF16), w_o_rnn[l].astype(BF16),
                       w_out[l].astype(BF16), sc2, sh2, ga2, g_ffn[l].reshape(1, d), g_final.reshape(1, d),
                       wu[:, :dff].astype(BF16), wu[:, dff:].astype(BF16), cw[:, :dff], cw[:, dff:],
                       cb[:, :dff], cb[:, dff:], w_down[l].astype(BF16), row_tile, fchunk=256)
    return h
```

```python
import functools
import math

import numpy as np
import jax
import jax.numpy as jnp
from jax import lax
from jax.experimental import pallas as pl
from jax.experimental.pallas import tpu as pltpu

N_HEADS = 8
HEAD_DIM = 64
N_KV_HEADS = 2
IDX_HEADS = 8
IDX_DIM = 64
TOPK_MAX = 256
N_BUCKETS = 32
MAX_DISTANCE = 128
N_RNN_BLOCKS = 8
RNN_CONV = 4
LRU_C = 8.0
FFN_CONV = 3
EPS = 1e-6

LANES = 128
SUBLANES = 8
VMEM_LIMIT = 56 * 1024 * 1024

MASK_NEG = -(2.0 ** 100)
KEY_NEG_INF = -2139095041

BF16 = jnp.bfloat16
F32 = jnp.float32


def _cparams(n_grid):
    return pltpu.CompilerParams(
        dimension_semantics=("arbitrary",) * n_grid, vmem_limit_bytes=VMEM_LIMIT)


def _resident(shape):
    nd = len(shape)
    return pl.BlockSpec(shape, lambda *_: (0,) * nd, pipeline_mode=pl.Buffered(1))


def _dot(a, b):
    return jnp.dot(a, b, preferred_element_type=F32)


def _dot_nt(a, b):
    return lax.dot_general(a, b, (((1,), (1,)), ((), ())), preferred_element_type=F32)


def _adaln_kernel(c_ref, w_ref, b_ref, o_ref):
    c = c_ref[...]
    c_act = (c * jax.nn.sigmoid(c)).astype(BF16)
    o_ref[...] = _dot(c_act, w_ref[...].astype(BF16)) + b_ref[...]


def _adaln(c, w, b):
    bsz, d = c.shape
    n = w.shape[1]
    return pl.pallas_call(
        _adaln_kernel,
        grid=(n // d,),
        in_specs=[pl.BlockSpec((bsz, d), lambda j: (0, 0)),
                  pl.BlockSpec((d, d), lambda j: (0, j)),
                  pl.BlockSpec((1, d), lambda j: (0, j))],
        out_specs=pl.BlockSpec((bsz, d), lambda j: (0, j)),
        out_shape=jax.ShapeDtypeStruct((bsz, n), F32),
        compiler_params=_cparams(1),
        name="adaln",
    )(c, w, b.reshape(1, n))


def _modulated_norm(x, g, sc, sh):
    ms = jnp.mean(x * x, axis=-1, keepdims=True)
    return (x * lax.rsqrt(ms + EPS) * g) * (1.0 + sc) + sh


def _in_proj_kernel(x_ref, sc_ref, sh_ref, g_ref, wt_ref, bt_ref, wk_ref, bk_ref,
                    wr_ref, br_ref, wy_ref, by_ref, wga_ref, bga_ref, wgb_ref, bgb_ref,
                    qt_ref, qit_ref, vt_ref, wit_ref, k_ref, ki_ref, xr_ref, gy_ref, sga_ref, sgb_ref):
    xn = _modulated_norm(x_ref[...], g_ref[...], sc_ref[...], sh_ref[...]).astype(BF16)

    res = _dot_nt(wt_ref[...], xn) + bt_ref[...]
    r0 = 0
    for h in range(N_HEADS):
        qt_ref[h] = (res[r0 + h * HEAD_DIM:r0 + (h + 1) * HEAD_DIM] * (HEAD_DIM ** -0.5)).astype(BF16)
    r0 += N_HEADS * HEAD_DIM
    for h in range(IDX_HEADS):
        qit_ref[h] = (res[r0 + h * IDX_DIM:r0 + (h + 1) * IDX_DIM] * (IDX_DIM ** -0.5)).astype(BF16)
    r0 += IDX_HEADS * IDX_DIM
    for j in range(N_KV_HEADS):
        vt_ref[j] = res[r0 + j * HEAD_DIM:r0 + (j + 1) * HEAD_DIM].astype(BF16)
    r0 += N_KV_HEADS * HEAD_DIM
    wit_ref[...] = res[r0:r0 + IDX_HEADS] * (IDX_HEADS ** -0.5)

    res = _dot(xn, wk_ref[...]) + bk_ref[...]
    for j in range(N_KV_HEADS):
        k_ref[j] = res[:, j * HEAD_DIM:(j + 1) * HEAD_DIM].astype(BF16)
    ki0 = N_KV_HEADS * HEAD_DIM
    ki_ref[...] = res[:, ki0:ki0 + IDX_DIM].astype(BF16)

    xr_ref[...] = (_dot(xn, wr_ref[...]) + br_ref[...]).astype(BF16)
    gy_ref[...] = jax.nn.gelu(_dot(xn, wy_ref[...]) + by_ref[...]).astype(BF16)
    sga_ref[...] = jax.nn.sigmoid(_dot(xn, wga_ref[...]) + bga_ref[...]).astype(BF16)
    sgb_ref[...] = jax.nn.sigmoid(_dot(xn, wgb_ref[...]) + bgb_ref[...]).astype(BF16)


def _in_proj(x, sc, sh, g, weights, tile):
    bsz, seq, d = x.shape
    row = lambda w: pl.BlockSpec((None, tile, w), lambda b, s: (b, s, 0))
    heads = lambda n, w: pl.BlockSpec((None, n, tile, w), lambda b, s: (b, 0, s, 0))
    heads_t = lambda n, w: pl.BlockSpec((None, n, w, tile), lambda b, s: (b, 0, 0, s))
    per_batch = pl.BlockSpec((None, 1, d), lambda b, s: (b, 0, 0))
    in_specs = [row(d), per_batch, per_batch, _resident((1, d))]
    in_specs += [_resident(w.shape) for w in weights]
    bsd = lambda w, dt: jax.ShapeDtypeStruct((bsz, seq, w), dt)
    out_shape = [
        jax.ShapeDtypeStruct((bsz, N_HEADS, HEAD_DIM, seq), BF16),
        jax.ShapeDtypeStruct((bsz, IDX_HEADS, IDX_DIM, seq), BF16),
        jax.ShapeDtypeStruct((bsz, N_KV_HEADS, HEAD_DIM, seq), BF16),
        jax.ShapeDtypeStruct((bsz, IDX_HEADS, seq), F32),
        jax.ShapeDtypeStruct((bsz, N_KV_HEADS, seq, HEAD_DIM), BF16),
        bsd(IDX_DIM, BF16),
        bsd(d, BF16), bsd(d, BF16), bsd(d, BF16), bsd(d, BF16)]
    out_specs = [heads_t(N_HEADS, HEAD_DIM), heads_t(IDX_HEADS, IDX_DIM), heads_t(N_KV_HEADS, HEAD_DIM),
                 pl.BlockSpec((None, IDX_HEADS, tile), lambda b, s: (b, 0, s)),
                 heads(N_KV_HEADS, HEAD_DIM), row(IDX_DIM),
                 row(d), row(d), row(d), row(d)]
    return pl.pallas_call(
        _in_proj_kernel,
        grid=(bsz, seq // tile),
        in_specs=in_specs, out_specs=out_specs, out_shape=out_shape,
        compiler_params=_cparams(2),
        name="in_proj",
    )(x, sc, sh, g, *weights)


def _t5_bucket_np(rel):
    max_exact = N_BUCKETS // 2
    n = np.maximum(rel, 0)
    nf = np.maximum(n, 1).astype(np.float64)
    large = max_exact + (np.log(nf / max_exact) / math.log(MAX_DISTANCE / max_exact)
                         * (N_BUCKETS - max_exact)).astype(np.int32)
    large = np.minimum(large, N_BUCKETS - 1)
    return np.where(n < max_exact, n, large).astype(np.int32)


def _far_distance():
    d = np.arange(0, 4 * MAX_DISTANCE)
    b = _t5_bucket_np(d)
    assert b[-1] == N_BUCKETS - 1
    return int(np.max(np.nonzero(b != N_BUCKETS - 1)[0])) + 1


def _sort_key(s):
    bits = lax.bitcast_convert_type(s, jnp.int32)
    return bits ^ ((bits >> 31) & jnp.int32(0x7FFFFFFF))


def _key_to_float(k):
    return lax.bitcast_convert_type(k ^ ((k >> 31) & jnp.int32(0x7FFFFFFF)), F32)


VALUE_STEPS = 12
SEARCH_FIXED = 16
SEARCH_UNROLL = 4
SCORE_ROWS = 64
LOGIT_AHEAD = 4
LOGIT_BUFS = 8


def _attn_kernel(qt_ref, qit_ref, wit_ref, k_ref, vt_ref, ki_ref, bucket_ref, relb_ref, tri_ref,
                 o_ref, key_ref, madd_ref, bias_ref, m_ref, acc_ref, ot_ref, d_ref, x_ref, p_ref, stat_ref,
                 *, tq, n_sel, n_near):
    b = pl.program_id(0)
    qt = pl.program_id(1)
    n_chunks = qt + 1
    tk = tq
    int_max = jnp.iinfo(jnp.int32).max

    @pl.when((b == 0) & (qt == 0))
    def _():
        for dlt in range(n_near + 1):
            bucket = bucket_ref[dlt]
            for h in range(N_HEADS):
                tile = jnp.full((tk, tq), relb_ref[N_BUCKETS - 1, h], F32)
                for bk in range(N_BUCKETS - 1):
                    tile = jnp.where(bucket == bk, relb_ref[bk, h], tile)
                bias_ref[dlt * N_HEADS + h] = tile

    t_row = qt * tq + lax.broadcasted_iota(jnp.int32, (1, tq), 1)

    def split(v):
        return v.reshape(v.shape[0] // SUBLANES, SUBLANES, v.shape[1])

    n_rb = tk // SCORE_ROWS
    assert n_rb * SCORE_ROWS == tk
    key_pos = lax.broadcasted_iota(jnp.int32, (SCORE_ROWS, tq), 0)
    qry_pos = lax.broadcasted_iota(jnp.int32, (SCORE_ROWS, tq), 1)

    def idx_dots(c, rb):
        rows = pl.ds(pl.multiple_of(c * tk + rb * SCORE_ROWS, SCORE_ROWS), SCORE_ROWS)
        kic = ki_ref[rows, :]
        for h in range(IDX_HEADS):
            d_ref[rb * IDX_HEADS + h] = _dot(kic, qit_ref[h])

    def score_chunk(c, nxt):
        off = pl.multiple_of(c * tk, tk)
        kmin, kmax, npos, nzero = stat_ref[0], stat_ref[1], stat_ref[2], stat_ref[3]
        for rb in range(n_rb):
            s = None
            for h in range(IDX_HEADS):
                term = wit_ref[h:h + 1, :] * jnp.maximum(d_ref[rb * IDX_HEADS + h], 0.0)
                s = term if s is None else s + term
            idx_dots(nxt, rb)
            valid = (c < qt) | (key_pos + rb * SCORE_ROWS <= qry_pos)
            s = jnp.where(valid, s, -jnp.inf)
            is_zero = s == 0.0
            key = jnp.where(is_zero, 0, _sort_key(s))
            key_ref[pl.ds(pl.multiple_of(off + rb * SCORE_ROWS, SCORE_ROWS), SCORE_ROWS), :] = key
            kmin = jnp.minimum(kmin, jnp.min(split(jnp.where(valid, key, int_max)), axis=0))
            kmax = jnp.maximum(kmax, jnp.max(split(key), axis=0))
            npos = npos + jnp.sum(split(jnp.where(s > 0.0, 1, 0)), axis=0)
            nzero = nzero + jnp.sum(split(jnp.where(is_zero, 1, 0)), axis=0)
        stat_ref[0], stat_ref[1], stat_ref[2], stat_ref[3] = kmin, kmax, npos, nzero

    stat_ref[0] = jnp.full((SUBLANES, tq), int_max, jnp.int32)
    stat_ref[1] = jnp.full((SUBLANES, tq), KEY_NEG_INF, jnp.int32)
    stat_ref[2] = jnp.zeros((SUBLANES, tq), jnp.int32)
    stat_ref[3] = jnp.zeros((SUBLANES, tq), jnp.int32)
    for rb in range(n_rb):
        idx_dots(0, rb)

    def score_step(c, carry):
        score_chunk(c, jnp.minimum(c + 1, qt))
        return carry

    lax.fori_loop(0, n_chunks, score_step, 0)
    kmin, kmax, npos, nzero = stat_ref[0], stat_ref[1], stat_ref[2], stat_ref[3]
    kmin = jnp.min(kmin, axis=0, keepdims=True)
    kmax = jnp.max(kmax, axis=0, keepdims=True)
    npos = jnp.sum(npos, axis=0, keepdims=True)
    nneg0 = npos + jnp.sum(nzero, axis=0, keepdims=True)

    need = jnp.minimum(t_row + 1, n_sel)

    def count_ge(mid):
        mid8 = jnp.broadcast_to(mid, (SUBLANES, tq))
        n_acc = tk // SCORE_ROWS

        def body(c, acc):
            off = pl.multiple_of(c * tk, tk)
            new = []
            for i in range(n_acc):
                rows = pl.ds(pl.multiple_of(off + i * SCORE_ROWS, SCORE_ROWS), SCORE_ROWS)
                new.append(acc[i] + jnp.sum(jnp.where(split(key_ref[rows, :]) >= mid8[None], 1, 0), axis=0))
            return tuple(new)
        acc = lax.fori_loop(0, n_chunks, body,
                            tuple(jnp.zeros((SUBLANES, tq), jnp.int32) for _ in range(n_acc)))
        return jnp.sum(sum(acc), axis=0, keepdims=True)

    def search_cond(st):
        return jnp.min(st[-1]) == 0

    def search_steps(st):
        for _ in range(SEARCH_UNROLL):
            st = search_step(st)
        return st

    def search_step(st):
        it, lo, hi, cnt_lo, cnt_hi, done = st
        half = (lo >> 1) + (hi >> 1) + (lo & hi & 1)
        mid_f = 0.5 * _key_to_float(lo) + 0.5 * _key_to_float(hi)
        by_value = jnp.minimum(jnp.maximum(_sort_key(mid_f), lo + 1), hi - 1)
        mid = jnp.where(it < VALUE_STEPS, by_value, half)
        cnt = count_ge(mid)
        live = done == 0
        up = live & (cnt >= need)
        dn = live & (cnt < need)
        lo = jnp.where(up, mid, lo)
        cnt_lo = jnp.where(up, cnt, cnt_lo)
        hi = jnp.where(dn, mid, hi)
        cnt_hi = jnp.where(dn, cnt, cnt_hi)
        done = jnp.where(stops(lo, hi, cnt_lo, cnt_hi), 1, done)
        return it + 1, lo, hi, cnt_lo, cnt_hi, done

    def stops(lo, hi, cnt_lo, cnt_hi):
        return (cnt_lo == need) | (hi == lo + 1) | (cnt_lo - cnt_hi <= 2)

    above = need <= npos
    below = need > nneg0
    lo0 = jnp.where(above, 1, jnp.where(below, kmin, 0))
    cnt_lo0 = jnp.where(above, npos, jnp.where(below, t_row + 1, nneg0))
    hi0 = jnp.where(above, jnp.minimum(kmax, int_max - 1) + 1, jnp.where(below, 0, 1))
    cnt_hi0 = jnp.where(above, 0, jnp.where(below, nneg0, npos))
    done0 = jnp.where(stops(lo0, hi0, cnt_lo0, cnt_hi0), 1, 0)
    searching = (qt + 1) * tq > n_sel

    state = lax.fori_loop(0, jnp.where(searching, SEARCH_FIXED, 0), lambda _, st: search_step(st),
                          (jnp.int32(0), lo0, hi0, cnt_lo0, cnt_hi0, done0))
    _, lo, hi, cnt_lo, cnt_hi, _ = lax.while_loop(search_cond, search_steps, state)

    pending = (cnt_lo > need) & (hi > lo + 1)
    stat_ref[0] = jnp.broadcast_to(lo, (SUBLANES, tq))
    stat_ref[1] = jnp.broadcast_to(cnt_lo, (SUBLANES, tq))

    @pl.when(searching)
    def _():
        lo8 = jnp.broadcast_to(lo, (SUBLANES, tq))
        hi8 = jnp.broadcast_to(hi, (SUBLANES, tq))

        def body(c, carry):
            big, small = carry
            kc = split(key_ref[pl.ds(pl.multiple_of(c * tk, tk), tk), :])
            big = jnp.maximum(big, jnp.max(jnp.where(kc < hi8[None], kc, KEY_NEG_INF), axis=0))
            small = jnp.minimum(small, jnp.min(jnp.where(kc >= lo8[None], kc, int_max), axis=0))
            return big, small
        big, small = lax.fori_loop(
            0, n_chunks, body,
            (jnp.full((SUBLANES, tq), KEY_NEG_INF, jnp.int32), jnp.full((SUBLANES, tq), int_max, jnp.int32)))
        big = jnp.max(big, axis=0, keepdims=True)
        small = jnp.min(small, axis=0, keepdims=True)
        stat_ref[0] = jnp.broadcast_to(jnp.where(pending, big, lo), (SUBLANES, tq))
        stat_ref[1] = jnp.broadcast_to(
            jnp.where(pending, cnt_hi + jnp.where(big == small, 2, 1), cnt_lo), (SUBLANES, tq))

    lo = stat_ref[0][0:1]
    cnt_lo = stat_ref[1][0:1]

    has_tie = jnp.max(jnp.where(cnt_lo > need, 1, 0)) > 0

    @pl.when(jnp.logical_not(has_tie))
    def _():
        def body(c, carry):
            off = pl.multiple_of(c * tk, tk)
            madd_ref[pl.ds(off, tk), :] = jnp.where(key_ref[pl.ds(off, tk), :] >= lo, 0.0, MASK_NEG)
            return carry
        lax.fori_loop(0, n_chunks, body, 0)

    @pl.when(has_tie)
    def _():
        tri = tri_ref[...]
        quota = jnp.where(cnt_lo > need, need - cnt_hi, n_sel).astype(F32)

        def body(c, seen):
            off = pl.multiple_of(c * tk, tk)
            kc = key_ref[pl.ds(off, tk), :]
            eq = kc == lo
            eqf = jnp.where(eq, 1.0, 0.0)
            rank = _dot(tri, eqf.astype(BF16)) + seen
            sel = (kc > lo) | (eq & (rank <= quota))
            madd_ref[pl.ds(off, tk), :] = jnp.where(sel, 0.0, MASK_NEG)
            return seen + jnp.sum(eqf, axis=0, keepdims=True)
        lax.fori_loop(0, n_chunks, body, jnp.zeros((1, tq), F32))

    m_ref[...] = jnp.full(m_ref.shape, MASK_NEG, F32)
    acc_ref[...] = jnp.zeros(acc_ref.shape, F32)
    group = N_HEADS // N_KV_HEADS
    ones_rows = jnp.ones((2 * SUBLANES, tk), BF16)

    def masked_logits(c, h):
        off = pl.multiple_of(c * tk, tk)
        near = jnp.minimum(qt - c, n_near) * N_HEADS
        return (_dot(k_ref[h // group, pl.ds(off, tk), :], qt_ref[h])
                + madd_ref[pl.ds(off, tk), :] + bias_ref[near + h]).astype(BF16)

    def softmax_pv(c, nxt):
        off = pl.multiple_of(c * tk, tk)
        vt_ext = [jnp.concatenate([vt_ref[kvh, :, pl.ds(off, tk)], ones_rows], axis=0)
                  for kvh in range(N_KV_HEADS)]

        def pv(h, alpha):
            prod = _dot(vt_ext[h // group], p_ref[h])[:HEAD_DIM + SUBLANES]
            acc_ref[h] = (alpha[None] * split(acc_ref[h])).reshape(HEAD_DIM + SUBLANES, tq) + prod

        prev_alpha = None
        for h in range(N_HEADS):
            ahead = h + LOGIT_AHEAD
            x_ref[ahead % LOGIT_BUFS] = (masked_logits(c, ahead) if ahead < N_HEADS
                                         else masked_logits(nxt, ahead - N_HEADS))
            xh = x_ref[h % LOGIT_BUFS]
            packed = 2 * SUBLANES
            mx = jnp.max(xh.reshape(tk // packed, packed, tq), axis=0).astype(F32)
            mx = jnp.maximum(mx[:SUBLANES], mx[SUBLANES:])
            for shift in (4, 2, 1):
                mx = jnp.maximum(mx, pltpu.roll(mx, shift, 0))
            m_old = m_ref[h]
            m_new = jnp.maximum(m_old, mx)
            alpha = jnp.exp(m_old - m_new)
            m_ref[h] = m_new
            p_ref[h] = jnp.exp(x_ref[h % LOGIT_BUFS] - m_new[0:1].astype(BF16))
            if h > 0:
                pv(h - 1, prev_alpha)
            prev_alpha = alpha
        pv(N_HEADS - 1, prev_alpha)

    for h in range(LOGIT_AHEAD):
        x_ref[h] = masked_logits(0, h)

    def attend(c, carry):
        softmax_pv(c, jnp.minimum(c + 1, qt))
        return carry

    lax.fori_loop(0, n_chunks, attend, 0)

    for h in range(N_HEADS):
        ot_ref[h * HEAD_DIM:(h + 1) * HEAD_DIM, :] = (
            acc_ref[h, 0:HEAD_DIM, :] / acc_ref[h, HEAD_DIM:HEAD_DIM + 1, :])
    o_ref[...] = ot_ref[...].T.astype(o_ref.dtype)


def _attention(qt, qit, wit, k, vt, ki, rel_bias, tq):
    bsz, _, _, seq = qt.shape
    n_sel = min(TOPK_MAX, seq // 4)
    far = _far_distance()
    n_near = min(seq // tq, (far - 1 + tq - 1) // tq + 1)
    i = np.arange(tq)[:, None]
    j = np.arange(tq)[None, :]
    bucket = np.stack([_t5_bucket_np(dlt * tq + j - i) for dlt in range(n_near)]
                      + [np.full((tq, tq), N_BUCKETS - 1, np.int32)])
    tri = (j <= i).astype(np.float32)

    kernel = functools.partial(_attn_kernel, tq=tq, n_sel=n_sel, n_near=n_near)
    qblk = lambda n, w: pl.BlockSpec((None, n, w, tq), lambda b, t: (b, 0, 0, t))
    return pl.pallas_call(
        kernel,
        grid=(bsz, seq // tq),
        in_specs=[qblk(N_HEADS, HEAD_DIM), qblk(IDX_HEADS, IDX_DIM),
                  pl.BlockSpec((None, IDX_HEADS, tq), lambda b, t: (b, 0, t)),
                  pl.BlockSpec((None, N_KV_HEADS, seq, HEAD_DIM), lambda b, t: (b, 0, 0, 0)),
                  pl.BlockSpec((None, N_KV_HEADS, HEAD_DIM, seq), lambda b, t: (b, 0, 0, 0)),
                  pl.BlockSpec((None, seq, IDX_DIM), lambda b, t: (b, 0, 0)),
                  _resident((n_near + 1, tq, tq)),
                  pl.BlockSpec(memory_space=pltpu.SMEM),
                  _resident((tq, tq))],
        out_specs=pl.BlockSpec((None, tq, N_HEADS * HEAD_DIM), lambda b, t: (b, t, 0)),
        out_shape=jax.ShapeDtypeStruct((bsz, seq, N_HEADS * HEAD_DIM), BF16),
        scratch_shapes=[pltpu.VMEM((seq, tq), jnp.int32),
                        pltpu.VMEM((seq, tq), F32),
                        pltpu.VMEM(((n_near + 1) * N_HEADS, tq, tq), F32),
                        pltpu.VMEM((N_HEADS, SUBLANES, tq), F32),
                        pltpu.VMEM((N_HEADS, HEAD_DIM + SUBLANES, tq), F32),
                        pltpu.VMEM((N_HEADS * HEAD_DIM, tq), F32),
                        pltpu.VMEM((tq // SCORE_ROWS * IDX_HEADS, SCORE_ROWS, tq), F32),
                        pltpu.VMEM((LOGIT_BUFS, tq, tq), BF16),
                        pltpu.VMEM((N_HEADS, tq, tq), BF16),
                        pltpu.VMEM((4, SUBLANES, tq), jnp.int32)],
        compiler_params=_cparams(2),
        name="attention",
    )(qt, qit, wit, k, vt, ki, jnp.asarray(bucket), rel_bias, jnp.asarray(tri, BF16))


N_SEG = SUBLANES


def _seg_pitch(seg_len):
    quarter = seg_len // 4 + 1
    if quarter % 2 == 0:
        quarter += 1
    return 4 * quarter


def _rglru_kernel(xr_ref, gy_ref, cw_ref, cb_ref, wrg_ref, brg_ref, wig_ref, big_ref, lam_ref,
                  o_ref, xp_ref, a_ref, u_ref, *, seq, n_grp):
    seg = seq // N_SEG
    pitch = _seg_pitch(seg)
    pad = SUBLANES

    xp_ref[0:pad, :] = jnp.zeros((pad, LANES), F32)
    for g in range(n_grp):
        ls = slice(g * LANES, (g + 1) * LANES)
        xp_ref[pad:pad + seq, :] = xr_ref[:, ls].astype(F32)
        cw = cw_ref[:, ls]
        xc = cb_ref[:, ls] + cw[0:1] * xp_ref[pl.ds(pad - 3, seq), :]
        for kk in range(1, RNN_CONV):
            xc = xc + cw[kk:kk + 1] * xp_ref[pl.ds(pad - 3 + kk, seq), :]
        xb = xc.astype(BF16)
        r = jax.nn.sigmoid(_dot(xb, wrg_ref[g]) + brg_ref[:, ls])
        ig = jax.nn.sigmoid(_dot(xb, wig_ref[g]) + big_ref[:, ls])
        z = -lam_ref[:, ls]
        softplus = jnp.maximum(z, 0.0) + jnp.log1p(jnp.exp(-jnp.abs(z)))
        log_a = (-LRU_C * softplus) * r
        a = jnp.exp(log_a)
        gain2 = -jnp.tanh(log_a) * (1.0 + a * a)
        u = xc * ig * jnp.where(gain2 > 0.0, gain2 * lax.rsqrt(gain2), 0.0)
        for j in range(N_SEG):
            a_ref[g, j * pitch:j * pitch + seg, :] = a[j * seg:(j + 1) * seg]
            u_ref[g, j * pitch:j * pitch + seg, :] = u[j * seg:(j + 1) * seg]

    def step(t, st):
        new = []
        for g in range(n_grp):
            h, p = st[g]
            a_t = a_ref[g, pl.ds(t, N_SEG, stride=pitch), :]
            u_t = u_ref[g, pl.ds(t, N_SEG, stride=pitch), :]
            h = a_t * h + u_t
            p = p * a_t
            u_ref[g, pl.ds(t, N_SEG, stride=pitch), :] = h
            a_ref[g, pl.ds(t, N_SEG, stride=pitch), :] = p
            new.append((h, p))
        return tuple(new)

    init = tuple((jnp.zeros((N_SEG, LANES), F32), jnp.ones((N_SEG, LANES), F32)) for _ in range(n_grp))
    final = lax.fori_loop(0, seg, step, init)

    for g in range(n_grp):
        ls = slice(g * LANES, (g + 1) * LANES)
        h_end, p_end = final[g]
        carry = jnp.zeros((1, LANES), F32)
        for j in range(N_SEG):
            rows = slice(j * pitch, j * pitch + seg)
            hj = u_ref[g, rows, :] + a_ref[g, rows, :] * carry
            o_ref[j * seg:(j + 1) * seg, ls] = (hj * gy_ref[j * seg:(j + 1) * seg, ls].astype(F32)).astype(o_ref.dtype)
            carry = h_end[j:j + 1] + p_end[j:j + 1] * carry


def _rglru(xr, gy, cw, cb, wrg, brg, wig, big, lam, cblk):
    bsz, seq, d = xr.shape
    n_grp = cblk // LANES
    pitch = _seg_pitch(seq // N_SEG)
    kernel = functools.partial(_rglru_kernel, seq=seq, n_grp=n_grp)
    act = pl.BlockSpec((None, seq, cblk), lambda b, c: (b, 0, c))
    vec = lambda r: pl.BlockSpec((r, cblk), lambda b, c: (0, c))
    gate_w = pl.BlockSpec((n_grp, LANES, LANES), lambda b, c: (c, 0, 0))
    return pl.pallas_call(
        kernel,
        grid=(bsz, d // cblk),
        in_specs=[act, act, vec(RNN_CONV), vec(1), gate_w, vec(1), gate_w, vec(1), vec(1)],
        out_specs=act,
        out_shape=jax.ShapeDtypeStruct((bsz, seq, d), BF16),
        scratch_shapes=[pltpu.VMEM((seq + SUBLANES, LANES), F32),
                        pltpu.VMEM((n_grp, N_SEG * pitch, LANES), F32),
                        pltpu.VMEM((n_grp, N_SEG * pitch, LANES), F32)],
        compiler_params=_cparams(2),
        name="rglru",
    )(xr, gy, cw, cb, wrg, brg, wig, big, lam)


def _merge_ffn_kernel(x_ref, attn_ref, rnn_ref, sga_ref, sgb_ref, ga1_ref, woa_ref, wor_ref, wout_ref,
                      sc_ref, sh_ref, ga2_ref, g_ref, gf_ref, wv_ref, wg_ref, cwv_ref, cwg_ref,
                      cbv_ref, cbg_ref, wd_ref, o_ref, buf_ref, carry_ref, act_ref, *, tile, fchunk, n_fchunks):
    s = pl.program_id(1)
    pad = SUBLANES

    @pl.when(s == 0)
    def _():
        carry_ref[...] = jnp.zeros(carry_ref.shape, F32)

    merged = (sga_ref[...].astype(F32) * _dot(attn_ref[...], woa_ref[...])
              + sgb_ref[...].astype(F32) * _dot(rnn_ref[...], wor_ref[...]))
    hres = x_ref[...] + ga1_ref[...] * _dot(merged.astype(BF16), wout_ref[...])
    xn = _modulated_norm(hres, g_ref[...], sc_ref[...], sh_ref[...]).astype(BF16)

    def conv(up, idx, cw, cb):
        buf_ref[0:pad, :] = carry_ref[idx]
        buf_ref[pad:pad + tile, :] = up
        carry_ref[idx] = up[tile - pad:tile]
        y = cb + cw[FFN_CONV - 1:FFN_CONV] * up
        for kk in range(FFN_CONV - 1):
            y = y + cw[kk:kk + 1] * buf_ref[pl.ds(pad - (FFN_CONV - 1) + kk, tile), :]
        return y

    for c in range(n_fchunks):
        cs = slice(c * fchunk, (c + 1) * fchunk)
        val = conv(_dot(xn, wv_ref[:, cs]), 2 * c, cwv_ref[:, cs], cbv_ref[:, cs])
        gte = conv(_dot(xn, wg_ref[:, cs]), 2 * c + 1, cwg_ref[:, cs], cbg_ref[:, cs])
        act_ref[:, cs] = ((gte * jax.nn.sigmoid(gte)) * val).astype(BF16)

    h2 = hres + ga2_ref[...] * _dot(act_ref[...], wd_ref[...])
    ms = jnp.mean(h2 * h2, axis=-1, keepdims=True)
    o_ref[...] = h2 * lax.rsqrt(ms + EPS) * gf_ref[...]


def _merge_ffn(x, attn, rnn, sga, sgb, ga1, woa, wor, wout, sc, sh, ga2, g, gf,
               wv, wg, cwv, cwg, cbv, cbg, wd, tile, fchunk):
    bsz, seq, d = x.shape
    dff = wv.shape[1]
    n_fchunks = dff // fchunk
    kernel = functools.partial(_merge_ffn_kernel, tile=tile, fchunk=fchunk, n_fchunks=n_fchunks)
    row = lambda w: pl.BlockSpec((None, tile, w), lambda b, s: (b, s, 0))
    per_batch = pl.BlockSpec((None, 1, d), lambda b, s: (b, 0, 0))
    merge_w = [woa, wor, wout]
    ffn_w = [g, gf, wv, wg, cwv, cwg, cbv, cbg, wd]
    return pl.pallas_call(
        kernel,
        grid=(bsz, seq // tile),
        in_specs=([row(d), row(attn.shape[-1]), row(d), row(d), row(d), per_batch]
                  + [_resident(a.shape) for a in merge_w]
                  + [per_batch, per_batch, per_batch]
                  + [_resident(a.shape) for a in ffn_w]),
        out_specs=row(d),
        out_shape=jax.ShapeDtypeStruct((bsz, seq, d), F32),
        scratch_shapes=[pltpu.VMEM((tile + SUBLANES, fchunk), F32),
                        pltpu.VMEM((2 * n_fchunks, SUBLANES, fchunk), F32),
                        pltpu.VMEM((tile, dff), BF16)],
        compiler_params=_cparams(2),
        name="merge_ffn",
    )(x, attn, rnn, sga, sgb, ga1, *merge_w, sc, sh, ga2, *ffn_w)


def _pick(seq, pref):
    t = min(seq, pref)
    assert seq % t == 0
    return t


def kernel(x, c, w_ada, b_ada, g_mix, w_in, b_in, rel_bias, conv_rnn_w, conv_rnn_b, w_rg, b_rg, w_ig, b_ig, lru_lambda, w_o_attn, w_o_rnn, w_out, g_ffn, w_up, conv_ffn_w, conv_ffn_b, w_down, g_final):
    bsz, seq, d = x.shape
    depth = w_ada.shape[0]
    assert depth == 1 and d == N_RNN_BLOCKS * LANES
    dff = w_down.shape[1]
    row_tile = _pick(seq, 512)
    tq = _pick(seq, 256)

    widths = (N_HEADS * HEAD_DIM, N_KV_HEADS * HEAD_DIM, N_KV_HEADS * HEAD_DIM, IDX_HEADS * IDX_DIM,
              IDX_DIM, IDX_HEADS, d, d, d, d)
    cuts = np.cumsum((0,) + widths)
    col = lambda a, i0, i1: a[..., cuts[i0]:cuts[i1]]

    h = x
    for l in range(depth):
        mod = _adaln(c, w_ada[l], b_ada[l])
        sh1, sc1, ga1, sh2, sc2, ga2 = [m[:, None, :] for m in jnp.split(mod, 6, axis=-1)]

        w, bias = w_in[l], b_in[l]
        t_rows = [col(w, 0, 1), col(w, 3, 4), col(w, 2, 3), col(w, 5, 6)]
        t_bias = [col(bias, 0, 1), col(bias, 3, 4), col(bias, 2, 3), col(bias, 5, 6)]
        n_t = sum(r.shape[1] for r in t_rows)
        t_pad = (-n_t) % (2 * SUBLANES)
        wt = jnp.pad(jnp.concatenate(t_rows, axis=1).T, ((0, t_pad), (0, 0))).astype(BF16)
        bt = jnp.pad(jnp.concatenate(t_bias), (0, t_pad)).reshape(-1, 1)
        k_pad = (-(widths[1] + widths[4])) % LANES
        wk = jnp.pad(jnp.concatenate([col(w, 1, 2), col(w, 4, 5)], axis=1), ((0, 0), (0, k_pad))).astype(BF16)
        bk = jnp.pad(jnp.concatenate([col(bias, 1, 2), col(bias, 4, 5)]), (0, k_pad)).reshape(1, -1)
        weights = [wt, bt, wk, bk]
        for i in range(6, 10):
            weights += [col(w, i, i + 1).astype(BF16), col(bias, i, i + 1).reshape(1, -1)]
        qt, qit, vt, wit, k, ki, xr, gy, sga, sgb = _in_proj(h, sc1, sh1, g_mix[l].reshape(1, d), weights, row_tile)

        attn = _attention(qt, qit, wit, k, vt, ki, rel_bias, tq)
        rnn = _rglru(xr, gy, conv_rnn_w[l], conv_rnn_b[l].reshape(1, d), w_rg[l].astype(BF16),
                     b_rg[l].reshape(1, d), w_ig[l].astype(BF16), b_ig[l].reshape(1, d),
                     lru_lambda[l].reshape(1, d), cblk=min(d, 512))
        wu, cw, cb = w_up[l], conv_ffn_w[l], conv_ffn_b[l].reshape(1, -1)
        h = _merge_ffn(h, attn, rnn, sga, sgb, ga1, w_o_attn[l].astype(BF16), w_o_rnn[l].astype(BF16),
                       w_out[l].astype(BF16), sc2, sh2, ga2, g_ffn[l].reshape(1, d), g_final.reshape(1, d),
                       wu[:, :dff].astype(BF16), wu[:, dff:].astype(BF16), cw[:, :dff], cw[:, dff:],
                       cb[:, :dff], cb[:, dff:], w_down[l].astype(BF16), row_tile, fchunk=256)
    return h
```
